```python
import jax, jax.numpy as jnp
from jax import lax
import numpy as np

D_MODEL = 1024
BATCH = 32
SEQ = 256
DEPTH = 1
DEC_BATCH = 8
DEC_SEQ = 4096
PAST_LEN = 256

GRID_W = 64
MIX_WIDTH = D_MODEL
LRU_WIDTH = MIX_WIDTH // 2
LRU_BLOCKS = 8
LRU_BLOCK_W = LRU_WIDTH // LRU_BLOCKS
LRU_CONV_WIDTH = 4
LRU_CONV_PAD_LEFT = 2
LRU_C = 8.0
RET_WIDTH = MIX_WIDTH - LRU_WIDTH
RET_HEADS = 4
RET_HEAD_DIM = RET_WIDTH // RET_HEADS
RET_CHUNK = 128
IN_WIDTH = 2 * LRU_WIDTH + 4 * RET_WIDTH
FFN_HIDDEN = ((8 * D_MODEL // 3 + 127) // 128) * 128
FFN_CONV_WIDTH = 3
N_MOD = 6
EPS = 1e-6

kernel_name = 'hymba_rglru_retention_convffn_dit_step'


def _rmsnorm(x, w):
    x32 = x.astype(jnp.float32)
    y = x32 * lax.rsqrt(jnp.mean(x32 * x32, axis=-1, keepdims=True) + EPS)
    return (y * w.astype(jnp.float32)).astype(x.dtype)


def _group_norm_heads(o):
    mu = jnp.mean(o, axis=-1, keepdims=True)
    var = jnp.mean(jnp.square(o - mu), axis=-1, keepdims=True)
    return (o - mu) * lax.rsqrt(var + EPS)


def _modulation(cvec, w, b):
    return (jax.nn.silu(cvec) @ w + b)[:, None, :]


def _dwconv1d(x, w, b, pad_left):
    width = w.shape[0]
    t = x.shape[1]
    xp = jnp.pad(x, ((0, 0), (pad_left, width - 1 - pad_left), (0, 0)))
    return sum(xp[:, j:j + t] * w[j] for j in range(width)) + b


def _dwconv2d_grid(x, w, b):
    bsz, t, ch = x.shape
    rows = t // GRID_W
    g = x.reshape(bsz, rows, GRID_W, ch)
    y = lax.conv_general_dilated(g, w[:, :, None, :], (1, 1), 'SAME',
                                 dimension_numbers=('NHWC', 'HWIO', 'NHWC'),
                                 feature_group_count=ch)
    return y.reshape(bsz, t, ch) + b


def _linear_combine(e1, e2):
    a1, b1 = e1
    a2, b2 = e2
    return a1 * a2, a2 * b1 + b2


def _rglru_dir(x, w_a, b_a, w_x, b_x, lam, h0, reverse):
    bsz, t, wdt = x.shape
    xb = x.reshape(bsz, t, LRU_BLOCKS, LRU_BLOCK_W)
    r = jax.nn.sigmoid(jnp.einsum('btnc,ncd->btnd', xb, w_a).reshape(bsz, t, wdt) + b_a)
    i = jax.nn.sigmoid(jnp.einsum('btnc,ncd->btnd', xb, w_x).reshape(bsz, t, wdt) + b_x)
    log_a = -LRU_C * r * jax.nn.softplus(-lam.astype(jnp.float32))
    a = jnp.exp(log_a)
    u = jnp.sqrt(-jnp.expm1(2.0 * log_a)) * (i * x)
    edge = -1 if reverse else 0
    u = u.at[:, edge].add(a[:, edge] * h0)
    _, h = lax.associative_scan(_linear_combine, (a, u), axis=1, reverse=reverse)
    return h, h[:, edge]


def _retention_dir(q, k, v, log_gamma, s0):
    bsz, t, nh, dh = q.shape
    n = t // RET_CHUNK
    q = q.reshape(bsz, n, RET_CHUNK, nh, dh)
    k = k.reshape(bsz, n, RET_CHUNK, nh, dh)
    v = v.reshape(bsz, n, RET_CHUNK, nh, dh)
    pos = jnp.arange(RET_CHUNK, dtype=jnp.float32)
    rel = pos[:, None] - pos[None, :]
    lg = log_gamma[:, None, None]
    intra_decay = jnp.where(rel >= 0, jnp.exp(lg * jnp.maximum(rel, 0.0)), 0.0)
    scores = jnp.einsum('bnihd,bnjhd->bnhij', q, k) * intra_decay
    intra = jnp.einsum('bnhij,bnjhe->bnihe', scores, v)
    tail = jnp.exp(log_gamma[:, None] * (RET_CHUNK - 1 - pos))
    head = jnp.exp(log_gamma[:, None] * (pos + 1.0))
    chunk_kv = jnp.einsum('bnjhd,bnjhe,hj->nbhde', k, v, tail)
    g_chunk = jnp.exp(log_gamma * RET_CHUNK)[:, None, None]

    def step(s, kv):
        return g_chunk * s + kv, s

    s_final, s_before = lax.scan(step, s0, chunk_kv)
    cross = jnp.einsum('bnihd,nbhde,hi->bnihe', q, s_before, head)
    return (intra + cross).reshape(bsz, t, nh, dh), s_final


def _retention_bidir(q, k, v, lg_fw, lg_bw, s_fw, s_bw):
    o_fw, sf = _retention_dir(q, k, v, lg_fw, s_fw)
    o_bw, sb = _retention_dir(q[:, ::-1], k[:, ::-1], v[:, ::-1], lg_bw, s_bw)
    return o_fw + o_bw[:, ::-1], sf, sb


def _layer(x, mod, p, st, latent):
    bsz, t, _ = x.shape
    shift1, scale1, gate1, shift2, scale2, gate2 = jnp.split(mod, N_MOD, axis=-1)
    h = _rmsnorm(x, p['norm1_w']) * (1.0 + scale1) + shift1
    proj = h @ p['w_in']
    L, R = LRU_WIDTH, RET_WIDTH
    lru_x, lru_g, q, k, v, ret_g = jnp.split(
        proj, [L, 2 * L, 2 * L + R, 2 * L + 2 * R, 2 * L + 3 * R], axis=-1)
    xc = _dwconv1d(lru_x, p['lru_conv_w'], p['lru_conv_b'], LRU_CONV_PAD_LEFT).astype(jnp.float32)
    h_fw, s_lru_fw = _rglru_dir(xc, p['lru_wa_fw'], p['lru_ba_fw'], p['lru_wx_fw'], p['lru_bx_fw'],
                                p['lru_lambda_fw'], st[0], False)
    h_bw, s_lru_bw = _rglru_dir(xc, p['lru_wa_bw'], p['lru_ba_bw'], p['lru_wx_bw'], p['lru_bx_bw'],
                                p['lru_lambda_bw'], st[1], True)
    y_lru = (h_fw + h_bw).astype(x.dtype) * jax.nn.gelu(lru_g)
    qh = q.reshape(bsz, t, RET_HEADS, RET_HEAD_DIM).astype(jnp.float32)
    kh = k.reshape(bsz, t, RET_HEADS, RET_HEAD_DIM).astype(jnp.float32) * (RET_HEAD_DIM ** -0.5)
    vh = v.reshape(bsz, t, RET_HEADS, RET_HEAD_DIM).astype(jnp.float32)
    lg_fw = jax.nn.log_sigmoid(p['ret_decay_fw'].astype(jnp.float32))
    lg_bw = jax.nn.log_sigmoid(p['ret_decay_bw'].astype(jnp.float32))
    o, s_ret_fw, s_ret_bw = _retention_bidir(qh, kh, vh, lg_fw, lg_bw, st[2], st[3])
    o = _group_norm_heads(o).reshape(bsz, t, RET_WIDTH) * p['ret_gn_w'].astype(jnp.float32)
    y_ret = o.astype(x.dtype) * jax.nn.silu(ret_g)
    x = x + gate1 * (jnp.concatenate([y_lru, y_ret], axis=-1) @ p['w_out'])
    h = _rmsnorm(x, p['norm2_w']) * (1.0 + scale2) + shift2
    g = h @ p['ffn_w_gate']
    if latent:
        g = _dwconv2d_grid(g, p['ffn_conv_w'], p['ffn_conv_b'])
    else:
        g = _dwconv1d(g, p['ffn_conv_w'][1], p['ffn_conv_b'], 1)
    x = x + gate2 * ((jax.nn.gelu(g) * (h @ p['ffn_w_up'])) @ p['ffn_w_down'])
    finals = (s_lru_fw.astype(x.dtype), s_lru_bw.astype(x.dtype),
              s_ret_fw.astype(x.dtype), s_ret_bw.astype(x.dtype))
    return x, finals


def setup_inputs(seed: int = 0) -> dict:
    key = jax.random.key(seed)
    ks = iter(jax.random.split(key, 48))

    def nrm(shape, scale):
        return scale * jax.random.normal(next(ks), shape, jnp.float32)

    def lru_lambda():
        a = jax.random.uniform(next(ks), (DEPTH, LRU_WIDTH), jnp.float32, 0.9, 0.999)
        pr = a ** (1.0 / LRU_C)
        return jnp.log(pr) - jnp.log1p(-pr)

    ret_base = jnp.log(2.0 ** (5.0 + jnp.arange(RET_HEADS, dtype=jnp.float32)) - 1.0)
    return {
        'x_prompt': nrm((BATCH, SEQ, D_MODEL), 1.0),
        'x_sample': nrm((DEC_BATCH, DEC_SEQ, D_MODEL), 1.0),
        'state_lru_fw': nrm((DEC_BATCH, DEPTH, LRU_WIDTH), 0.5),
        'state_lru_bw': nrm((DEC_BATCH, DEPTH, LRU_WIDTH), 0.5),
        'state_ret_fw': nrm((DEC_BATCH, DEPTH, RET_HEADS, RET_HEAD_DIM, RET_HEAD_DIM), 0.3),
        'state_ret_bw': nrm((DEC_BATCH, DEPTH, RET_HEADS, RET_HEAD_DIM, RET_HEAD_DIM), 0.3),
        'c': nrm((DEC_BATCH, D_MODEL), 1.0),
        'c_ctx': nrm((D_MODEL,), 1.0),
        'norm1_w': 1.0 + nrm((DEPTH, D_MODEL), 0.02),
        'w_mod': nrm((DEPTH, D_MODEL, N_MOD * D_MODEL), 0.5 * D_MODEL ** -0.5),
        'b_mod': nrm((DEPTH, N_MOD * D_MODEL), 0.02),
        'w_in': nrm((DEPTH, D_MODEL, IN_WIDTH), D_MODEL ** -0.5),
        'lru_conv_w': nrm((DEPTH, LRU_CONV_WIDTH, LRU_WIDTH), LRU_CONV_WIDTH ** -0.5),
        'lru_conv_b': nrm((DEPTH, LRU_WIDTH), 0.02),
        'lru_wa_fw': nrm((DEPTH, LRU_BLOCKS, LRU_BLOCK_W, LRU_BLOCK_W), LRU_BLOCK_W ** -0.5),
        'lru_ba_fw': nrm((DEPTH, LRU_WIDTH), 0.02),
        'lru_wx_fw': nrm((DEPTH, LRU_BLOCKS, LRU_BLOCK_W, LRU_BLOCK_W), LRU_BLOCK_W ** -0.5),
        'lru_bx_fw': nrm((DEPTH, LRU_WIDTH), 0.02),
        'lru_lambda_fw': lru_lambda(),
        'lru_wa_bw': nrm((DEPTH, LRU_BLOCKS, LRU_BLOCK_W, LRU_BLOCK_W), LRU_BLOCK_W ** -0.5),
        'lru_ba_bw': nrm((DEPTH, LRU_WIDTH), 0.02),
        'lru_wx_bw': nrm((DEPTH, LRU_BLOCKS, LRU_BLOCK_W, LRU_BLOCK_W), LRU_BLOCK_W ** -0.5),
        'lru_bx_bw': nrm((DEPTH, LRU_WIDTH), 0.02),
        'lru_lambda_bw': lru_lambda(),
        'ret_decay_fw': ret_base[None, :] + nrm((DEPTH, RET_HEADS), 0.1),
        'ret_decay_bw': ret_base[None, :] + nrm((DEPTH, RET_HEADS), 0.1),
        'ret_gn_w': 1.0 + nrm((DEPTH, RET_WIDTH), 0.02),
        'w_out': nrm((DEPTH, MIX_WIDTH, D_MODEL), MIX_WIDTH ** -0.5),
        'norm2_w': 1.0 + nrm((DEPTH, D_MODEL), 0.02),
        'ffn_w_gate': nrm((DEPTH, D_MODEL, FFN_HIDDEN), D_MODEL ** -0.5),
        'ffn_w_up': nrm((DEPTH, D_MODEL, FFN_HIDDEN), D_MODEL ** -0.5),
        'ffn_conv_w': nrm((DEPTH, FFN_CONV_WIDTH, FFN_CONV_WIDTH, FFN_HIDDEN), 1.0 / FFN_CONV_WIDTH),
        'ffn_conv_b': nrm((DEPTH, FFN_HIDDEN), 0.02),
        'ffn_w_down': nrm((DEPTH, FFN_HIDDEN, D_MODEL), FFN_HIDDEN ** -0.5),
        'final_norm_w': 1.0 + nrm((D_MODEL,), 0.02),
    }


def reference(x_prompt, x_sample, state_lru_fw, state_lru_bw, state_ret_fw, state_ret_bw, c, c_ctx,
              norm1_w, w_mod, b_mod, w_in, lru_conv_w, lru_conv_b,
              lru_wa_fw, lru_ba_fw, lru_wx_fw, lru_bx_fw, lru_lambda_fw,
              lru_wa_bw, lru_ba_bw, lru_wx_bw, lru_bx_bw, lru_lambda_bw,
              ret_decay_fw, ret_decay_bw, ret_gn_w, w_out, norm2_w,
              ffn_w_gate, ffn_w_up, ffn_conv_w, ffn_conv_b, ffn_w_down, final_norm_w):
    x_p = x_prompt
    x_s = x_sample
    bp = x_prompt.shape[0]
    lru_fw_list, lru_bw_list, ret_fw_list, ret_bw_list = [], [], [], []
    for l in range(DEPTH):
        p = dict(norm1_w=norm1_w[l], w_in=w_in[l], lru_conv_w=lru_conv_w[l], lru_conv_b=lru_conv_b[l],
                 lru_wa_fw=lru_wa_fw[l], lru_ba_fw=lru_ba_fw[l], lru_wx_fw=lru_wx_fw[l],
                 lru_bx_fw=lru_bx_fw[l], lru_lambda_fw=lru_lambda_fw[l],
                 lru_wa_bw=lru_wa_bw[l], lru_ba_bw=lru_ba_bw[l], lru_wx_bw=lru_wx_bw[l],
                 lru_bx_bw=lru_bx_bw[l], lru_lambda_bw=lru_lambda_bw[l],
                 ret_decay_fw=ret_decay_fw[l], ret_decay_bw=ret_decay_bw[l], ret_gn_w=ret_gn_w[l],
                 w_out=w_out[l], norm2_w=norm2_w[l], ffn_w_gate=ffn_w_gate[l], ffn_w_up=ffn_w_up[l],
                 ffn_conv_w=ffn_conv_w[l], ffn_conv_b=ffn_conv_b[l], ffn_w_down=ffn_w_down[l])
        mod_ctx = _modulation(c_ctx[None, :], w_mod[l], b_mod[l])
        mod_lat = _modulation(c, w_mod[l], b_mod[l])
        init_ctx = (jnp.zeros((bp, LRU_WIDTH), jnp.float32),
                    jnp.zeros((bp, LRU_WIDTH), jnp.float32),
                    jnp.zeros((bp, RET_HEADS, RET_HEAD_DIM, RET_HEAD_DIM), jnp.float32),
                    jnp.zeros((bp, RET_HEADS, RET_HEAD_DIM, RET_HEAD_DIM), jnp.float32))
        x_p, finals = _layer(x_p, mod_ctx, p, init_ctx, False)
        lru_fw_list.append(finals[0])
        lru_bw_list.append(finals[1])
        ret_fw_list.append(finals[2])
        ret_bw_list.append(finals[3])
        init_lat = (state_lru_fw[:, l].astype(jnp.float32), state_lru_bw[:, l].astype(jnp.float32),
                    state_ret_fw[:, l].astype(jnp.float32), state_ret_bw[:, l].astype(jnp.float32))
        x_s, _ = _layer(x_s, mod_lat, p, init_lat, True)
    y_prompt = _rmsnorm(x_p, final_norm_w)
    y_sample = _rmsnorm(x_s, final_norm_w)
    new_lru_fw = jnp.stack(lru_fw_list, axis=1)
    new_lru_bw = jnp.stack(lru_bw_list, axis=1)
    new_ret_fw = jnp.stack(ret_fw_list, axis=1)
    new_ret_bw = jnp.stack(ret_bw_list, axis=1)
    return (y_prompt, y_sample, new_lru_fw, new_lru_bw, new_ret_fw, new_ret_bw)
```

```python
import functools
import math

import jax
import jax.numpy as jnp
from jax import lax
from jax.experimental import pallas as pl
from jax.experimental.pallas import tpu as pltpu

F32 = jnp.float32
BF16 = jnp.bfloat16

N_MOD = 6
EPS = 1e-6
LRU_C = 8.0
LRU_BLOCKS = 8
LRU_CONV_WIDTH = 4
LRU_CONV_PAD_LEFT = 2
RET_HEADS = 4
RET_CHUNK = 128
GRID_W = 64
GATE_TILE = 256
FFN_TILE = 256
HALO_ROWS = 8
TIME_CHUNK = 512
FFN_ROWS = 512
VMEM_LIMIT = 56 * 1024 * 1024


def _dot(a, b):
    return jnp.dot(a.astype(BF16), b.astype(BF16), preferred_element_type=F32)


def _sigmoid(x):
    return 1.0 / (1.0 + jnp.exp(-x))


def _silu(x):
    return x * _sigmoid(x)


def _gelu_tanh(x):
    c = math.sqrt(2.0 / math.pi)
    return 0.5 * x * (1.0 + jnp.tanh(c * (x + 0.044715 * (x * x * x))))


def _softplus(x):
    return jnp.maximum(x, 0.0) + jnp.log1p(jnp.exp(-jnp.abs(x)))


def _rms(x):
    return x * lax.rsqrt(jnp.mean(x * x, axis=-1, keepdims=True) + EPS)


def _params(*sem):
    return pltpu.CompilerParams(dimension_semantics=sem, vmem_limit_bytes=VMEM_LIMIT)


def _const_spec(shape):
    nd = len(shape)
    return pl.BlockSpec(shape, lambda *_: (0,) * nd)


def _mod_kernel(c_ref, w_ref, b_ref, o_ref):
    o_ref[...] = _dot(_silu(c_ref[...]), w_ref[...]) + b_ref[...]


def _modulation(cvec, w, b):
    rows, d = cvec.shape
    n = w.shape[1]
    tn = n // 4
    return pl.pallas_call(
        _mod_kernel,
        grid=(n // tn,),
        in_specs=[pl.BlockSpec((rows, d), lambda j: (0, 0)),
                  pl.BlockSpec((d, tn), lambda j: (0, j)),
                  pl.BlockSpec((1, tn), lambda j: (0, j))],
        out_specs=pl.BlockSpec((rows, tn), lambda j: (0, j)),
        out_shape=jax.ShapeDtypeStruct((rows, n), F32),
        compiler_params=_params("arbitrary"),
        name="modulation",
    )(cvec, w, b)


TAB_INTRA, TAB_HEAD_FW, TAB_HEAD_BW, TAB_TAIL_FW, TAB_TAIL_BW, TAB_STEP_FW, TAB_STEP_BW = range(7)


def _tab_kernel(dec_ref, tab_ref):
    c = RET_CHUNK
    row = lax.broadcasted_iota(jnp.int32, (c, c), 0).astype(F32)
    col = lax.broadcasted_iota(jnp.int32, (c, c), 1).astype(F32)
    rel = row - col
    for h in range(RET_HEADS):
        lf = -_softplus(-jnp.full((c, c), dec_ref[0, h], F32))
        lb = -_softplus(-jnp.full((c, c), dec_ref[1, h], F32))
        tab_ref[TAB_INTRA, h] = jnp.where(rel > 0, jnp.exp(lf * rel),
                                          jnp.where(rel < 0, jnp.exp(lb * (-rel)), 2.0))
        tab_ref[TAB_HEAD_FW, h] = jnp.exp(lf * (row + 1.0))
        tab_ref[TAB_HEAD_BW, h] = jnp.exp(lb * (c - row))
        tab_ref[TAB_TAIL_FW, h] = jnp.exp(lf * (c - 1.0 - row))
        tab_ref[TAB_TAIL_BW, h] = jnp.exp(lb * row)
        tab_ref[TAB_STEP_FW, h] = jnp.exp(lf * c)
        tab_ref[TAB_STEP_BW, h] = jnp.exp(lb * c)


def _decay_tables(dec):
    return pl.pallas_call(
        _tab_kernel,
        in_specs=[pl.BlockSpec(memory_space=pltpu.SMEM)],
        out_shape=jax.ShapeDtypeStruct((7, RET_HEADS, RET_CHUNK, RET_CHUNK), F32),
        name="decay_tables",
    )(dec)


def _in_kernel(x_ref, mod_ref, nw_ref, w_ref, o_ref):
    m = mod_ref[0]
    h = (_rms(x_ref[0]) * nw_ref[...]) * (1.0 + m[1:2]) + m[0:1]
    o_ref[0] = _dot(h, w_ref[...])


def _in_proj(x, mod, mod_row, norm_w, w_in_bf, tm):
    b, t, d = x.shape
    n = w_in_bf.shape[1]
    return pl.pallas_call(
        _in_kernel,
        grid=(b, t // tm),
        in_specs=[pl.BlockSpec((1, tm, d), lambda i, j: (i, j, 0)),
                  pl.BlockSpec((1, N_MOD, d), lambda i, j: (mod_row(i), 0, 0)),
                  _const_spec((1, d)),
                  _const_spec((d, n))],
        out_specs=pl.BlockSpec((1, tm, n), lambda i, j: (i, j, 0)),
        out_shape=jax.ShapeDtypeStruct((b, t, n), F32),
        compiler_params=_params("arbitrary", "arbitrary"),
        name="in_proj",
    )(x, mod, norm_w, w_in_bf)


def _lru_conv(xe_ref, prev_ref, cur_ref, next_ref, cw_ref, cb_ref, first, last, tc):
    xe_ref[0:HALO_ROWS, :] = jnp.where(first, 0.0, prev_ref[0])
    xe_ref[HALO_ROWS:HALO_ROWS + tc, :] = cur_ref[0]
    xe_ref[HALO_ROWS + tc:2 * HALO_ROWS + tc, :] = jnp.where(last, 0.0, next_ref[0])
    cw = cw_ref[...]
    acc = cb_ref[...]
    for j in range(LRU_CONV_WIDTH):
        off = HALO_ROWS - LRU_CONV_PAD_LEFT + j
        acc = acc + xe_ref[off:off + tc, :] * cw[j:j + 1]
    return acc


def _lru_gates(xc, xc_bf, wg_ref, lp_ref, direction):
    lp = lp_ref[direction]
    width = xc.shape[1]

    def gate(g):
        parts = [jnp.dot(xc_bf[:, GATE_TILE * m:GATE_TILE * (m + 1)], wg_ref[direction, g, m],
                         preferred_element_type=F32) for m in range(width // GATE_TILE)]
        return jnp.concatenate(parts, axis=1)

    r = _sigmoid(gate(0) + lp[0:1])
    i = _sigmoid(gate(1) + lp[1:2])
    log_a = (-LRU_C * _softplus(-lp[2:3])) * r
    a = jnp.exp(log_a)
    u = jnp.sqrt(-jnp.tanh(log_a) * (a * a + 1.0)) * (i * xc)
    return a, u


def _chunk_scan(a, u, reverse):
    tc = a.shape[0]
    row = lax.broadcasted_iota(jnp.int32, a.shape, 0)
    h, p = u, a
    d = 1
    while d < tc:
        keep = (row < tc - d) if reverse else (row >= d)
        shift = (tc - d) if reverse else d
        h_prev = jnp.where(keep, pltpu.roll(h, shift, 0), 0.0)
        p_prev = jnp.where(keep, pltpu.roll(p, shift, 0), 1.0)
        h = p * h_prev + h
        p = p * p_prev
        d *= 2
    return h, p


def _kv_update(s_ref, tab_ref, k_scaled, v_bf, head, tail_tab, step_tab):
    kv = lax.dot_general((k_scaled * tab_ref[tail_tab, head]).astype(BF16), v_bf,
                         (((0,), (0,)), ((), ())), preferred_element_type=F32)
    s_ref[head] = tab_ref[step_tab, head] * s_ref[head] + kv


def _bw_kernel(prev_ref, cur_ref, next_ref, k_ref, v_ref, tab_ref, cw_ref, cb_ref, wg_ref, lp_ref,
               h0_ref, s0_ref, hb_ref, sbw_ref, finh_ref, fins_ref, xe_ref, hcar_ref, s_ref, *, tc, nc):
    c = pl.program_id(1)
    chunk = nc - 1 - c
    dh = RET_CHUNK

    @pl.when(c == 0)
    def _():
        hcar_ref[...] = h0_ref[0]
        s_ref[...] = s0_ref[0]

    xc = _lru_conv(xe_ref, prev_ref, cur_ref, next_ref, cw_ref, cb_ref, chunk == 0, chunk == nc - 1, tc)
    a, u = _lru_gates(xc, xc.astype(BF16), wg_ref, lp_ref, 1)
    h, p = _chunk_scan(a, u, True)
    carry = hcar_ref[...]
    hb_ref[0, 0] = carry
    hcar_ref[...] = h[0:1] + p[0:1] * carry

    @pl.when(c == 0)
    def _():
        finh_ref[0] = h[tc - 1:tc] + p[tc - 1:tc] * carry

    scale = dh ** -0.5
    for n in reversed(range(tc // RET_CHUNK)):
        rows = slice(n * RET_CHUNK, (n + 1) * RET_CHUNK)
        for hd in range(RET_HEADS):
            cols = slice(hd * dh, (hd + 1) * dh)
            sbw_ref[0, n, hd] = s_ref[hd].astype(BF16)
            _kv_update(s_ref, tab_ref, k_ref[0, rows, cols] * scale, v_ref[0, rows, cols].astype(BF16),
                       hd, TAB_TAIL_BW, TAB_STEP_BW)

    @pl.when(c == nc - 1)
    def _():
        fins_ref[0] = s_ref[...]


def _backward_sweep(proj, tab, cw, cb, wg, lp, h0, s0, tc):
    b, t, _ = proj.shape
    lw = cw.shape[1]
    nc = t // tc
    nh = tc // HALO_ROWS
    nsub = tc // RET_CHUNK
    dh = RET_CHUNK
    ch = lambda j: nc - 1 - j
    return pl.pallas_call(
        functools.partial(_bw_kernel, tc=tc, nc=nc),
        grid=(b, nc),
        in_specs=[
            pl.BlockSpec((1, HALO_ROWS, lw), lambda i, j: (i, jnp.maximum(ch(j) * nh - 1, 0), 0)),
            pl.BlockSpec((1, tc, lw), lambda i, j: (i, ch(j), 0)),
            pl.BlockSpec((1, HALO_ROWS, lw), lambda i, j: (i, jnp.minimum((ch(j) + 1) * nh, t // HALO_ROWS - 1), 0)),
            pl.BlockSpec((1, tc, lw), lambda i, j: (i, ch(j), 3)),
            pl.BlockSpec((1, tc, lw), lambda i, j: (i, ch(j), 4)),
            _const_spec(tab.shape), _const_spec(cw.shape), _const_spec(cb.shape),
            _const_spec(wg.shape), _const_spec(lp.shape),
            pl.BlockSpec((1, 1, lw), lambda i, j: (i, 0, 0)),
            pl.BlockSpec((1, RET_HEADS, dh, dh), lambda i, j: (i, 0, 0, 0)),
        ],
        out_specs=[
            pl.BlockSpec((1, 1, 1, lw), lambda i, j: (i, ch(j), 0, 0)),
            pl.BlockSpec((1, nsub, RET_HEADS, dh, dh), lambda i, j: (i, ch(j), 0, 0, 0)),
            pl.BlockSpec((1, 1, lw), lambda i, j: (i, 0, 0)),
            pl.BlockSpec((1, RET_HEADS, dh, dh), lambda i, j: (i, 0, 0, 0)),
        ],
        out_shape=[
            jax.ShapeDtypeStruct((b, nc, 1, lw), F32),
            jax.ShapeDtypeStruct((b, t // RET_CHUNK, RET_HEADS, dh, dh), BF16),
            jax.ShapeDtypeStruct((b, 1, lw), F32),
            jax.ShapeDtypeStruct((b, RET_HEADS, dh, dh), F32),
        ],
        scratch_shapes=[pltpu.VMEM((tc + 2 * HALO_ROWS, lw), F32),
                        pltpu.VMEM((1, lw), F32),
                        pltpu.VMEM((RET_HEADS, dh, dh), F32)],
        compiler_params=_params("arbitrary", "arbitrary"),
        name="backward_sweep",
    )(proj, proj, proj, proj, proj, tab, cw, cb, wg, lp, h0, s0)


def _fw_kernel(x_ref, mod_ref, prev_ref, cur_ref, next_ref, lg_ref, q_ref, k_ref, v_ref, rg_ref,
               hb_ref, sbw_ref, tab_ref, cw_ref, cb_ref, wg_ref, lp_ref, gnw_ref, wout_ref, h0_ref, s0_ref,
               o_ref, finh_ref, fins_ref, xe_ref, hcar_ref, s_ref, y_ref, *, tc, nc):
    c = pl.program_id(1)
    dh = RET_CHUNK
    lw = cur_ref.shape[2]

    @pl.when(c == 0)
    def _():
        hcar_ref[...] = h0_ref[0]
        s_ref[...] = s0_ref[0]

    xc = _lru_conv(xe_ref, prev_ref, cur_ref, next_ref, cw_ref, cb_ref, c == 0, c == nc - 1, tc)
    xc_bf = xc.astype(BF16)
    a, u = _lru_gates(xc, xc_bf, wg_ref, lp_ref, 0)
    h, p = _chunk_scan(a, u, False)
    h_fw = h + p * hcar_ref[...]
    hcar_ref[...] = h_fw[tc - 1:tc]

    @pl.when(c == 0)
    def _():
        finh_ref[0] = h_fw[0:1]

    a, u = _lru_gates(xc, xc_bf, wg_ref, lp_ref, 1)
    h, p = _chunk_scan(a, u, True)
    h_bw = h + p * hb_ref[0, 0]
    y_ref[:, 0:lw] = ((h_fw + h_bw) * _gelu_tanh(lg_ref[0])).astype(BF16)

    scale = dh ** -0.5
    gnw = gnw_ref[...]
    for n in range(tc // RET_CHUNK):
        rows = slice(n * RET_CHUNK, (n + 1) * RET_CHUNK)
        for hd in range(RET_HEADS):
            cols = slice(hd * dh, (hd + 1) * dh)
            q = q_ref[0, rows, cols]
            k = k_ref[0, rows, cols] * scale
            v_bf = v_ref[0, rows, cols].astype(BF16)
            scores = lax.dot_general(q.astype(BF16), k.astype(BF16), (((1,), (1,)), ((), ())),
                                     preferred_element_type=F32)
            intra = _dot(scores * tab_ref[TAB_INTRA, hd], v_bf)
            q_both = jnp.concatenate([q * tab_ref[TAB_HEAD_FW, hd], q * tab_ref[TAB_HEAD_BW, hd]], axis=1)
            s_both = jnp.concatenate([s_ref[hd].astype(BF16), sbw_ref[0, n, hd]], axis=0)
            o = intra + _dot(q_both, s_both)
            dev = o - jnp.mean(o, axis=-1, keepdims=True)
            on = dev * lax.rsqrt(jnp.mean(dev * dev, axis=-1, keepdims=True) + EPS) * gnw[:, cols]
            y_ref[rows, lw + hd * dh:lw + (hd + 1) * dh] = (on * _silu(rg_ref[0, rows, cols])).astype(BF16)
            _kv_update(s_ref, tab_ref, k, v_bf, hd, TAB_TAIL_FW, TAB_STEP_FW)

    o_ref[0] = x_ref[0] + mod_ref[0][2:3] * _dot(y_ref[...], wout_ref[...])

    @pl.when(c == nc - 1)
    def _():
        fins_ref[0] = s_ref[...]


def _forward_sweep(x, mod, mod_row, proj, hb, sbw, tab, cw, cb, wg, lp, gnw, wout_bf, h0, s0, tc):
    b, t, d = x.shape
    lw = cw.shape[1]
    nc = t // tc
    nh = tc // HALO_ROWS
    nsub = tc // RET_CHUNK
    dh = RET_CHUNK
    col = lambda k: pl.BlockSpec((1, tc, lw), lambda i, j: (i, j, k))
    return pl.pallas_call(
        functools.partial(_fw_kernel, tc=tc, nc=nc),
        grid=(b, nc),
        in_specs=[
            pl.BlockSpec((1, tc, d), lambda i, j: (i, j, 0)),
            pl.BlockSpec((1, N_MOD, d), lambda i, j: (mod_row(i), 0, 0)),
            pl.BlockSpec((1, HALO_ROWS, lw), lambda i, j: (i, jnp.maximum(j * nh - 1, 0), 0)),
            col(0),
            pl.BlockSpec((1, HALO_ROWS, lw), lambda i, j: (i, jnp.minimum((j + 1) * nh, t // HALO_ROWS - 1), 0)),
            col(1), col(2), col(3), col(4), col(5),
            pl.BlockSpec((1, 1, 1, lw), lambda i, j: (i, j, 0, 0)),
            pl.BlockSpec((1, nsub, RET_HEADS, dh, dh), lambda i, j: (i, j, 0, 0, 0)),
            _const_spec(tab.shape), _const_spec(cw.shape), _const_spec(cb.shape),
            _const_spec(wg.shape), _const_spec(lp.shape), _const_spec(gnw.shape), _const_spec(wout_bf.shape),
            pl.BlockSpec((1, 1, lw), lambda i, j: (i, 0, 0)),
            pl.BlockSpec((1, RET_HEADS, dh, dh), lambda i, j: (i, 0, 0, 0)),
        ],
        out_specs=[
            pl.BlockSpec((1, tc, d), lambda i, j: (i, j, 0)),
            pl.BlockSpec((1, 1, lw), lambda i, j: (i, 0, 0)),
            pl.BlockSpec((1, RET_HEADS, dh, dh), lambda i, j: (i, 0, 0, 0)),
        ],
        out_shape=[
            jax.ShapeDtypeStruct((b, t, d), F32),
            jax.ShapeDtypeStruct((b, 1, lw), F32),
            jax.ShapeDtypeStruct((b, RET_HEADS, dh, dh), F32),
        ],
        scratch_shapes=[pltpu.VMEM((tc + 2 * HALO_ROWS, lw), F32),
                        pltpu.VMEM((1, lw), F32),
                        pltpu.VMEM((RET_HEADS, dh, dh), F32),
                        pltpu.VMEM((tc, wout_bf.shape[0]), BF16)],
        compiler_params=_params("arbitrary", "arbitrary"),
        name="forward_sweep",
    )(x, mod, proj, proj, proj, proj, proj, proj, proj, proj, hb, sbw, tab, cw, cb, wg, lp, gnw, wout_bf, h0, s0)


def _ffn_kernel(*refs, tf, halo, period, taps, final):
    if halo:
        xp_ref, x_ref, xn_ref = refs[:3]
        refs = refs[3:]
    else:
        x_ref = refs[0]
        refs = refs[1:]
    mod_ref, n2w_ref, wg_ref, wu_ref, wd_ref, cw_ref, cb_ref, fnw_ref, o_ref, h2_ref, acc_ref = refs
    m = mod_ref[0]
    n2w = n2w_ref[...]

    def norm_mod(x):
        return (_rms(x) * n2w) * (1.0 + m[4:5]) + m[3:4]

    x = x_ref[0]
    h2_ref[halo:halo + tf, :] = norm_mod(x).astype(BF16)
    if halo:
        j = pl.program_id(1)
        nj = pl.num_programs(1)
        h2_ref[0:halo, :] = jnp.where(j == 0, 0.0, norm_mod(xp_ref[0])).astype(BF16)
        h2_ref[halo + tf:2 * halo + tf, :] = jnp.where(j == nj - 1, 0.0, norm_mod(xn_ref[0])).astype(BF16)
    acc_ref[...] = jnp.zeros_like(acc_ref)

    rows = tf + 2 * halo
    pos = lax.broadcasted_iota(jnp.int32, (rows, FFN_TILE), 0) & (period - 1)
    has_left = pos != 0
    has_right = pos != period - 1

    def body(n, carry):
        g = jnp.dot(h2_ref[...], wg_ref[n], preferred_element_type=F32)
        g_left = jnp.where(has_left, pltpu.roll(g, 1, 0), 0.0)
        g_right = jnp.where(has_right, pltpu.roll(g, rows - 1, 0), 0.0)
        cw = cw_ref[n]
        conv = cb_ref[n]
        for dr, wrow in taps:
            lo = halo + dr * period
            conv = (conv + g_left[lo:lo + tf] * cw[3 * wrow:3 * wrow + 1]
                    + g[lo:lo + tf] * cw[3 * wrow + 1:3 * wrow + 2]
                    + g_right[lo:lo + tf] * cw[3 * wrow + 2:3 * wrow + 3])
        up = jnp.dot(h2_ref[halo:halo + tf, :], wu_ref[n], preferred_element_type=F32)
        acc_ref[...] += _dot(_gelu_tanh(conv) * up, wd_ref[n])
        return carry

    lax.fori_loop(0, wg_ref.shape[0], body, 0)
    out = x + m[5:6] * acc_ref[...]
    o_ref[0] = _rms(out) * fnw_ref[...] if final else out


def _ffn(x, mod, mod_row, n2w, wg, wu, wd, cw, cb, fnw, latent, final):
    b, t, d = x.shape
    if latent:
        tf, halo, period = FFN_ROWS, GRID_W, GRID_W
        taps = ((-1, 0), (0, 1), (1, 2))
        nb = tf // halo
        x_specs = [pl.BlockSpec((1, halo, d), lambda i, j: (i, jnp.maximum(j * nb - 1, 0), 0)),
                   pl.BlockSpec((1, tf, d), lambda i, j: (i, j, 0)),
                   pl.BlockSpec((1, halo, d), lambda i, j: (i, jnp.minimum((j + 1) * nb, t // halo - 1), 0))]
        x_args = (x, x, x)
    else:
        tf, halo, period = t, 0, t
        taps = ((0, 1),)
        x_specs = [pl.BlockSpec((1, tf, d), lambda i, j: (i, j, 0))]
        x_args = (x,)
    assert period & (period - 1) == 0 and t % tf == 0
    resident = lambda a: pl.BlockSpec(a.shape, lambda *_: (0,) * a.ndim, pipeline_mode=pl.Buffered(1))
    return pl.pallas_call(
        functools.partial(_ffn_kernel, tf=tf, halo=halo, period=period, taps=taps, final=final),
        grid=(b, t // tf),
        in_specs=x_specs + [
            pl.BlockSpec((1, N_MOD, d), lambda i, j: (mod_row(i), 0, 0)),
            _const_spec(n2w.shape), resident(wg), resident(wu), resident(wd),
            _const_spec(cw.shape), _const_spec(cb.shape), _const_spec(fnw.shape)],
        out_specs=pl.BlockSpec((1, tf, d), lambda i, j: (i, j, 0)),
        out_shape=jax.ShapeDtypeStruct((b, t, d), F32),
        scratch_shapes=[pltpu.VMEM((tf + 2 * halo, d), BF16), pltpu.VMEM((tf, d), F32)],
        compiler_params=_params("arbitrary", "arbitrary"),
        name="ffn",
    )(*x_args, mod, n2w, wg, wu, wd, cw, cb, fnw)


def _block_diag_tiles(w):
    nb, c, _ = w.shape
    per = GATE_TILE // c
    eye = jnp.eye(per, dtype=w.dtype)
    w = w.reshape(nb // per, per, c, c)
    return jnp.einsum('mjab,jk->mjakb', w, eye).reshape(nb // per, GATE_TILE, GATE_TILE)


def _hidden_tiles(w):
    lead = w.shape[:-1]
    w = w.reshape(*lead, w.shape[-1] // FFN_TILE, FFN_TILE)
    return jnp.moveaxis(w, -2, 0)


def _layer(x, mod, mod_row, p, tab, states, latent, final):
    b, t, d = x.shape
    tc = min(t, TIME_CHUNK)
    proj = _in_proj(x, mod, mod_row, p['norm1_w'], p['w_in'], tc)
    hb, sbw, fin_h_bw, fin_s_bw = _backward_sweep(
        proj, tab, p['lru_conv_w'], p['lru_conv_b'], p['wg'], p['lp'], states[1], states[3], tc)
    x, fin_h_fw, fin_s_fw = _forward_sweep(
        x, mod, mod_row, proj, hb, sbw, tab, p['lru_conv_w'], p['lru_conv_b'], p['wg'], p['lp'],
        p['ret_gn_w'], p['w_out'], states[0], states[2], tc)
    x = _ffn(x, mod, mod_row, p['norm2_w'], p['ffn_wg'], p['ffn_wu'], p['ffn_wd'], p['ffn_cw'], p['ffn_cb'],
             p['final_norm_w'], latent, final)
    return x, (fin_h_fw, fin_h_bw, fin_s_fw, fin_s_bw)


def kernel(x_prompt, x_sample, state_lru_fw, state_lru_bw, state_ret_fw, state_ret_bw, c, c_ctx,
           norm1_w, w_mod, b_mod, w_in, lru_conv_w, lru_conv_b,
           lru_wa_fw, lru_ba_fw, lru_wx_fw, lru_bx_fw, lru_lambda_fw,
           lru_wa_bw, lru_ba_bw, lru_wx_bw, lru_bx_bw, lru_lambda_bw,
           ret_decay_fw, ret_decay_bw, ret_gn_w, w_out, norm2_w,
           ffn_w_gate, ffn_w_up, ffn_conv_w, ffn_conv_b, ffn_w_down, final_norm_w):
    depth = w_in.shape[0]
    bp = x_prompt.shape[0]
    bs, d = c.shape
    lw = lru_conv_w.shape[2]
    dh = RET_CHUNK

    ctx_row = bs
    rows = -(-(bs + 1) // 8) * 8
    cvec = jnp.zeros((rows, d), F32).at[:bs].set(c).at[ctx_row].set(c_ctx)
    row_ctx = lambda i: ctx_row
    row_lat = lambda i: i

    x_p, x_s = x_prompt, x_sample
    fins = []
    for l in range(depth):
        mod = _modulation(cvec, w_mod[l], b_mod[l][None]).reshape(rows, N_MOD, d)
        tab = _decay_tables(jnp.stack([ret_decay_fw[l], ret_decay_bw[l]]).astype(F32))
        p = dict(
            norm1_w=norm1_w[l][None], w_in=w_in[l].astype(BF16),
            lru_conv_w=lru_conv_w[l], lru_conv_b=lru_conv_b[l][None],
            wg=jnp.stack([jnp.stack([_block_diag_tiles(lru_wa_fw[l]), _block_diag_tiles(lru_wx_fw[l])]),
                          jnp.stack([_block_diag_tiles(lru_wa_bw[l]), _block_diag_tiles(lru_wx_bw[l])])]
                         ).astype(BF16),
            lp=jnp.stack([jnp.stack([lru_ba_fw[l], lru_bx_fw[l], lru_lambda_fw[l]]),
                          jnp.stack([lru_ba_bw[l], lru_bx_bw[l], lru_lambda_bw[l]])]).astype(F32),
            ret_gn_w=ret_gn_w[l][None], w_out=w_out[l].astype(BF16), norm2_w=norm2_w[l][None],
            ffn_wg=_hidden_tiles(ffn_w_gate[l]).astype(BF16),
            ffn_wu=_hidden_tiles(ffn_w_up[l]).astype(BF16),
            ffn_wd=ffn_w_down[l].reshape(-1, FFN_TILE, d).astype(BF16),
            ffn_cw=_hidden_tiles(ffn_conv_w[l].reshape(9, -1)),
            ffn_cb=_hidden_tiles(ffn_conv_b[l][None]),
            final_norm_w=final_norm_w[None],
        )
        final = l == depth - 1
        zeros_h = jnp.zeros((bp, 1, lw), F32)
        zeros_s = jnp.zeros((bp, RET_HEADS, dh, dh), F32)
        x_p, fin = _layer(x_p, mod, row_ctx, p, tab, (zeros_h, zeros_h, zeros_s, zeros_s), False, final)
        fins.append(fin)
        init_lat = (state_lru_fw[:, l][:, None].astype(F32), state_lru_bw[:, l][:, None].astype(F32),
                    state_ret_fw[:, l].astype(F32), state_ret_bw[:, l].astype(F32))
        x_s, _ = _layer(x_s, mod, row_lat, p, tab, init_lat, True, final)

    dt = x_prompt.dtype
    new_lru_fw = jnp.concatenate([f[0] for f in fins], axis=1).astype(dt)
    new_lru_bw = jnp.concatenate([f[1] for f in fins], axis=1).astype(dt)
    new_ret_fw = jnp.stack([f[2] for f in fins], axis=1).astype(dt)
    new_ret_bw = jnp.stack([f[3] for f in fins], axis=1).astype(dt)
    return (x_p, x_s, new_lru_fw, new_lru_bw, new_ret_fw, new_ret_bw)
```

```python
import functools
import math

import jax
import jax.numpy as jnp
from jax import lax
from jax.experimental import pallas as pl
from jax.experimental.pallas import tpu as pltpu

F32 = jnp.float32
BF16 = jnp.bfloat16

N_MOD = 6
EPS = 1e-6
LRU_C = 8.0
LRU_BLOCKS = 8
LRU_CONV_WIDTH = 4
LRU_CONV_PAD_LEFT = 2
RET_HEADS = 4
RET_CHUNK = 128
GRID_W = 64
GATE_TILE = 256
FFN_TILE = 256
HALO_ROWS = 8
TIME_CHUNK = 512
FFN_ROWS = 512
VMEM_LIMIT = 56 * 1024 * 1024


def _dot(a, b):
    return jnp.dot(a.astype(BF16), b.astype(BF16), preferred_element_type=F32)


def _sigmoid(x):
    return 1.0 / (1.0 + jnp.exp(-x))


def _silu(x):
    return x * _sigmoid(x)


def _gelu_tanh(x):
    c = math.sqrt(2.0 / math.pi)
    return 0.5 * x * (1.0 + jnp.tanh(c * (x + 0.044715 * (x * x * x))))


def _softplus(x):
    return jnp.maximum(x, 0.0) + jnp.log1p(jnp.exp(-jnp.abs(x)))


def _rms(x):
    return x * lax.rsqrt(jnp.mean(x * x, axis=-1, keepdims=True) + EPS)


def _params(*sem):
    return pltpu.CompilerParams(dimension_semantics=sem, vmem_limit_bytes=VMEM_LIMIT)


def _const_spec(shape):
    nd = len(shape)
    return pl.BlockSpec(shape, lambda *_: (0,) * nd)


def _mod_kernel(c_ref, w_ref, b_ref, o_ref):
    o_ref[...] = _dot(_silu(c_ref[...]), w_ref[...]) + b_ref[...]


def _modulation(cvec, w, b):
    rows, d = cvec.shape
    n = w.shape[1]
    tn = n // 4
    return pl.pallas_call(
        _mod_kernel,
        grid=(n // tn,),
        in_specs=[pl.BlockSpec((rows, d), lambda j: (0, 0)),
                  pl.BlockSpec((d, tn), lambda j: (0, j)),
                  pl.BlockSpec((1, tn), lambda j: (0, j))],
        out_specs=pl.BlockSpec((rows, tn), lambda j: (0, j)),
        out_shape=jax.ShapeDtypeStruct((rows, n), F32),
        compiler_params=_params("arbitrary"),
        name="modulation",
    )(cvec, w, b)


TAB_INTRA, TAB_HEAD_FW, TAB_HEAD_BW, TAB_TAIL_FW, TAB_TAIL_BW, TAB_STEP_FW, TAB_STEP_BW = range(7)


def _tab_kernel(dec_ref, tab_ref):
    c = RET_CHUNK
    row = lax.broadcasted_iota(jnp.int32, (c, c), 0).astype(F32)
    col = lax.broadcasted_iota(jnp.int32, (c, c), 1).astype(F32)
    rel = row - col
    for h in range(RET_HEADS):
        lf = -_softplus(-jnp.full((c, c), dec_ref[0, h], F32))
        lb = -_softplus(-jnp.full((c, c), dec_ref[1, h], F32))
        tab_ref[TAB_INTRA, h] = jnp.where(rel > 0, jnp.exp(lf * rel),
                                          jnp.where(rel < 0, jnp.exp(lb * (-rel)), 2.0))
        tab_ref[TAB_HEAD_FW, h] = jnp.exp(lf * (row + 1.0))
        tab_ref[TAB_HEAD_BW, h] = jnp.exp(lb * (c - row))
        tab_ref[TAB_TAIL_FW, h] = jnp.exp(lf * (c - 1.0 - row))
        tab_ref[TAB_TAIL_BW, h] = jnp.exp(lb * row)
        tab_ref[TAB_STEP_FW, h] = jnp.exp(lf * c)
        tab_ref[TAB_STEP_BW, h] = jnp.exp(lb * c)


def _decay_tables(dec):
    return pl.pallas_call(
        _tab_kernel,
        in_specs=[pl.BlockSpec(memory_space=pltpu.SMEM)],
        out_shape=jax.ShapeDtypeStruct((7, RET_HEADS, RET_CHUNK, RET_CHUNK), F32),
        name="decay_tables",
    )(dec)


def _in_kernel(x_ref, mod_ref, nw_ref, w_ref, o_ref):
    m = mod_ref[0]
    h = (_rms(x_ref[0]) * nw_ref[...]) * (1.0 + m[1:2]) + m[0:1]
    o_ref[0] = _dot(h, w_ref[...])


def _in_proj(x, mod, mod_row, norm_w, w_in_bf, tm):
    b, t, d = x.shape
    n = w_in_bf.shape[1]
    return pl.pallas_call(
        _in_kernel,
        grid=(b, t // tm),
        in_specs=[pl.BlockSpec((1, tm, d), lambda i, j: (i, j, 0)),
                  pl.BlockSpec((1, N_MOD, d), lambda i, j: (mod_row(i), 0, 0)),
                  _const_spec((1, d)),
                  _const_spec((d, n))],
        out_specs=pl.BlockSpec((1, tm, n), lambda i, j: (i, j, 0)),
        out_shape=jax.ShapeDtypeStruct((b, t, n), F32),
        compiler_params=_params("arbitrary", "arbitrary"),
        name="in_proj",
    )(x, mod, norm_w, w_in_bf)


def _lru_conv(xe_ref, prev_ref, cur_ref, next_ref, cw_ref, cb_ref, first, last, tc):
    xe_ref[0:HALO_ROWS, :] = jnp.where(first, 0.0, prev_ref[0])
    xe_ref[HALO_ROWS:HALO_ROWS + tc, :] = cur_ref[0]
    xe_ref[HALO_ROWS + tc:2 * HALO_ROWS + tc, :] = jnp.where(last, 0.0, next_ref[0])
    cw = cw_ref[...]
    acc = cb_ref[...]
    for j in range(LRU_CONV_WIDTH):
        off = HALO_ROWS - LRU_CONV_PAD_LEFT + j
        acc = acc + xe_ref[off:off + tc, :] * cw[j:j + 1]
    return acc


def _lru_gates(xc, xc_bf, wg_ref, lp_ref, direction):
    lp = lp_ref[direction]
    width = xc.shape[1]

    def gate(g):
        parts = [jnp.dot(xc_bf[:, GATE_TILE * m:GATE_TILE * (m + 1)], wg_ref[direction, g, m],
                         preferred_element_type=F32) for m in range(width // GATE_TILE)]
        return jnp.concatenate(parts, axis=1)

    r = _sigmoid(gate(0) + lp[0:1])
    i = _sigmoid(gate(1) + lp[1:2])
    log_a = (-LRU_C * _softplus(-lp[2:3])) * r
    a = jnp.exp(log_a)
    u = jnp.sqrt(-jnp.tanh(log_a) * (a * a + 1.0)) * (i * xc)
    return a, u


def _chunk_scan(a, u, reverse):
    tc = a.shape[0]
    row = lax.broadcasted_iota(jnp.int32, a.shape, 0)
    h, p = u, a
    d = 1
    while d < tc:
        keep = (row < tc - d) if reverse else (row >= d)
        shift = (tc - d) if reverse else d
        h_prev = jnp.where(keep, pltpu.roll(h, shift, 0), 0.0)
        p_prev = jnp.where(keep, pltpu.roll(p, shift, 0), 1.0)
        h = p * h_prev + h
        p = p * p_prev
        d *= 2
    return h, p


def _kv_update(s_ref, tab_ref, k_scaled, v_bf, head, tail_tab, step_tab):
    kv = lax.dot_general((k_scaled * tab_ref[tail_tab, head]).astype(BF16), v_bf,
                         (((0,), (0,)), ((), ())), preferred_element_type=F32)
    s_ref[head] = tab_ref[step_tab, head] * s_ref[head] + kv


def _bw_kernel(prev_ref, cur_ref, next_ref, k_ref, v_ref, tab_ref, cw_ref, cb_ref, wg_ref, lp_ref,
               h0_ref, s0_ref, hb_ref, sbw_ref, finh_ref, fins_ref, xe_ref, hcar_ref, s_ref, *, tc, nc):
    c = pl.program_id(1)
    chunk = nc - 1 - c
    dh = RET_CHUNK

    @pl.when(c == 0)
    def _():
        hcar_ref[...] = h0_ref[0]
        s_ref[...] = s0_ref[0]

    xc = _lru_conv(xe_ref, prev_ref, cur_ref, next_ref, cw_ref, cb_ref, chunk == 0, chunk == nc - 1, tc)
    a, u = _lru_gates(xc, xc.astype(BF16), wg_ref, lp_ref, 1)
    h, p = _chunk_scan(a, u, True)
    carry = hcar_ref[...]
    hb_ref[0, 0] = carry
    hcar_ref[...] = h[0:1] + p[0:1] * carry

    @pl.when(c == 0)
    def _():
        finh_ref[0] = h[tc - 1:tc] + p[tc - 1:tc] * carry

    scale = dh ** -0.5
    for n in reversed(range(tc // RET_CHUNK)):
        rows = slice(n * RET_CHUNK, (n + 1) * RET_CHUNK)
        for hd in range(RET_HEADS):
            cols = slice(hd * dh, (hd + 1) * dh)
            sbw_ref[0, n, hd] = s_ref[hd].astype(BF16)
            _kv_update(s_ref, tab_ref, k_ref[0, rows, cols] * scale, v_ref[0, rows, cols].astype(BF16),
                       hd, TAB_TAIL_BW, TAB_STEP_BW)

    @pl.when(c == nc - 1)
    def _():
        fins_ref[0] = s_ref[...]


def _backward_sweep(proj, tab, cw, cb, wg, lp, h0, s0, tc):
    b, t, _ = proj.shape
    lw = cw.shape[1]
    nc = t // tc
    nh = tc // HALO_ROWS
    nsub = tc // RET_CHUNK
    dh = RET_CHUNK
    ch = lambda j: nc - 1 - j
    return pl.pallas_call(
        functools.partial(_bw_kernel, tc=tc, nc=nc),
        grid=(b, nc),
        in_specs=[
            pl.BlockSpec((1, HALO_ROWS, lw), lambda i, j: (i, jnp.maximum(ch(j) * nh - 1, 0), 0)),
            pl.BlockSpec((1, tc, lw), lambda i, j: (i, ch(j), 0)),
            pl.BlockSpec((1, HALO_ROWS, lw), lambda i, j: (i, jnp.minimum((ch(j) + 1) * nh, t // HALO_ROWS - 1), 0)),
            pl.BlockSpec((1, tc, lw), lambda i, j: (i, ch(j), 3)),
            pl.BlockSpec((1, tc, lw), lambda i, j: (i, ch(j), 4)),
            _const_spec(tab.shape), _const_spec(cw.shape), _const_spec(cb.shape),
            _const_spec(wg.shape), _const_spec(lp.shape),
            pl.BlockSpec((1, 1, lw), lambda i, j: (i, 0, 0)),
            pl.BlockSpec((1, RET_HEADS, dh, dh), lambda i, j: (i, 0, 0, 0)),
        ],
        out_specs=[
            pl.BlockSpec((1, 1, 1, lw), lambda i, j: (i, ch(j), 0, 0)),
            pl.BlockSpec((1, nsub, RET_HEADS, dh, dh), lambda i, j: (i, ch(j), 0, 0, 0)),
            pl.BlockSpec((1, 1, lw), lambda i, j: (i, 0, 0)),
            pl.BlockSpec((1, RET_HEADS, dh, dh), lambda i, j: (i, 0, 0, 0)),
        ],
        out_shape=[
            jax.ShapeDtypeStruct((b, nc, 1, lw), F32),
            jax.ShapeDtypeStruct((b, t // RET_CHUNK, RET_HEADS, dh, dh), BF16),
            jax.ShapeDtypeStruct((b, 1, lw), F32),
            jax.ShapeDtypeStruct((b, RET_HEADS, dh, dh), F32),
        ],
        scratch_shapes=[pltpu.VMEM((tc + 2 * HALO_ROWS, lw), F32),
                        pltpu.VMEM((1, lw), F32),
                        pltpu.VMEM((RET_HEADS, dh, dh), F32)],
        compiler_params=_params("arbitrary", "arbitrary"),
        name="backward_sweep",
    )(proj, proj, proj, proj, proj, tab, cw, cb, wg, lp, h0, s0)


def _fw_kernel(x_ref, mod_ref, prev_ref, cur_ref, next_ref, lg_ref, q_ref, k_ref, v_ref, rg_ref,
               hb_ref, sbw_ref, tab_ref, cw_ref, cb_ref, wg_ref, lp_ref, gnw_ref, wout_ref, h0_ref, s0_ref,
               o_ref, finh_ref, fins_ref, xe_ref, hcar_ref, s_ref, y_ref, *, tc, nc):
    c = pl.program_id(1)
    dh = RET_CHUNK
    lw = cur_ref.shape[2]

    @pl.when(c == 0)
    def _():
        hcar_ref[...] = h0_ref[0]
        s_ref[...] = s0_ref[0]

    xc = _lru_conv(xe_ref, prev_ref, cur_ref, next_ref, cw_ref, cb_ref, c == 0, c == nc - 1, tc)
    xc_bf = xc.astype(BF16)
    a, u = _lru_gates(xc, xc_bf, wg_ref, lp_ref, 0)
    h, p = _chunk_scan(a, u, False)
    h_fw = h + p * hcar_ref[...]
    hcar_ref[...] = h_fw[tc - 1:tc]

    @pl.when(c == 0)
    def _():
        finh_ref[0] = h_fw[0:1]

    a, u = _lru_gates(xc, xc_bf, wg_ref, lp_ref, 1)
    h, p = _chunk_scan(a, u, True)
    h_bw = h + p * hb_ref[0, 0]
    y_ref[:, 0:lw] = ((h_fw + h_bw) * _gelu_tanh(lg_ref[0])).astype(BF16)

    scale = dh ** -0.5
    gnw = gnw_ref[...]
    for n in range(tc // RET_CHUNK):
        rows = slice(n * RET_CHUNK, (n + 1) * RET_CHUNK)
        for hd in range(RET_HEADS):
            cols = slice(hd * dh, (hd + 1) * dh)
            q = q_ref[0, rows, cols]
            k = k_ref[0, rows, cols] * scale
            v_bf = v_ref[0, rows, cols].astype(BF16)
            scores = lax.dot_general(q.astype(BF16), k.astype(BF16), (((1,), (1,)), ((), ())),
                                     preferred_element_type=F32)
            intra = _dot(scores * tab_ref[TAB_INTRA, hd], v_bf)
            q_both = jnp.concatenate([q * tab_ref[TAB_HEAD_FW, hd], q * tab_ref[TAB_HEAD_BW, hd]], axis=1)
            s_both = jnp.concatenate([s_ref[hd].astype(BF16), sbw_ref[0, n, hd]], axis=0)
            o = intra + _dot(q_both, s_both)
            dev = o - jnp.mean(o, axis=-1, keepdims=True)
            on = dev * lax.rsqrt(jnp.mean(dev * dev, axis=-1, keepdims=True) + EPS) * gnw[:, cols]
            y_ref[rows, lw + hd * dh:lw + (hd + 1) * dh] = (on * _silu(rg_ref[0, rows, cols])).astype(BF16)
            _kv_update(s_ref, tab_ref, k, v_bf, hd, TAB_TAIL_FW, TAB_STEP_FW)

    o_ref[0] = x_ref[0] + mod_ref[0][2:3] * _dot(y_ref[...], wout_ref[...])

    @pl.when(c == nc - 1)
    def _():
        fins_ref[0] = s_ref[...]


def _forward_sweep(x, mod, mod_row, proj, hb, sbw, tab, cw, cb, wg, lp, gnw, wout_bf, h0, s0, tc):
    b, t, d = x.shape
    lw = cw.shape[1]
    nc = t // tc
    nh = tc // HALO_ROWS
    nsub = tc // RET_CHUNK
    dh = RET_CHUNK
    col = lambda k: pl.BlockSpec((1, tc, lw), lambda i, j: (i, j, k))
    return pl.pallas_call(
        functools.partial(_fw_kernel, tc=tc, nc=nc),
        grid=(b, nc),
        in_specs=[
            pl.BlockSpec((1, tc, d), lambda i, j: (i, j, 0)),
            pl.BlockSpec((1, N_MOD, d), lambda i, j: (mod_row(i), 0, 0)),
            pl.BlockSpec((1, HALO_ROWS, lw), lambda i, j: (i, jnp.maximum(j * nh - 1, 0), 0)),
            col(0),
            pl.BlockSpec((1, HALO_ROWS, lw), lambda i, j: (i, jnp.minimum((j + 1) * nh, t // HALO_ROWS - 1), 0)),
            col(1), col(2), col(3), col(4), col(5),
            pl.BlockSpec((1, 1, 1, lw), lambda i, j: (i, j, 0, 0)),
            pl.BlockSpec((1, nsub, RET_HEADS, dh, dh), lambda i, j: (i, j, 0, 0, 0)),
            _const_spec(tab.shape), _const_spec(cw.shape), _const_spec(cb.shape),
            _const_spec(wg.shape), _const_spec(lp.shape), _const_spec(gnw.shape), _const_spec(wout_bf.shape),
            pl.BlockSpec((1, 1, lw), lambda i, j: (i, 0, 0)),
            pl.BlockSpec((1, RET_HEADS, dh, dh), lambda i, j: (i, 0, 0, 0)),
        ],
        out_specs=[
            pl.BlockSpec((1, tc, d), lambda i, j: (i, j, 0)),
            pl.BlockSpec((1, 1, lw), lambda i, j: (i, 0, 0)),
            pl.BlockSpec((1, RET_HEADS, dh, dh), lambda i, j: (i, 0, 0, 0)),
        ],
        out_shape=[
            jax.ShapeDtypeStruct((b, t, d), F32),
            jax.ShapeDtypeStruct((b, 1, lw), F32),
            jax.ShapeDtypeStruct((b, RET_HEADS, dh, dh), F32),
        ],
        scratch_shapes=[pltpu.VMEM((tc + 2 * HALO_ROWS, lw), F32),
                        pltpu.VMEM((1, lw), F32),
                        pltpu.VMEM((RET_HEADS, dh, dh), F32),
                        pltpu.VMEM((tc, wout_bf.shape[0]), BF16)],
        compiler_params=_params("arbitrary", "arbitrary"),
        name="forward_sweep",
    )(x, mod, proj, proj, proj, proj, proj, proj, proj, proj, hb, sbw, tab, cw, cb, wg, lp, gnw, wout_bf, h0, s0)


def _ffn_kernel(*refs, tf, halo, period, taps, final):
    if halo:
        xp_ref, x_ref, xn_ref = refs[:3]
        refs = refs[3:]
    else:
        x_ref = refs[0]
        refs = refs[1:]
    mod_ref, n2w_ref, wg_ref, wu_ref, wd_ref, cw_ref, cb_ref, fnw_ref, o_ref, h2_ref, act_ref = refs
    m = mod_ref[0]
    n2w = n2w_ref[...]

    def norm_mod(x):
        return (_rms(x) * n2w) * (1.0 + m[4:5]) + m[3:4]

    x = x_ref[0]
    h2_ref[halo:halo + tf, :] = norm_mod(x).astype(BF16)
    if halo:
        j = pl.program_id(1)
        nj = pl.num_programs(1)
        h2_ref[0:halo, :] = jnp.where(j == 0, 0.0, norm_mod(xp_ref[0])).astype(BF16)
        h2_ref[halo + tf:2 * halo + tf, :] = jnp.where(j == nj - 1, 0.0, norm_mod(xn_ref[0])).astype(BF16)
    rows = tf + 2 * halo
    pos = lax.broadcasted_iota(jnp.int32, (rows, FFN_TILE), 0) & (period - 1)
    has_left = pos != 0
    has_right = pos != period - 1

    for n in range(wg_ref.shape[1] // FFN_TILE):
        cs = slice(n * FFN_TILE, (n + 1) * FFN_TILE)
        g = jnp.dot(h2_ref[...], wg_ref[:, cs], preferred_element_type=F32)
        g_left = jnp.where(has_left, pltpu.roll(g, 1, 0), 0.0)
        g_right = jnp.where(has_right, pltpu.roll(g, rows - 1, 0), 0.0)
        cw = cw_ref[:, cs]
        conv = cb_ref[:, cs]
        for dr, wrow in taps:
            lo = halo + dr * period
            conv = (conv + g_left[lo:lo + tf] * cw[3 * wrow:3 * wrow + 1]
                    + g[lo:lo + tf] * cw[3 * wrow + 1:3 * wrow + 2]
                    + g_right[lo:lo + tf] * cw[3 * wrow + 2:3 * wrow + 3])
        up = jnp.dot(h2_ref[halo:halo + tf, :], wu_ref[:, cs], preferred_element_type=F32)
        act_ref[:, cs] = (_gelu_tanh(conv) * up).astype(BF16)

    out = x + m[5:6] * jnp.dot(act_ref[...], wd_ref[...], preferred_element_type=F32)
    o_ref[0] = _rms(out) * fnw_ref[...] if final else out


def _ffn(x, mod, mod_row, n2w, wg, wu, wd, cw, cb, fnw, latent, final):
    b, t, d = x.shape
    if latent:
        tf, halo, period = FFN_ROWS, GRID_W, GRID_W
        taps = ((-1, 0), (0, 1), (1, 2))
        nb = tf // halo
        x_specs = [pl.BlockSpec((1, halo, d), lambda i, j: (i, jnp.maximum(j * nb - 1, 0), 0)),
                   pl.BlockSpec((1, tf, d), lambda i, j: (i, j, 0)),
                   pl.BlockSpec((1, halo, d), lambda i, j: (i, jnp.minimum((j + 1) * nb, t // halo - 1), 0))]
        x_args = (x, x, x)
    else:
        tf, halo, period = t, 0, t
        taps = ((0, 1),)
        x_specs = [pl.BlockSpec((1, tf, d), lambda i, j: (i, j, 0))]
        x_args = (x,)
    assert period & (period - 1) == 0 and t % tf == 0
    resident = lambda a: pl.BlockSpec(a.shape, lambda *_: (0,) * a.ndim, pipeline_mode=pl.Buffered(1))
    return pl.pallas_call(
        functools.partial(_ffn_kernel, tf=tf, halo=halo, period=period, taps=taps, final=final),
        grid=(b, t // tf),
        in_specs=x_specs + [
            pl.BlockSpec((1, N_MOD, d), lambda i, j: (mod_row(i), 0, 0)),
            _const_spec(n2w.shape), resident(wg), resident(wu), resident(wd),
            _const_spec(cw.shape), _const_spec(cb.shape), _const_spec(fnw.shape)],
        out_specs=pl.BlockSpec((1, tf, d), lambda i, j: (i, j, 0)),
        out_shape=jax.ShapeDtypeStruct((b, t, d), F32),
        scratch_shapes=[pltpu.VMEM((tf + 2 * halo, d), BF16), pltpu.VMEM((tf, wd.shape[0]), BF16)],
        compiler_params=_params("arbitrary", "arbitrary"),
        name="ffn",
    )(*x_args, mod, n2w, wg, wu, wd, cw, cb, fnw)


def _block_diag_tiles(w):
    nb, c, _ = w.shape
    per = GATE_TILE // c
    eye = jnp.eye(per, dtype=w.dtype)
    w = w.reshape(nb // per, per, c, c)
    return jnp.einsum('mjab,jk->mjakb', w, eye).reshape(nb // per, GATE_TILE, GATE_TILE)


def _layer(x, mod, mod_row, p, tab, states, latent, final):
    b, t, d = x.shape
    tc = min(t, TIME_CHUNK)
    proj = _in_proj(x, mod, mod_row, p['norm1_w'], p['w_in'], tc)
    hb, sbw, fin_h_bw, fin_s_bw = _backward_sweep(
        proj, tab, p['lru_conv_w'], p['lru_conv_b'], p['wg'], p['lp'], states[1], states[3], tc)
    x, fin_h_fw, fin_s_fw = _forward_sweep(
        x, mod, mod_row, proj, hb, sbw, tab, p['lru_conv_w'], p['lru_conv_b'], p['wg'], p['lp'],
        p['ret_gn_w'], p['w_out'], states[0], states[2], tc)
    x = _ffn(x, mod, mod_row, p['norm2_w'], p['ffn_wg'], p['ffn_wu'], p['ffn_wd'], p['ffn_cw'], p['ffn_cb'],
             p['final_norm_w'], latent, final)
    return x, (fin_h_fw, fin_h_bw, fin_s_fw, fin_s_bw)


def kernel(x_prompt, x_sample, state_lru_fw, state_lru_bw, state_ret_fw, state_ret_bw, c, c_ctx,
           norm1_w, w_mod, b_mod, w_in, lru_conv_w, lru_conv_b,
           lru_wa_fw, lru_ba_fw, lru_wx_fw, lru_bx_fw, lru_lambda_fw,
           lru_wa_bw, lru_ba_bw, lru_wx_bw, lru_bx_bw, lru_lambda_bw,
           ret_decay_fw, ret_decay_bw, ret_gn_w, w_out, norm2_w,
           ffn_w_gate, ffn_w_up, ffn_conv_w, ffn_conv_b, ffn_w_down, final_norm_w):
    depth = w_in.shape[0]
    bp = x_prompt.shape[0]
    bs, d = c.shape
    lw = lru_conv_w.shape[2]
    dh = RET_CHUNK

    ctx_row = bs
    rows = -(-(bs + 1) // 8) * 8
    cvec = jnp.zeros((rows, d), F32).at[:bs].set(c).at[ctx_row].set(c_ctx)
    row_ctx = lambda i: ctx_row
    row_lat = lambda i: i

    x_p, x_s = x_prompt, x_sample
    fins = []
    for l in range(depth):
        mod = _modulation(cvec, w_mod[l], b_mod[l][None]).reshape(rows, N_MOD, d)
        tab = _decay_tables(jnp.stack([ret_decay_fw[l], ret_decay_bw[l]]).astype(F32))
        p = dict(
            norm1_w=norm1_w[l][None], w_in=w_in[l].astype(BF16),
            lru_conv_w=lru_conv_w[l], lru_conv_b=lru_conv_b[l][None],
            wg=jnp.stack([jnp.stack([_block_diag_tiles(lru_wa_fw[l]), _block_diag_tiles(lru_wx_fw[l])]),
                          jnp.stack([_block_diag_tiles(lru_wa_bw[l]), _block_diag_tiles(lru_wx_bw[l])])]
                         ).astype(BF16),
            lp=jnp.stack([jnp.stack([lru_ba_fw[l], lru_bx_fw[l], lru_lambda_fw[l]]),
                          jnp.stack([lru_ba_bw[l], lru_bx_bw[l], lru_lambda_bw[l]])]).astype(F32),
            ret_gn_w=ret_gn_w[l][None], w_out=w_out[l].astype(BF16), norm2_w=norm2_w[l][None],
            ffn_wg=ffn_w_gate[l].astype(BF16), ffn_wu=ffn_w_up[l].astype(BF16), ffn_wd=ffn_w_down[l].astype(BF16),
            ffn_cw=ffn_conv_w[l].reshape(9, -1), ffn_cb=ffn_conv_b[l][None],
            final_norm_w=final_norm_w[None],
        )
        final = l == depth - 1
        zeros_h = jnp.zeros((bp, 1, lw), F32)
        zeros_s = jnp.zeros((bp, RET_HEADS, dh, dh), F32)
        x_p, fin = _layer(x_p, mod, row_ctx, p, tab, (zeros_h, zeros_h, zeros_s, zeros_s), False, final)
        fins.append(fin)
        init_lat = (state_lru_fw[:, l][:, None].astype(F32), state_lru_bw[:, l][:, None].astype(F32),
                    state_ret_fw[:, l].astype(F32), state_ret_bw[:, l].astype(F32))
        x_s, _ = _layer(x_s, mod, row_lat, p, tab, init_lat, True, final)

    dt = x_prompt.dtype
    new_lru_fw = jnp.concatenate([f[0] for f in fins], axis=1).astype(dt)
    new_lru_bw = jnp.concatenate([f[1] for f in fins], axis=1).astype(dt)
    new_ret_fw = jnp.stack([f[2] for f in fins], axis=1).astype(dt)
    new_ret_bw = jnp.stack([f[3] for f in fins], axis=1).astype(dt)
    return (x_p, x_s, new_lru_fw, new_lru_bw, new_ret_fw, new_ret_bw)
```

```python
import functools
import math

import jax
import jax.numpy as jnp
from jax import lax
from jax.experimental import pallas as pl
from jax.experimental.pallas import tpu as pltpu

F32 = jnp.float32
BF16 = jnp.bfloat16

N_MOD = 6
EPS = 1e-6
LRU_C = 8.0
LRU_BLOCKS = 8
LRU_CONV_WIDTH = 4
LRU_CONV_PAD_LEFT = 2
RET_HEADS = 4
RET_CHUNK = 128
GRID_W = 64
GATE_TILE = 256
FFN_TILE = 256
HALO_ROWS = 8
LANES = 128
SEG_ROWS = 4
TIME_CHUNK = 512
FFN_ROWS = 1024
VMEM_LIMIT = 56 * 1024 * 1024


def _dot(a, b):
    return jnp.dot(a.astype(BF16), b.astype(BF16), preferred_element_type=F32)


def _sigmoid(x):
    return 1.0 / (1.0 + jnp.exp(-x))


def _silu(x):
    return x * _sigmoid(x)


def _gelu_tanh(x):
    c = math.sqrt(2.0 / math.pi)
    return 0.5 * x * (1.0 + jnp.tanh(c * (x + 0.044715 * (x * x * x))))


def _softplus(x):
    return jnp.maximum(x, 0.0) + jnp.log1p(jnp.exp(-jnp.abs(x)))


def _rms(x):
    return x * lax.rsqrt(jnp.mean(x * x, axis=-1, keepdims=True) + EPS)


def _params(*sem):
    return pltpu.CompilerParams(dimension_semantics=sem, vmem_limit_bytes=VMEM_LIMIT)


def _const_spec(shape):
    nd = len(shape)
    return pl.BlockSpec(shape, lambda *_: (0,) * nd)


def _mod_kernel(c_ref, w_ref, b_ref, o_ref):
    o_ref[...] = _dot(_silu(c_ref[...]), w_ref[...]) + b_ref[...]


def _modulation(cvec, w, b):
    rows, d = cvec.shape
    n = w.shape[1]
    tn = n // 4
    return pl.pallas_call(
        _mod_kernel,
        grid=(n // tn,),
        in_specs=[pl.BlockSpec((rows, d), lambda j: (0, 0)),
                  pl.BlockSpec((d, tn), lambda j: (0, j)),
                  pl.BlockSpec((1, tn), lambda j: (0, j))],
        out_specs=pl.BlockSpec((rows, tn), lambda j: (0, j)),
        out_shape=jax.ShapeDtypeStruct((rows, n), F32),
        compiler_params=_params("arbitrary"),
        name="modulation",
    )(cvec, w, b)


TAB_INTRA, TAB_HEAD_FW, TAB_HEAD_BW, TAB_TAIL_FW, TAB_TAIL_BW, TAB_STEP_FW, TAB_STEP_BW = range(7)


def _tab_kernel(dec_ref, tab_ref):
    c = RET_CHUNK
    row = lax.broadcasted_iota(jnp.int32, (c, c), 0).astype(F32)
    col = lax.broadcasted_iota(jnp.int32, (c, c), 1).astype(F32)
    rel = row - col
    for h in range(RET_HEADS):
        lf = -_softplus(-jnp.full((c, c), dec_ref[0, h], F32))
        lb = -_softplus(-jnp.full((c, c), dec_ref[1, h], F32))
        tab_ref[TAB_INTRA, h] = jnp.where(rel > 0, jnp.exp(lf * rel),
                                          jnp.where(rel < 0, jnp.exp(lb * (-rel)), 2.0))
        tab_ref[TAB_HEAD_FW, h] = jnp.exp(lf * (row + 1.0))
        tab_ref[TAB_HEAD_BW, h] = jnp.exp(lb * (c - row))
        tab_ref[TAB_TAIL_FW, h] = jnp.exp(lf * (c - 1.0 - row))
        tab_ref[TAB_TAIL_BW, h] = jnp.exp(lb * row)
        tab_ref[TAB_STEP_FW, h] = jnp.exp(lf * c)
        tab_ref[TAB_STEP_BW, h] = jnp.exp(lb * c)


def _decay_tables(dec):
    return pl.pallas_call(
        _tab_kernel,
        in_specs=[pl.BlockSpec(memory_space=pltpu.SMEM)],
        out_shape=jax.ShapeDtypeStruct((7, RET_HEADS, RET_CHUNK, RET_CHUNK), F32),
        name="decay_tables",
    )(dec)


def _in_kernel(x_ref, mod_ref, nw_ref, w_ref, o_ref):
    m = mod_ref[0]
    h = (_rms(x_ref[0]) * nw_ref[...]) * (1.0 + m[1:2]) + m[0:1]
    o_ref[0] = _dot(h, w_ref[...])


def _in_proj(x, mod, mod_row, norm_w, w_in_bf, tm):
    b, t, d = x.shape
    n = w_in_bf.shape[1]
    return pl.pallas_call(
        _in_kernel,
        grid=(b, t // tm),
        in_specs=[pl.BlockSpec((1, tm, d), lambda i, j: (i, j, 0)),
                  pl.BlockSpec((1, N_MOD, d), lambda i, j: (mod_row(i), 0, 0)),
                  _const_spec((1, d)),
                  _const_spec((d, n))],
        out_specs=pl.BlockSpec((1, tm, n), lambda i, j: (i, j, 0)),
        out_shape=jax.ShapeDtypeStruct((b, t, n), F32),
        compiler_params=_params("arbitrary", "arbitrary"),
        name="in_proj",
    )(x, mod, norm_w, w_in_bf)


def _lru_conv(prev_ref, cur_ref, next_ref, cw_ref, cb_ref, first, last, tc):
    cur = cur_ref[0]
    ext = jnp.concatenate([jnp.where(first, 0.0, prev_ref[0]), cur, jnp.where(last, 0.0, next_ref[0])], axis=0)
    rows = tc + 2 * HALO_ROWS
    cw = cw_ref[...]
    acc = cb_ref[...] + cur * cw[LRU_CONV_PAD_LEFT:LRU_CONV_PAD_LEFT + 1]
    for j in range(LRU_CONV_WIDTH):
        off = j - LRU_CONV_PAD_LEFT
        if off:
            acc = acc + pltpu.roll(ext, (-off) % rows, 0)[HALO_ROWS:HALO_ROWS + tc] * cw[j:j + 1]
    return acc


def _lru_gates(xc, xc_bf, wg_ref, lp_ref, direction):
    lp = lp_ref[direction]
    width = xc.shape[1]

    def gate(g):
        parts = [jnp.dot(xc_bf[:, GATE_TILE * m:GATE_TILE * (m + 1)], wg_ref[direction, g, m],
                         preferred_element_type=F32) for m in range(width // GATE_TILE)]
        return jnp.concatenate(parts, axis=1)

    r = _sigmoid(gate(0) + lp[0:1])
    i = _sigmoid(gate(1) + lp[1:2])
    log_a = (-LRU_C * _softplus(-lp[2:3])) * r
    a = jnp.exp(log_a)
    u = jnp.sqrt(-jnp.tanh(log_a) * (a * a + 1.0)) * (i * xc)
    return a, u


def _store_slabs(ref, val):
    for k in range(ref.shape[0]):
        ref[k] = val[:, k * LANES:(k + 1) * LANES]


def _lru_scan(a_ref, u_ref, h_ref, h_pre, carry_ref, reverse, tc):
    nslab = a_ref.shape[0]
    blk = 8 * SEG_ROWS
    nblk = tc // blk
    row = lax.broadcasted_iota(jnp.int32, (8, LANES), 0)
    order = tuple(range(SEG_ROWS - 1, -1, -1)) if reverse else tuple(range(SEG_ROWS))
    edge = 0 if reverse else 7

    def earlier(x, d, fill):
        if reverse:
            return jnp.where(row < 8 - d, pltpu.roll(x, 8 - d, 0), fill)
        return jnp.where(row >= d, pltpu.roll(x, d, 0), fill)

    def body(bi, carry):
        base = pl.multiple_of(((nblk - 1 - bi) if reverse else bi) * blk, blk)
        for k in range(nslab):
            idx = [pl.ds(base + j, 8, stride=SEG_ROWS) for j in range(SEG_ROWS)]
            av = [a_ref[k, i, :] for i in idx]
            uv = [u_ref[k, i, :] for i in idx]
            h, p = uv[order[0]], av[order[0]]
            for j in order[1:]:
                h = av[j] * h + uv[j]
                p = av[j] * p
            d = 1
            while d < 8:
                h = p * earlier(h, d, 0.0) + h
                p = p * earlier(p, d, 1.0)
                d *= 2
            c_in = carry_ref[k]
            state = earlier(h, 1, 0.0) + earlier(p, 1, 1.0) * c_in
            carry_ref[k] = jnp.broadcast_to(h[edge:edge + 1] + p[edge:edge + 1] * c_in[edge:edge + 1], (8, LANES))
            for j in order:
                state = av[j] * state + uv[j]
                h_ref[(*h_pre, k, idx[j], slice(None))] = state
        return carry

    lax.fori_loop(0, nblk, body, 0)


def _kv_update(s_ref, tab_ref, k_scaled, v_bf, head, tail_tab, step_tab):
    kv = lax.dot_general((k_scaled * tab_ref[tail_tab, head]).astype(BF16), v_bf,
                         (((0,), (0,)), ((), ())), preferred_element_type=F32)
    s_ref[head] = tab_ref[step_tab, head] * s_ref[head] + kv


def _bw_kernel(prev_ref, cur_ref, next_ref, k_ref, v_ref, tab_ref, cw_ref, cb_ref, wg_ref, lp_ref,
               h0_ref, s0_ref, xc_ref, hbw_ref, sbw_ref, finh_ref, fins_ref, a_ref, u_ref, hcar_ref, s_ref,
               *, tc, nc):
    c = pl.program_id(1)
    chunk = nc - 1 - c
    dh = RET_CHUNK
    nslab = a_ref.shape[0]

    @pl.when(c == 0)
    def _():
        for k in range(nslab):
            hcar_ref[k] = jnp.broadcast_to(h0_ref[0, :, k * LANES:(k + 1) * LANES], (8, LANES))
        s_ref[...] = s0_ref[0]

    xc = _lru_conv(prev_ref, cur_ref, next_ref, cw_ref, cb_ref, chunk == 0, chunk == nc - 1, tc)
    xc_ref[0] = xc
    a, u = _lru_gates(xc, xc.astype(BF16), wg_ref, lp_ref, 1)
    _store_slabs(a_ref, a)
    _store_slabs(u_ref, u)
    _lru_scan(a_ref, u_ref, hbw_ref, (0,), hcar_ref, True, tc)

    @pl.when(c == 0)
    def _():
        for k in range(nslab):
            finh_ref[0, :, k * LANES:(k + 1) * LANES] = hbw_ref[0, k, tc - 1:tc, :]

    scale = dh ** -0.5
    for n in reversed(range(tc // RET_CHUNK)):
        rows = slice(n * RET_CHUNK, (n + 1) * RET_CHUNK)
        for hd in range(RET_HEADS):
            cols = slice(hd * dh, (hd + 1) * dh)
            sbw_ref[0, n, hd] = s_ref[hd].astype(BF16)
            _kv_update(s_ref, tab_ref, k_ref[0, rows, cols] * scale, v_ref[0, rows, cols].astype(BF16),
                       hd, TAB_TAIL_BW, TAB_STEP_BW)

    @pl.when(c == nc - 1)
    def _():
        fins_ref[0] = s_ref[...]


def _backward_sweep(proj, tab, cw, cb, wg, lp, h0, s0, tc):
    b, t, _ = proj.shape
    lw = cw.shape[1]
    nc = t // tc
    nh = tc // HALO_ROWS
    nsub = tc // RET_CHUNK
    nslab = lw // LANES
    dh = RET_CHUNK
    ch = lambda j: nc - 1 - j
    return pl.pallas_call(
        functools.partial(_bw_kernel, tc=tc, nc=nc),
        grid=(b, nc),
        in_specs=[
            pl.BlockSpec((1, HALO_ROWS, lw), lambda i, j: (i, jnp.maximum(ch(j) * nh - 1, 0), 0)),
            pl.BlockSpec((1, tc, lw), lambda i, j: (i, ch(j), 0)),
            pl.BlockSpec((1, HALO_ROWS, lw), lambda i, j: (i, jnp.minimum((ch(j) + 1) * nh, t // HALO_ROWS - 1), 0)),
            pl.BlockSpec((1, tc, lw), lambda i, j: (i, ch(j), 3)),
            pl.BlockSpec((1, tc, lw), lambda i, j: (i, ch(j), 4)),
            _const_spec(tab.shape), _const_spec(cw.shape), _const_spec(cb.shape),
            _const_spec(wg.shape), _const_spec(lp.shape),
            pl.BlockSpec((1, 1, lw), lambda i, j: (i, 0, 0)),
            pl.BlockSpec((1, RET_HEADS, dh, dh), lambda i, j: (i, 0, 0, 0)),
        ],
        out_specs=[
            pl.BlockSpec((1, tc, lw), lambda i, j: (i, ch(j), 0)),
            pl.BlockSpec((1, nslab, tc, LANES), lambda i, j: (i, 0, ch(j), 0)),
            pl.BlockSpec((1, nsub, RET_HEADS, dh, dh), lambda i, j: (i, ch(j), 0, 0, 0)),
            pl.BlockSpec((1, 1, lw), lambda i, j: (i, 0, 0)),
            pl.BlockSpec((1, RET_HEADS, dh, dh), lambda i, j: (i, 0, 0, 0)),
        ],
        out_shape=[
            jax.ShapeDtypeStruct((b, t, lw), F32),
            jax.ShapeDtypeStruct((b, nslab, t, LANES), F32),
            jax.ShapeDtypeStruct((b, t // RET_CHUNK, RET_HEADS, dh, dh), BF16),
            jax.ShapeDtypeStruct((b, 1, lw), F32),
            jax.ShapeDtypeStruct((b, RET_HEADS, dh, dh), F32),
        ],
        scratch_shapes=[pltpu.VMEM((nslab, tc, LANES), F32),
                        pltpu.VMEM((nslab, tc, LANES), F32),
                        pltpu.VMEM((nslab, 8, LANES), F32),
                        pltpu.VMEM((RET_HEADS, dh, dh), F32)],
        compiler_params=_params("arbitrary", "arbitrary"),
        name="backward_sweep",
    )(proj, proj, proj, proj, proj, tab, cw, cb, wg, lp, h0, s0)


def _fw_kernel(x_ref, mod_ref, xc_ref, lg_ref, q_ref, k_ref, v_ref, rg_ref,
               hbw_ref, sbw_ref, tab_ref, wg_ref, lp_ref, gnw_ref, wout_ref, h0_ref, s0_ref,
               o_ref, finh_ref, fins_ref, a_ref, u_ref, h_ref, hcar_ref, s_ref, y_ref, *, tc, nc):
    c = pl.program_id(1)
    dh = RET_CHUNK
    lw = xc_ref.shape[2]
    nslab = a_ref.shape[0]

    @pl.when(c == 0)
    def _():
        for k in range(nslab):
            hcar_ref[k] = jnp.broadcast_to(h0_ref[0, :, k * LANES:(k + 1) * LANES], (8, LANES))
        s_ref[...] = s0_ref[0]

    xc = xc_ref[0]
    a, u = _lru_gates(xc, xc.astype(BF16), wg_ref, lp_ref, 0)
    _store_slabs(a_ref, a)
    _store_slabs(u_ref, u)
    _lru_scan(a_ref, u_ref, h_ref, (), hcar_ref, False, tc)
    for k in range(nslab):
        cols = slice(k * LANES, (k + 1) * LANES)
        y_ref[:, cols] = ((h_ref[k] + hbw_ref[0, k]) * _gelu_tanh(lg_ref[0, :, cols])).astype(BF16)

    @pl.when(c == 0)
    def _():
        for k in range(nslab):
            finh_ref[0, :, k * LANES:(k + 1) * LANES] = h_ref[k, 0:1, :]

    scale = dh ** -0.5
    gnw = gnw_ref[...]
    for n in range(tc // RET_CHUNK):
        rows = slice(n * RET_CHUNK, (n + 1) * RET_CHUNK)
        for hd in range(RET_HEADS):
            cols = slice(hd * dh, (hd + 1) * dh)
            q = q_ref[0, rows, cols]
            k = k_ref[0, rows, cols] * scale
            v_bf = v_ref[0, rows, cols].astype(BF16)
            scores = lax.dot_general(q.astype(BF16), k.astype(BF16), (((1,), (1,)), ((), ())),
                                     preferred_element_type=F32)
            intra = _dot(scores * tab_ref[TAB_INTRA, hd], v_bf)
            q_both = jnp.concatenate([q * tab_ref[TAB_HEAD_FW, hd], q * tab_ref[TAB_HEAD_BW, hd]], axis=1)
            s_both = jnp.concatenate([s_ref[hd].astype(BF16), sbw_ref[0, n, hd]], axis=0)
            o = intra + _dot(q_both, s_both)
            dev = o - jnp.mean(o, axis=-1, keepdims=True)
            on = dev * lax.rsqrt(jnp.mean(dev * dev, axis=-1, keepdims=True) + EPS) * gnw[:, cols]
            y_ref[rows, lw + hd * dh:lw + (hd + 1) * dh] = (on * _silu(rg_ref[0, rows, cols])).astype(BF16)
            _kv_update(s_ref, tab_ref, k, v_bf, hd, TAB_TAIL_FW, TAB_STEP_FW)

    o_ref[0] = x_ref[0] + mod_ref[0][2:3] * _dot(y_ref[...], wout_ref[...])

    @pl.when(c == nc - 1)
    def _():
        fins_ref[0] = s_ref[...]


def _forward_sweep(x, mod, mod_row, proj, xc, hbw, sbw, tab, wg, lp, gnw, wout_bf, h0, s0, tc):
    b, t, d = x.shape
    lw = xc.shape[2]
    nc = t // tc
    nsub = tc // RET_CHUNK
    nslab = lw // LANES
    dh = RET_CHUNK
    col = lambda k: pl.BlockSpec((1, tc, lw), lambda i, j: (i, j, k))
    return pl.pallas_call(
        functools.partial(_fw_kernel, tc=tc, nc=nc),
        grid=(b, nc),
        in_specs=[
            pl.BlockSpec((1, tc, d), lambda i, j: (i, j, 0)),
            pl.BlockSpec((1, N_MOD, d), lambda i, j: (mod_row(i), 0, 0)),
            col(0),
            col(1), col(2), col(3), col(4), col(5),
            pl.BlockSpec((1, nslab, tc, LANES), lambda i, j: (i, 0, j, 0)),
            pl.BlockSpec((1, nsub, RET_HEADS, dh, dh), lambda i, j: (i, j, 0, 0, 0)),
            _const_spec(tab.shape),
            _const_spec(wg.shape), _const_spec(lp.shape), _const_spec(gnw.shape), _const_spec(wout_bf.shape),
            pl.BlockSpec((1, 1, lw), lambda i, j: (i, 0, 0)),
            pl.BlockSpec((1, RET_HEADS, dh, dh), lambda i, j: (i, 0, 0, 0)),
        ],
        out_specs=[
            pl.BlockSpec((1, tc, d), lambda i, j: (i, j, 0)),
            pl.BlockSpec((1, 1, lw), lambda i, j: (i, 0, 0)),
            pl.BlockSpec((1, RET_HEADS, dh, dh), lambda i, j: (i, 0, 0, 0)),
        ],
        out_shape=[
            jax.ShapeDtypeStruct((b, t, d), F32),
            jax.ShapeDtypeStruct((b, 1, lw), F32),
            jax.ShapeDtypeStruct((b, RET_HEADS, dh, dh), F32),
        ],
        scratch_shapes=[pltpu.VMEM((nslab, tc, LANES), F32),
                        pltpu.VMEM((nslab, tc, LANES), F32),
                        pltpu.VMEM((nslab, tc, LANES), F32),
                        pltpu.VMEM((nslab, 8, LANES), F32),
                        pltpu.VMEM((RET_HEADS, dh, dh), F32),
                        pltpu.VMEM((tc, wout_bf.shape[0]), BF16)],
        compiler_params=_params("arbitrary", "arbitrary"),
        name="forward_sweep",
    )(x, mod, xc, proj, proj, proj, proj, proj, hbw, sbw, tab, wg, lp, gnw, wout_bf, h0, s0)


def _ffn_kernel(*refs, tf, halo, period, taps, final):
    if halo:
        xp_ref, x_ref, xn_ref = refs[:3]
        refs = refs[3:]
    else:
        x_ref = refs[0]
        refs = refs[1:]
    mod_ref, n2w_ref, wg_ref, wu_ref, wd_ref, cw_ref, cb_ref, fnw_ref, o_ref, h2_ref, act_ref = refs
    m = mod_ref[0]
    n2w = n2w_ref[...]

    def norm_mod(x):
        return (_rms(x) * n2w) * (1.0 + m[4:5]) + m[3:4]

    x = x_ref[0]
    h2_ref[halo:halo + tf, :] = norm_mod(x).astype(BF16)
    if halo:
        j = pl.program_id(1)
        nj = pl.num_programs(1)
        h2_ref[0:halo, :] = jnp.where(j == 0, 0.0, norm_mod(xp_ref[0])).astype(BF16)
        h2_ref[halo + tf:2 * halo + tf, :] = jnp.where(j == nj - 1, 0.0, norm_mod(xn_ref[0])).astype(BF16)
    rows = tf + 2 * halo
    pos = lax.broadcasted_iota(jnp.int32, (rows, FFN_TILE), 0) & (period - 1)
    has_left = pos != 0
    has_right = pos != period - 1

    for n in range(wg_ref.shape[1] // FFN_TILE):
        cs = slice(n * FFN_TILE, (n + 1) * FFN_TILE)
        g = jnp.dot(h2_ref[...], wg_ref[:, cs], preferred_element_type=F32)
        g_left = jnp.where(has_left, pltpu.roll(g, 1, 0), 0.0)
        g_right = jnp.where(has_right, pltpu.roll(g, rows - 1, 0), 0.0)
        cw = cw_ref[:, cs]
        conv = cb_ref[:, cs]
        for dr, wrow in taps:
            lo = halo + dr * period
            conv = (conv + g_left[lo:lo + tf] * cw[3 * wrow:3 * wrow + 1]
                    + g[lo:lo + tf] * cw[3 * wrow + 1:3 * wrow + 2]
                    + g_right[lo:lo + tf] * cw[3 * wrow + 2:3 * wrow + 3])
        up = jnp.dot(h2_ref[halo:halo + tf, :], wu_ref[:, cs], preferred_element_type=F32)
        act_ref[:, cs] = (_gelu_tanh(conv) * up).astype(BF16)

    out = x + m[5:6] * jnp.dot(act_ref[...], wd_ref[...], preferred_element_type=F32)
    o_ref[0] = _rms(out) * fnw_ref[...] if final else out


def _ffn(x, mod, mod_row, n2w, wg, wu, wd, cw, cb, fnw, latent, final):
    b, t, d = x.shape
    if latent:
        tf, halo, period = FFN_ROWS, GRID_W, GRID_W
        taps = ((-1, 0), (0, 1), (1, 2))
        nb = tf // halo
        x_specs = [pl.BlockSpec((1, halo, d), lambda i, j: (i, jnp.maximum(j * nb - 1, 0), 0)),
                   pl.BlockSpec((1, tf, d), lambda i, j: (i, j, 0)),
                   pl.BlockSpec((1, halo, d), lambda i, j: (i, jnp.minimum((j + 1) * nb, t // halo - 1), 0))]
        x_args = (x, x, x)
    else:
        tf, halo, period = t, 0, t
        taps = ((0, 1),)
        x_specs = [pl.BlockSpec((1, tf, d), lambda i, j: (i, j, 0))]
        x_args = (x,)
    assert period & (period - 1) == 0 and t % tf == 0
    resident = lambda a: pl.BlockSpec(a.shape, lambda *_: (0,) * a.ndim, pipeline_mode=pl.Buffered(1))
    return pl.pallas_call(
        functools.partial(_ffn_kernel, tf=tf, halo=halo, period=period, taps=taps, final=final),
        grid=(b, t // tf),
        in_specs=x_specs + [
            pl.BlockSpec((1, N_MOD, d), lambda i, j: (mod_row(i), 0, 0)),
            _const_spec(n2w.shape), resident(wg), resident(wu), resident(wd),
            _const_spec(cw.shape), _const_spec(cb.shape), _const_spec(fnw.shape)],
        out_specs=pl.BlockSpec((1, tf, d), lambda i, j: (i, j, 0)),
        out_shape=jax.ShapeDtypeStruct((b, t, d), F32),
        scratch_shapes=[pltpu.VMEM((tf + 2 * halo, d), BF16), pltpu.VMEM((tf, wd.shape[0]), BF16)],
        compiler_params=_params("arbitrary", "arbitrary"),
        name="ffn",
    )(*x_args, mod, n2w, wg, wu, wd, cw, cb, fnw)


def _block_diag_tiles(w):
    nb, c, _ = w.shape
    per = GATE_TILE // c
    eye = jnp.eye(per, dtype=w.dtype)
    w = w.reshape(nb // per, per, c, c)
    return jnp.einsum('mjab,jk->mjakb', w, eye).reshape(nb // per, GATE_TILE, GATE_TILE)


def _layer(x, mod, mod_row, p, tab, states, latent, final):
    b, t, d = x.shape
    tc = min(t, TIME_CHUNK)
    proj = _in_proj(x, mod, mod_row, p['norm1_w'], p['w_in'], tc)
    xc, hbw, sbw, fin_h_bw, fin_s_bw = _backward_sweep(
        proj, tab, p['lru_conv_w'], p['lru_conv_b'], p['wg'], p['lp'], states[1], states[3], tc)
    x, fin_h_fw, fin_s_fw = _forward_sweep(
        x, mod, mod_row, proj, xc, hbw, sbw, tab, p['wg'], p['lp'], p['ret_gn_w'], p['w_out'],
        states[0], states[2], tc)
    x = _ffn(x, mod, mod_row, p['norm2_w'], p['ffn_wg'], p['ffn_wu'], p['ffn_wd'], p['ffn_cw'], p['ffn_cb'],
             p['final_norm_w'], latent, final)
    return x, (fin_h_fw, fin_h_bw, fin_s_fw, fin_s_bw)


def kernel(x_prompt, x_sample, state_lru_fw, state_lru_bw, state_ret_fw, state_ret_bw, c, c_ctx,
           norm1_w, w_mod, b_mod, w_in, lru_conv_w, lru_conv_b,
           lru_wa_fw, lru_ba_fw, lru_wx_fw, lru_bx_fw, lru_lambda_fw,
           lru_wa_bw, lru_ba_bw, lru_wx_bw, lru_bx_bw, lru_lambda_bw,
           ret_decay_fw, ret_decay_bw, ret_gn_w, w_out, norm2_w,
           ffn_w_gate, ffn_w_up, ffn_conv_w, ffn_conv_b, ffn_w_down, final_norm_w):
    depth = w_in.shape[0]
    bp = x_prompt.shape[0]
    bs, d = c.shape
    lw = lru_conv_w.shape[2]
    dh = RET_CHUNK

    ctx_row = bs
    rows = -(-(bs + 1) // 8) * 8
    cvec = jnp.zeros((rows, d), F32).at[:bs].set(c).at[ctx_row].set(c_ctx)
    row_ctx = lambda i: ctx_row
    row_lat = lambda i: i

    x_p, x_s = x_prompt, x_sample
    fins = []
    for l in range(depth):
        mod = _modulation(cvec, w_mod[l], b_mod[l][None]).reshape(rows, N_MOD, d)
        tab = _decay_tables(jnp.stack([ret_decay_fw[l], ret_decay_bw[l]]).astype(F32))
        p = dict(
            norm1_w=norm1_w[l][None], w_in=w_in[l].astype(BF16),
            lru_conv_w=lru_conv_w[l], lru_conv_b=lru_conv_b[l][None],
            wg=jnp.stack([jnp.stack([_block_diag_tiles(lru_wa_fw[l]), _block_diag_tiles(lru_wx_fw[l])]),
                          jnp.stack([_block_diag_tiles(lru_wa_bw[l]), _block_diag_tiles(lru_wx_bw[l])])]
                         ).astype(BF16),
            lp=jnp.stack([jnp.stack([lru_ba_fw[l], lru_bx_fw[l], lru_lambda_fw[l]]),
                          jnp.stack([lru_ba_bw[l], lru_bx_bw[l], lru_lambda_bw[l]])]).astype(F32),
            ret_gn_w=ret_gn_w[l][None], w_out=w_out[l].astype(BF16), norm2_w=norm2_w[l][None],
            ffn_wg=ffn_w_gate[l].astype(BF16), ffn_wu=ffn_w_up[l].astype(BF16), ffn_wd=ffn_w_down[l].astype(BF16),
            ffn_cw=ffn_conv_w[l].reshape(9, -1), ffn_cb=ffn_conv_b[l][None],
            final_norm_w=final_norm_w[None],
        )
        final = l == depth - 1
        zeros_h = jnp.zeros((bp, 1, lw), F32)
        zeros_s = jnp.zeros((bp, RET_HEADS, dh, dh), F32)
        x_p, fin = _layer(x_p, mod, row_ctx, p, tab, (zeros_h, zeros_h, zeros_s, zeros_s), False, final)
        fins.append(fin)
        init_lat = (state_lru_fw[:, l][:, None].astype(F32), state_lru_bw[:, l][:, None].astype(F32),
                    state_ret_fw[:, l].astype(F32), state_ret_bw[:, l].astype(F32))
        x_s, _ = _layer(x_s, mod, row_lat, p, tab, init_lat, True, final)

    dt = x_prompt.dtype
    new_lru_fw = jnp.concatenate([f[0] for f in fins], axis=1).astype(dt)
    new_lru_bw = jnp.concatenate([f[1] for f in fins], axis=1).astype(dt)
    new_ret_fw = jnp.stack([f[2] for f in fins], axis=1).astype(dt)
    new_ret_bw = jnp.stack([f[3] for f in fins], axis=1).astype(dt)
    return (x_p, x_s, new_lru_fw, new_lru_bw, new_ret_fw, new_ret_bw)
```

```python
import functools
import math

import jax
import jax.numpy as jnp
from jax import lax
from jax.experimental import pallas as pl
from jax.experimental.pallas import tpu as pltpu

F32 = jnp.float32
BF16 = jnp.bfloat16

N_MOD = 6
EPS = 1e-6
LRU_C = 8.0
LRU_BLOCKS = 8
LRU_CONV_WIDTH = 4
LRU_CONV_PAD_LEFT = 2
RET_HEADS = 4
RET_CHUNK = 128
GRID_W = 64
GATE_TILE = 256
FFN_TILE = 256
HALO_ROWS = 8
LEAD_ROWS = 16
LANES = 128
SEG_ROWS = 4
TIME_CHUNK = 512
FFN_ROWS = 512
VMEM_LIMIT = 56 * 1024 * 1024


def _dot(a, b):
    return jnp.dot(a.astype(BF16), b.astype(BF16), preferred_element_type=F32)


def _sigmoid(x):
    return 1.0 / (1.0 + jnp.exp(-x))


def _silu(x):
    return x * _sigmoid(x)


def _gelu_tanh(x):
    c = math.sqrt(2.0 / math.pi)
    return 0.5 * x * (1.0 + jnp.tanh(c * (x + 0.044715 * (x * x * x))))


def _softplus(x):
    return jnp.maximum(x, 0.0) + jnp.log1p(jnp.exp(-jnp.abs(x)))


def _rms(x):
    return x * lax.rsqrt(jnp.mean(x * x, axis=-1, keepdims=True) + EPS)


def _params(*sem):
    return pltpu.CompilerParams(dimension_semantics=sem, vmem_limit_bytes=VMEM_LIMIT)


def _const_spec(shape):
    nd = len(shape)
    return pl.BlockSpec(shape, lambda *_: (0,) * nd)


def _mod_kernel(c_ref, w_ref, b_ref, o_ref):
    o_ref[...] = _dot(_silu(c_ref[...]), w_ref[...]) + b_ref[...]


def _modulation(cvec, w, b):
    rows, d = cvec.shape
    n = w.shape[1]
    tn = n // 4
    return pl.pallas_call(
        _mod_kernel,
        grid=(n // tn,),
        in_specs=[pl.BlockSpec((rows, d), lambda j: (0, 0)),
                  pl.BlockSpec((d, tn), lambda j: (0, j)),
                  pl.BlockSpec((1, tn), lambda j: (0, j))],
        out_specs=pl.BlockSpec((rows, tn), lambda j: (0, j)),
        out_shape=jax.ShapeDtypeStruct((rows, n), F32),
        compiler_params=_params("arbitrary"),
        name="modulation",
    )(cvec, w, b)


TAB_INTRA, TAB_HEAD_FW, TAB_HEAD_BW, TAB_TAIL_FW, TAB_TAIL_BW, TAB_STEP_FW, TAB_STEP_BW = range(7)


def _tab_kernel(dec_ref, tab_ref):
    c = RET_CHUNK
    row = lax.broadcasted_iota(jnp.int32, (c, c), 0).astype(F32)
    col = lax.broadcasted_iota(jnp.int32, (c, c), 1).astype(F32)
    rel = row - col
    for h in range(RET_HEADS):
        lf = -_softplus(-jnp.full((c, c), dec_ref[0, h], F32))
        lb = -_softplus(-jnp.full((c, c), dec_ref[1, h], F32))
        tab_ref[TAB_INTRA, h] = jnp.where(rel > 0, jnp.exp(lf * rel),
                                          jnp.where(rel < 0, jnp.exp(lb * (-rel)), 2.0))
        tab_ref[TAB_HEAD_FW, h] = jnp.exp(lf * (row + 1.0))
        tab_ref[TAB_HEAD_BW, h] = jnp.exp(lb * (c - row))
        tab_ref[TAB_TAIL_FW, h] = jnp.exp(lf * (c - 1.0 - row))
        tab_ref[TAB_TAIL_BW, h] = jnp.exp(lb * row)
        tab_ref[TAB_STEP_FW, h] = jnp.exp(lf * c)
        tab_ref[TAB_STEP_BW, h] = jnp.exp(lb * c)


def _decay_tables(dec):
    return pl.pallas_call(
        _tab_kernel,
        in_specs=[pl.BlockSpec(memory_space=pltpu.SMEM)],
        out_shape=jax.ShapeDtypeStruct((7, RET_HEADS, RET_CHUNK, RET_CHUNK), F32),
        name="decay_tables",
    )(dec)


def _norm_mod(x, m, nw):
    return _rms(x) * (nw * (1.0 + m[1:2])) + m[0:1]


def _lru_conv(prev, cur, nxt, cw_ref, cb_ref):
    tc = cur.shape[0]
    lead = prev.shape[0]
    ext = jnp.concatenate([prev, cur, nxt], axis=0)
    rows = ext.shape[0]
    cw = cw_ref[...]
    acc = cb_ref[...] + cur * cw[LRU_CONV_PAD_LEFT:LRU_CONV_PAD_LEFT + 1]
    for j in range(LRU_CONV_WIDTH):
        off = j - LRU_CONV_PAD_LEFT
        if off:
            acc = acc + pltpu.roll(ext, (-off) % rows, 0)[lead:lead + tc] * cw[j:j + 1]
    return acc


def _lru_gates(xc, xc_bf, wg_ref, lp_ref, direction):
    lp = lp_ref[direction]
    width = xc.shape[1]

    def gate(g):
        parts = [jnp.dot(xc_bf[:, GATE_TILE * m:GATE_TILE * (m + 1)], wg_ref[direction, g, m],
                         preferred_element_type=F32) for m in range(width // GATE_TILE)]
        return jnp.concatenate(parts, axis=1)

    r = _sigmoid(gate(0) + lp[0:1])
    i = _sigmoid(gate(1) + lp[1:2])
    log_a = (-LRU_C * _softplus(-lp[2:3])) * r
    a = jnp.exp(log_a)
    u = jnp.sqrt(-jnp.tanh(log_a) * (a * a + 1.0)) * (i * xc)
    return a, u


def _store_slabs(ref, val):
    for k in range(ref.shape[0]):
        ref[k] = val[:, k * LANES:(k + 1) * LANES]


def _lru_scan(a_ref, u_ref, h_ref, h_pre, carry_ref, reverse, tc):
    nslab = a_ref.shape[0]
    blk = 8 * SEG_ROWS
    nblk = tc // blk
    row = lax.broadcasted_iota(jnp.int32, (8, LANES), 0)
    order = tuple(range(SEG_ROWS - 1, -1, -1)) if reverse else tuple(range(SEG_ROWS))
    edge = 0 if reverse else 7

    def earlier(x, d, fill):
        if reverse:
            return jnp.where(row < 8 - d, pltpu.roll(x, 8 - d, 0), fill)
        return jnp.where(row >= d, pltpu.roll(x, d, 0), fill)

    carry = [carry_ref[k] for k in range(nslab)]
    for bi in range(nblk):
        base = ((nblk - 1 - bi) if reverse else bi) * blk
        for k in range(nslab):
            idx = [pl.ds(base + j, 8, stride=SEG_ROWS) for j in range(SEG_ROWS)]
            av = [a_ref[k, i, :] for i in idx]
            uv = [u_ref[k, i, :] for i in idx]
            h, p = uv[order[0]], av[order[0]]
            for j in order[1:]:
                h = av[j] * h + uv[j]
                p = av[j] * p
            d = 1
            while d < 8:
                h = p * earlier(h, d, 0.0) + h
                p = p * earlier(p, d, 1.0)
                d *= 2
            c_in = carry[k]
            state = earlier(h, 1, 0.0) + earlier(p, 1, 1.0) * c_in
            carry[k] = jnp.broadcast_to(h[edge:edge + 1] + p[edge:edge + 1] * c_in[edge:edge + 1], (8, LANES))
            for j in order:
                state = av[j] * state + uv[j]
                h_ref[(*h_pre, k, idx[j], slice(None))] = state
    for k in range(nslab):
        carry_ref[k] = carry[k]


def _kv_update(s_ref, tab_ref, k_tail_bf, v_bf, head, step_tab):
    kv = lax.dot_general(k_tail_bf, v_bf, (((0,), (0,)), ((), ())), preferred_element_type=F32)
    s_ref[head] = tab_ref[step_tab, head] * s_ref[head] + kv


def _bw_kernel(xprev_ref, x_ref, mod_ref, nw_ref, w_ref, tab_ref, cw_ref, cb_ref, wg_ref, lp_ref, h0_ref, s0_ref,
               xc_ref, hbw_ref, sbw_ref, kb_ref, ktf_ref, vb_ref, finh_ref, fins_ref,
               a_ref, u_ref, hcar_ref, s_ref, nx_ref, *, tc, nc):
    c = pl.program_id(1)
    chunk = nc - 1 - c
    dh = RET_CHUNK
    nslab = a_ref.shape[0]
    lw = nslab * LANES

    @pl.when(c == 0)
    def _():
        for k in range(nslab):
            hcar_ref[k] = jnp.broadcast_to(h0_ref[0, :, k * LANES:(k + 1) * LANES], (8, LANES))
        s_ref[...] = s0_ref[0]
        nx_ref[...] = jnp.zeros_like(nx_ref)

    m = mod_ref[0]
    nw = nw_ref[...]
    h_cur = _norm_mod(x_ref[0], m, nw).astype(BF16)
    h_ext = jnp.concatenate([_norm_mod(xprev_ref[0], m, nw).astype(BF16), h_cur], axis=0)
    lx_ext = jnp.dot(h_ext, w_ref[:, 0:lw], preferred_element_type=F32)
    kv = jnp.dot(h_cur, w_ref[:, lw:], preferred_element_type=F32)
    lead = xprev_ref.shape[1]
    lx = lx_ext[lead:]
    lx_prev = jnp.where(chunk == 0, 0.0, lx_ext[0:lead])
    xc = _lru_conv(lx_prev, lx, nx_ref[...], cw_ref, cb_ref)
    nx_ref[...] = lx[0:nx_ref.shape[0]]
    xc_ref[0] = xc

    a, u = _lru_gates(xc, xc.astype(BF16), wg_ref, lp_ref, 1)
    _store_slabs(a_ref, a)
    _store_slabs(u_ref, u)
    _lru_scan(a_ref, u_ref, hbw_ref, (0,), hcar_ref, True, tc)

    k = kv[:, 0:RET_HEADS * dh] * dh ** -0.5
    v_bf = kv[:, RET_HEADS * dh:].astype(BF16)
    kb_ref[0] = k.astype(BF16)
    vb_ref[0] = v_bf
    for n in reversed(range(tc // RET_CHUNK)):
        rows = slice(n * RET_CHUNK, (n + 1) * RET_CHUNK)
        for hd in range(RET_HEADS):
            cols = slice(hd * dh, (hd + 1) * dh)
            sbw_ref[0, n, hd] = s_ref[hd].astype(BF16)
            ktf_ref[0, rows, cols] = (k[rows, cols] * tab_ref[TAB_TAIL_FW, hd]).astype(BF16)
            _kv_update(s_ref, tab_ref, (k[rows, cols] * tab_ref[TAB_TAIL_BW, hd]).astype(BF16), v_bf[rows, cols],
                       hd, TAB_STEP_BW)

    @pl.when(c == 0)
    def _():
        for k in range(nslab):
            finh_ref[0, :, k * LANES:(k + 1) * LANES] = hbw_ref[0, k, tc - 1:tc, :]

    @pl.when(c == nc - 1)
    def _():
        fins_ref[0] = s_ref[...]


def _resident(a):
    return pl.BlockSpec(a.shape, lambda *_: (0,) * a.ndim, pipeline_mode=pl.Buffered(1))


def _backward_sweep(x, mod, mod_row, nw, w_bw, tab, cw, cb, wg, lp, h0, s0, tc):
    b, t, d = x.shape
    lw = cw.shape[1]
    nc = t // tc
    nlead = tc // LEAD_ROWS
    nsub = tc // RET_CHUNK
    nslab = lw // LANES
    dh = RET_CHUNK
    ch = lambda j: nc - 1 - j
    tok = lambda dt: jax.ShapeDtypeStruct((b, t, lw), dt)
    tok_spec = pl.BlockSpec((1, tc, lw), lambda i, j: (i, ch(j), 0))
    return pl.pallas_call(
        functools.partial(_bw_kernel, tc=tc, nc=nc),
        grid=(b, nc),
        in_specs=[
            pl.BlockSpec((1, LEAD_ROWS, d), lambda i, j: (i, jnp.maximum(ch(j) * nlead - 1, 0), 0)),
            pl.BlockSpec((1, tc, d), lambda i, j: (i, ch(j), 0)),
            pl.BlockSpec((1, N_MOD, d), lambda i, j: (mod_row(i), 0, 0)),
            _const_spec(nw.shape), _resident(w_bw), _resident(tab), _const_spec(cw.shape), _const_spec(cb.shape),
            _resident(wg), _const_spec(lp.shape),
            pl.BlockSpec((1, 1, lw), lambda i, j: (i, 0, 0)),
            pl.BlockSpec((1, RET_HEADS, dh, dh), lambda i, j: (i, 0, 0, 0)),
        ],
        out_specs=[
            tok_spec,
            pl.BlockSpec((1, nslab, tc, LANES), lambda i, j: (i, 0, ch(j), 0)),
            pl.BlockSpec((1, nsub, RET_HEADS, dh, dh), lambda i, j: (i, ch(j), 0, 0, 0)),
            tok_spec, tok_spec, tok_spec,
            pl.BlockSpec((1, 1, lw), lambda i, j: (i, 0, 0)),
            pl.BlockSpec((1, RET_HEADS, dh, dh), lambda i, j: (i, 0, 0, 0)),
        ],
        out_shape=[
            tok(F32),
            jax.ShapeDtypeStruct((b, nslab, t, LANES), F32),
            jax.ShapeDtypeStruct((b, t // RET_CHUNK, RET_HEADS, dh, dh), BF16),
            tok(BF16), tok(BF16), tok(BF16),
            jax.ShapeDtypeStruct((b, 1, lw), F32),
            jax.ShapeDtypeStruct((b, RET_HEADS, dh, dh), F32),
        ],
        scratch_shapes=[pltpu.VMEM((nslab, tc, LANES), F32),
                        pltpu.VMEM((nslab, tc, LANES), F32),
                        pltpu.VMEM((nslab, 8, LANES), F32),
                        pltpu.VMEM((RET_HEADS, dh, dh), F32),
                        pltpu.VMEM((HALO_ROWS, lw), F32)],
        compiler_params=_params("arbitrary", "arbitrary"),
        name="backward_sweep",
    )(x, x, mod, nw, w_bw, tab, cw, cb, wg, lp, h0, s0)


def _fw_kernel(x_ref, mod_ref, nw_ref, w_ref, xc_ref, hbw_ref, sbw_ref, kb_ref, ktf_ref, vb_ref,
               tab_ref, wg_ref, lp_ref, gnw_ref, wout_ref, h0_ref, s0_ref,
               o_ref, finh_ref, fins_ref, a_ref, u_ref, h_ref, hcar_ref, s_ref, y_ref, *, tc, nc):
    c = pl.program_id(1)
    dh = RET_CHUNK
    lw = xc_ref.shape[2]
    nslab = a_ref.shape[0]

    @pl.when(c == 0)
    def _():
        for k in range(nslab):
            hcar_ref[k] = jnp.broadcast_to(h0_ref[0, :, k * LANES:(k + 1) * LANES], (8, LANES))
        s_ref[...] = s0_ref[0]

    m = mod_ref[0]
    proj = jnp.dot(_norm_mod(x_ref[0], m, nw_ref[...]).astype(BF16), w_ref[...], preferred_element_type=F32)
    lru_g = proj[:, 0:lw]
    q_all = proj[:, lw:lw + RET_HEADS * dh]
    ret_g = proj[:, lw + RET_HEADS * dh:]

    xc = xc_ref[0]
    a, u = _lru_gates(xc, xc.astype(BF16), wg_ref, lp_ref, 0)
    _store_slabs(a_ref, a)
    _store_slabs(u_ref, u)
    _lru_scan(a_ref, u_ref, h_ref, (), hcar_ref, False, tc)
    for k in range(nslab):
        cols = slice(k * LANES, (k + 1) * LANES)
        y_ref[:, cols] = ((h_ref[k] + hbw_ref[0, k]) * _gelu_tanh(lru_g[:, cols])).astype(BF16)

    gnw = gnw_ref[...]
    for n in range(tc // RET_CHUNK):
        rows = slice(n * RET_CHUNK, (n + 1) * RET_CHUNK)
        for hd in range(RET_HEADS):
            cols = slice(hd * dh, (hd + 1) * dh)
            q = q_all[rows, cols]
            v_bf = vb_ref[0, rows, cols]
            scores = lax.dot_general(q.astype(BF16), kb_ref[0, rows, cols], (((1,), (1,)), ((), ())),
                                     preferred_element_type=F32)
            intra = _dot(scores * tab_ref[TAB_INTRA, hd], v_bf)
            q_both = jnp.concatenate([q * tab_ref[TAB_HEAD_FW, hd], q * tab_ref[TAB_HEAD_BW, hd]], axis=1)
            s_both = jnp.concatenate([s_ref[hd].astype(BF16), sbw_ref[0, n, hd]], axis=0)
            o = intra + _dot(q_both, s_both)
            dev = o - jnp.mean(o, axis=-1, keepdims=True)
            on = dev * lax.rsqrt(jnp.mean(dev * dev, axis=-1, keepdims=True) + EPS) * gnw[:, cols]
            y_ref[rows, lw + hd * dh:lw + (hd + 1) * dh] = (on * _silu(ret_g[rows, cols])).astype(BF16)
            _kv_update(s_ref, tab_ref, ktf_ref[0, rows, cols], v_bf, hd, TAB_STEP_FW)

    o_ref[0] = x_ref[0] + m[2:3] * _dot(y_ref[...], wout_ref[...])

    @pl.when(c == 0)
    def _():
        for k in range(nslab):
            finh_ref[0, :, k * LANES:(k + 1) * LANES] = h_ref[k, 0:1, :]

    @pl.when(c == nc - 1)
    def _():
        fins_ref[0] = s_ref[...]


def _forward_sweep(x, mod, mod_row, nw, w_fw, xc, hbw, sbw, kb, ktf, vb, tab, wg, lp, gnw, wout_bf, h0, s0, tc):
    b, t, d = x.shape
    lw = xc.shape[2]
    nc = t // tc
    nsub = tc // RET_CHUNK
    nslab = lw // LANES
    dh = RET_CHUNK
    tok_spec = pl.BlockSpec((1, tc, lw), lambda i, j: (i, j, 0))
    return pl.pallas_call(
        functools.partial(_fw_kernel, tc=tc, nc=nc),
        grid=(b, nc),
        in_specs=[
            pl.BlockSpec((1, tc, d), lambda i, j: (i, j, 0)),
            pl.BlockSpec((1, N_MOD, d), lambda i, j: (mod_row(i), 0, 0)),
            _const_spec(nw.shape), _resident(w_fw),
            tok_spec,
            pl.BlockSpec((1, nslab, tc, LANES), lambda i, j: (i, 0, j, 0)),
            pl.BlockSpec((1, nsub, RET_HEADS, dh, dh), lambda i, j: (i, j, 0, 0, 0)),
            tok_spec, tok_spec, tok_spec,
            _resident(tab),
            _resident(wg), _const_spec(lp.shape), _const_spec(gnw.shape), _resident(wout_bf),
            pl.BlockSpec((1, 1, lw), lambda i, j: (i, 0, 0)),
            pl.BlockSpec((1, RET_HEADS, dh, dh), lambda i, j: (i, 0, 0, 0)),
        ],
        out_specs=[
            pl.BlockSpec((1, tc, d), lambda i, j: (i, j, 0)),
            pl.BlockSpec((1, 1, lw), lambda i, j: (i, 0, 0)),
            pl.BlockSpec((1, RET_HEADS, dh, dh), lambda i, j: (i, 0, 0, 0)),
        ],
        out_shape=[
            jax.ShapeDtypeStruct((b, t, d), F32),
            jax.ShapeDtypeStruct((b, 1, lw), F32),
            jax.ShapeDtypeStruct((b, RET_HEADS, dh, dh), F32),
        ],
        scratch_shapes=[pltpu.VMEM((nslab, tc, LANES), F32),
                        pltpu.VMEM((nslab, tc, LANES), F32),
                        pltpu.VMEM((nslab, tc, LANES), F32),
                        pltpu.VMEM((nslab, 8, LANES), F32),
                        pltpu.VMEM((RET_HEADS, dh, dh), F32),
                        pltpu.VMEM((tc, wout_bf.shape[0]), BF16)],
        compiler_params=_params("arbitrary", "arbitrary"),
        name="forward_sweep",
    )(x, mod, nw, w_fw, xc, hbw, sbw, kb, ktf, vb, tab, wg, lp, gnw, wout_bf, h0, s0)


def _ffn_kernel(*refs, tf, halo, period, taps, final):
    if halo:
        xp_ref, x_ref, xn_ref = refs[:3]
        refs = refs[3:]
    else:
        x_ref = refs[0]
        refs = refs[1:]
    mod_ref, n2w_ref, wg_ref, wu_ref, wd_ref, cw_ref, cb_ref, fnw_ref, o_ref, h2_ref, act_ref = refs
    m = mod_ref[0]
    n2w = n2w_ref[...]

    def norm_mod(x):
        return (_rms(x) * n2w) * (1.0 + m[4:5]) + m[3:4]

    x = x_ref[0]
    h2_ref[halo:halo + tf, :] = norm_mod(x).astype(BF16)
    if halo:
        j = pl.program_id(1)
        nj = pl.num_programs(1)
        h2_ref[0:halo, :] = jnp.where(j == 0, 0.0, norm_mod(xp_ref[0])).astype(BF16)
        h2_ref[halo + tf:2 * halo + tf, :] = jnp.where(j == nj - 1, 0.0, norm_mod(xn_ref[0])).astype(BF16)
    rows = tf + 2 * halo
    pos = lax.broadcasted_iota(jnp.int32, (rows, FFN_TILE), 0) & (period - 1)
    has_left = pos != 0
    has_right = pos != period - 1

    for n in range(wg_ref.shape[1] // FFN_TILE):
        cs = slice(n * FFN_TILE, (n + 1) * FFN_TILE)
        g = jnp.dot(h2_ref[...], wg_ref[:, cs], preferred_element_type=F32)
        g_left = jnp.where(has_left, pltpu.roll(g, 1, 0), 0.0)
        g_right = jnp.where(has_right, pltpu.roll(g, rows - 1, 0), 0.0)
        cw = cw_ref[:, cs]
        conv = cb_ref[:, cs]
        for dr, wrow in taps:
            lo = halo + dr * period
            conv = (conv + g_left[lo:lo + tf] * cw[3 * wrow:3 * wrow + 1]
                    + g[lo:lo + tf] * cw[3 * wrow + 1:3 * wrow + 2]
                    + g_right[lo:lo + tf] * cw[3 * wrow + 2:3 * wrow + 3])
        up = jnp.dot(h2_ref[halo:halo + tf, :], wu_ref[:, cs], preferred_element_type=F32)
        act_ref[:, cs] = (_gelu_tanh(conv) * up).astype(BF16)

    out = x + m[5:6] * jnp.dot(act_ref[...], wd_ref[...], preferred_element_type=F32)
    o_ref[0] = _rms(out) * fnw_ref[...] if final else out


def _ffn(x, mod, mod_row, n2w, wg, wu, wd, cw, cb, fnw, latent, final):
    b, t, d = x.shape
    if latent:
        tf, halo, period = FFN_ROWS, GRID_W, GRID_W
        taps = ((-1, 0), (0, 1), (1, 2))
        nb = tf // halo
        x_specs = [pl.BlockSpec((1, halo, d), lambda i, j: (i, jnp.maximum(j * nb - 1, 0), 0)),
                   pl.BlockSpec((1, tf, d), lambda i, j: (i, j, 0)),
                   pl.BlockSpec((1, halo, d), lambda i, j: (i, jnp.minimum((j + 1) * nb, t // halo - 1), 0))]
        x_args = (x, x, x)
    else:
        tf, halo, period = t, 0, t
        taps = ((0, 1),)
        x_specs = [pl.BlockSpec((1, tf, d), lambda i, j: (i, j, 0))]
        x_args = (x,)
    assert period & (period - 1) == 0 and t % tf == 0
    resident = _resident
    return pl.pallas_call(
        functools.partial(_ffn_kernel, tf=tf, halo=halo, period=period, taps=taps, final=final),
        grid=(b, t // tf),
        in_specs=x_specs + [
            pl.BlockSpec((1, N_MOD, d), lambda i, j: (mod_row(i), 0, 0)),
            _const_spec(n2w.shape), resident(wg), resident(wu), resident(wd),
            _const_spec(cw.shape), _const_spec(cb.shape), _const_spec(fnw.shape)],
        out_specs=pl.BlockSpec((1, tf, d), lambda i, j: (i, j, 0)),
        out_shape=jax.ShapeDtypeStruct((b, t, d), F32),
        scratch_shapes=[pltpu.VMEM((tf + 2 * halo, d), BF16), pltpu.VMEM((tf, wd.shape[0]), BF16)],
        compiler_params=_params("arbitrary", "arbitrary"),
        name="ffn",
    )(*x_args, mod, n2w, wg, wu, wd, cw, cb, fnw)


def _block_diag_tiles(w):
    nb, c, _ = w.shape
    per = GATE_TILE // c
    eye = jnp.eye(per, dtype=w.dtype)
    w = w.reshape(nb // per, per, c, c)
    return jnp.einsum('mjab,jk->mjakb', w, eye).reshape(nb // per, GATE_TILE, GATE_TILE)


def _layer(x, mod, mod_row, p, tab, states, latent, final):
    b, t, d = x.shape
    tc = min(t, TIME_CHUNK)
    xc, hbw, sbw, kb, ktf, vb, fin_h_bw, fin_s_bw = _backward_sweep(
        x, mod, mod_row, p['norm1_w'], p['w_bw'], tab, p['lru_conv_w'], p['lru_conv_b'], p['wg'], p['lp'],
        states[1], states[3], tc)
    x, fin_h_fw, fin_s_fw = _forward_sweep(
        x, mod, mod_row, p['norm1_w'], p['w_fw'], xc, hbw, sbw, kb, ktf, vb, tab, p['wg'], p['lp'],
        p['ret_gn_w'], p['w_out'], states[0], states[2], tc)
    x = _ffn(x, mod, mod_row, p['norm2_w'], p['ffn_wg'], p['ffn_wu'], p['ffn_wd'], p['ffn_cw'], p['ffn_cb'],
             p['final_norm_w'], latent, final)
    return x, (fin_h_fw, fin_h_bw, fin_s_fw, fin_s_bw)


def kernel(x_prompt, x_sample, state_lru_fw, state_lru_bw, state_ret_fw, state_ret_bw, c, c_ctx,
           norm1_w, w_mod, b_mod, w_in, lru_conv_w, lru_conv_b,
           lru_wa_fw, lru_ba_fw, lru_wx_fw, lru_bx_fw, lru_lambda_fw,
           lru_wa_bw, lru_ba_bw, lru_wx_bw, lru_bx_bw, lru_lambda_bw,
           ret_decay_fw, ret_decay_bw, ret_gn_w, w_out, norm2_w,
           ffn_w_gate, ffn_w_up, ffn_conv_w, ffn_conv_b, ffn_w_down, final_norm_w):
    depth = w_in.shape[0]
    bp = x_prompt.shape[0]
    bs, d = c.shape
    lw = lru_conv_w.shape[2]
    dh = RET_CHUNK
    rw = RET_HEADS * dh

    ctx_row = bs
    rows = -(-(bs + 1) // 8) * 8
    cvec = jnp.zeros((rows, d), F32).at[:bs].set(c).at[ctx_row].set(c_ctx)
    row_ctx = lambda i: ctx_row
    row_lat = lambda i: i

    x_p, x_s = x_prompt, x_sample
    fins = []
    for l in range(depth):
        mod = _modulation(cvec, w_mod[l], b_mod[l][None]).reshape(rows, N_MOD, d)
        tab = _decay_tables(jnp.stack([ret_decay_fw[l], ret_decay_bw[l]]).astype(F32))
        p = dict(
            norm1_w=norm1_w[l][None],
            w_bw=jnp.concatenate([w_in[l][:, 0:lw], w_in[l][:, 2 * lw + rw:2 * lw + 3 * rw]], axis=1).astype(BF16),
            w_fw=jnp.concatenate([w_in[l][:, lw:2 * lw + rw], w_in[l][:, 2 * lw + 3 * rw:]], axis=1).astype(BF16),
            lru_conv_w=lru_conv_w[l], lru_conv_b=lru_conv_b[l][None],
            wg=jnp.stack([jnp.stack([_block_diag_tiles(lru_wa_fw[l]), _block_diag_tiles(lru_wx_fw[l])]),
                          jnp.stack([_block_diag_tiles(lru_wa_bw[l]), _block_diag_tiles(lru_wx_bw[l])])]
                         ).astype(BF16),
            lp=jnp.stack([jnp.stack([lru_ba_fw[l], lru_bx_fw[l], lru_lambda_fw[l]]),
                          jnp.stack([lru_ba_bw[l], lru_bx_bw[l], lru_lambda_bw[l]])]).astype(F32),
            ret_gn_w=ret_gn_w[l][None], w_out=w_out[l].astype(BF16), norm2_w=norm2_w[l][None],
            ffn_wg=ffn_w_gate[l].astype(BF16), ffn_wu=ffn_w_up[l].astype(BF16), ffn_wd=ffn_w_down[l].astype(BF16),
            ffn_cw=ffn_conv_w[l].reshape(9, -1), ffn_cb=ffn_conv_b[l][None],
            final_norm_w=final_norm_w[None],
        )
        final = l == depth - 1
        zeros_h = jnp.zeros((bp, 1, lw), F32)
        zeros_s = jnp.zeros((bp, RET_HEADS, dh, dh), F32)
        x_p, fin = _layer(x_p, mod, row_ctx, p, tab, (zeros_h, zeros_h, zeros_s, zeros_s), False, final)
        fins.append(fin)
        init_lat = (state_lru_fw[:, l][:, None].astype(F32), state_lru_bw[:, l][:, None].astype(F32),
                    state_ret_fw[:, l].astype(F32), state_ret_bw[:, l].astype(F32))
        x_s, _ = _layer(x_s, mod, row_lat, p, tab, init_lat, True, final)

    dt = x_prompt.dtype
    new_lru_fw = jnp.concatenate([f[0] for f in fins], axis=1).astype(dt)
    new_lru_bw = jnp.concatenate([f[1] for f in fins], axis=1).astype(dt)
    new_ret_fw = jnp.stack([f[2] for f in fins], axis=1).astype(dt)
    new_ret_bw = jnp.stack([f[3] for f in fins], axis=1).astype(dt)
    return (x_p, x_s, new_lru_fw, new_lru_bw, new_ret_fw, new_ret_bw)
```

```python
import functools
import math

import jax
import jax.numpy as jnp
from jax import lax
from jax.experimental import pallas as pl
from jax.experimental.pallas import tpu as pltpu

F32 = jnp.float32
BF16 = jnp.bfloat16

N_MOD = 6
EPS = 1e-6
LRU_C = 8.0
LRU_BLOCKS = 8
LRU_CONV_WIDTH = 4
LRU_CONV_PAD_LEFT = 2
RET_HEADS = 4
RET_CHUNK = 128
GRID_W = 64
GATE_TILE = 256
FFN_TILE = 256
HALO_ROWS = 8
LEAD_ROWS = 16
LANES = 128
SEG_ROWS = 4
TIME_CHUNK = 512
FFN_ROWS = 512
VMEM_LIMIT = 56 * 1024 * 1024

def _dot(a, b):
    return jnp.dot(a.astype(BF16), b.astype(BF16), preferred_element_type=F32)


def _sigmoid(x):
    return 1.0 / (1.0 + jnp.exp(-x))


def _silu(x):
    return x * _sigmoid(x)


def _gelu_tanh(x):
    c = math.sqrt(2.0 / math.pi)
    return (0.5 * x) * (1.0 + jnp.tanh(x * (c + (c * 0.044715) * (x * x))))


def _softplus(x):
    return jnp.maximum(x, 0.0) + jnp.log1p(jnp.exp(-jnp.abs(x)))


def _rms(x):
    return x * lax.rsqrt(jnp.mean(x * x, axis=-1, keepdims=True) + EPS)


def _params(*sem):
    return pltpu.CompilerParams(dimension_semantics=sem, vmem_limit_bytes=VMEM_LIMIT)


def _const_spec(shape):
    nd = len(shape)
    return pl.BlockSpec(shape, lambda *_: (0,) * nd)


def _resident(a):
    return pl.BlockSpec(a.shape, lambda *_: (0,) * a.ndim, pipeline_mode=pl.Buffered(1))


def _mod_kernel(c_ref, w_ref, b_ref, o_ref):
    o_ref[...] = _dot(_silu(c_ref[...]), w_ref[...]) + b_ref[...]


def _modulation(cvec, w, b):
    rows, d = cvec.shape
    n = w.shape[1]
    tn = n // 4
    return pl.pallas_call(
        _mod_kernel,
        grid=(n // tn,),
        in_specs=[pl.BlockSpec((rows, d), lambda j: (0, 0)),
                  pl.BlockSpec((d, tn), lambda j: (0, j)),
                  pl.BlockSpec((1, tn), lambda j: (0, j))],
        out_specs=pl.BlockSpec((rows, tn), lambda j: (0, j)),
        out_shape=jax.ShapeDtypeStruct((rows, n), F32),
        compiler_params=_params("arbitrary"),
        name="modulation",
    )(cvec, w, b)


TAB_INTRA, TAB_HEAD_FW, TAB_HEAD_BW, TAB_TAIL_FW, TAB_TAIL_BW, TAB_STEP_FW, TAB_STEP_BW = range(7)


def _tab_kernel(dec_ref, tab_ref):
    c = RET_CHUNK
    row = lax.broadcasted_iota(jnp.int32, (c, c), 0).astype(F32)
    col = lax.broadcasted_iota(jnp.int32, (c, c), 1).astype(F32)
    rel = row - col
    for h in range(RET_HEADS):
        lf = -_softplus(-jnp.full((c, c), dec_ref[0, h], F32))
        lb = -_softplus(-jnp.full((c, c), dec_ref[1, h], F32))
        tab_ref[TAB_INTRA, h] = jnp.where(rel > 0, jnp.exp(lf * rel),
                                          jnp.where(rel < 0, jnp.exp(lb * (-rel)), 2.0))
        tab_ref[TAB_HEAD_FW, h] = jnp.exp(lf * (row + 1.0))
        tab_ref[TAB_HEAD_BW, h] = jnp.exp(lb * (c - row))
        tab_ref[TAB_TAIL_FW, h] = jnp.exp(lf * (c - 1.0 - row))
        tab_ref[TAB_TAIL_BW, h] = jnp.exp(lb * row)
        tab_ref[TAB_STEP_FW, h] = jnp.exp(lf * c)
        tab_ref[TAB_STEP_BW, h] = jnp.exp(lb * c)


def _decay_tables(dec):
    return pl.pallas_call(
        _tab_kernel,
        in_specs=[pl.BlockSpec(memory_space=pltpu.SMEM)],
        out_shape=jax.ShapeDtypeStruct((7, RET_HEADS, RET_CHUNK, RET_CHUNK), F32),
        name="decay_tables",
    )(dec)


def _norm_mod(x, m, nw):
    return _rms(x) * (nw * (1.0 + m[1:2])) + m[0:1]


def _lru_conv(prev, cur, nxt, cw_ref, cb_ref):
    tc = cur.shape[0]
    lead = prev.shape[0]
    ext = jnp.concatenate([prev, cur, nxt], axis=0)
    rows = ext.shape[0]
    cw = cw_ref[...]
    acc = cb_ref[...] + cur * cw[LRU_CONV_PAD_LEFT:LRU_CONV_PAD_LEFT + 1]
    for j in range(LRU_CONV_WIDTH):
        off = j - LRU_CONV_PAD_LEFT
        if off:
            acc = acc + pltpu.roll(ext, (-off) % rows, 0)[lead:lead + tc] * cw[j:j + 1]
    return acc


def _lru_gates(xc, xc_bf, wg_ref, lp_ref, direction):
    lp = lp_ref[direction]
    width = xc.shape[1]

    def gate(g):
        parts = [jnp.dot(xc_bf[:, GATE_TILE * m:GATE_TILE * (m + 1)], wg_ref[direction, g, m],
                         preferred_element_type=F32) for m in range(width // GATE_TILE)]
        return jnp.concatenate(parts, axis=1)

    r = _sigmoid(gate(0) + lp[0:1])
    i = _sigmoid(gate(1) + lp[1:2])
    log_a = (-LRU_C * _softplus(-lp[2:3])) * r
    a = jnp.exp(log_a)
    u = jnp.sqrt(-jnp.tanh(log_a) * (a * a + 1.0)) * (i * xc)
    return a, u


def _store_slabs(ref, val):
    for k in range(ref.shape[0]):
        ref[k] = val[:, k * LANES:(k + 1) * LANES]


def _lru_scan(a_ref, u_ref, h_ref, h_pre, carry_ref, reverse, tc):
    nslab = a_ref.shape[0]
    blk = 8 * SEG_ROWS
    nblk = tc // blk
    row = lax.broadcasted_iota(jnp.int32, (8, LANES), 0)
    order = tuple(range(SEG_ROWS - 1, -1, -1)) if reverse else tuple(range(SEG_ROWS))
    edge = 0 if reverse else 7

    def earlier(x, d, fill):
        if reverse:
            return jnp.where(row < 8 - d, pltpu.roll(x, 8 - d, 0), fill)
        return jnp.where(row >= d, pltpu.roll(x, d, 0), fill)

    carry = [carry_ref[k] for k in range(nslab)]
    for bi in range(nblk):
        base = ((nblk - 1 - bi) if reverse else bi) * blk
        for k in range(nslab):
            idx = [pl.ds(base + j, 8, stride=SEG_ROWS) for j in range(SEG_ROWS)]
            av = [a_ref[k, i, :] for i in idx]
            uv = [u_ref[k, i, :] for i in idx]
            h, p = uv[order[0]], av[order[0]]
            for j in order[1:]:
                h = av[j] * h + uv[j]
                p = av[j] * p
            d = 1
            while d < 8:
                h = p * earlier(h, d, 0.0) + h
                p = p * earlier(p, d, 1.0)
                d *= 2
            c_in = carry[k]
            state = earlier(h, 1, 0.0) + earlier(p, 1, 1.0) * c_in
            carry[k] = jnp.broadcast_to(h[edge:edge + 1] + p[edge:edge + 1] * c_in[edge:edge + 1], (8, LANES))
            for j in order:
                state = av[j] * state + uv[j]
                h_ref[(*h_pre, k, idx[j], slice(None))] = state
    for k in range(nslab):
        carry_ref[k] = carry[k]


def _kv_update(s_ref, tab_ref, k_tail_bf, v_bf, head, step_tab):
    kv = lax.dot_general(k_tail_bf, v_bf, (((0,), (0,)), ((), ())), preferred_element_type=F32)
    s_ref[head] = tab_ref[step_tab, head] * s_ref[head] + kv


def _bw_kernel(xprev_ref, x_ref, mod_ref, nw_ref, w_ref, tab_ref, cw_ref, cb_ref, wg_ref, lp_ref, h0_ref, s0_ref,
               xc_ref, hbw_ref, sbw_ref, kb_ref, ktf_ref, vb_ref, hin_ref, finh_ref, fins_ref,
               a_ref, u_ref, hcar_ref, s_ref, nx_ref, *, tc, nc):
    c = pl.program_id(1)
    chunk = nc - 1 - c
    dh = RET_CHUNK
    nslab = a_ref.shape[0]
    lw = nslab * LANES

    @pl.when(c == 0)
    def _():
        for k in range(nslab):
            hcar_ref[k] = jnp.broadcast_to(h0_ref[0, :, k * LANES:(k + 1) * LANES], (8, LANES))
        s_ref[...] = s0_ref[0]
        nx_ref[...] = jnp.zeros_like(nx_ref)

    m = mod_ref[0]
    nw = nw_ref[...]
    h_cur = _norm_mod(x_ref[0], m, nw).astype(BF16)
    hin_ref[0] = h_cur
    h_ext = jnp.concatenate([_norm_mod(xprev_ref[0], m, nw).astype(BF16), h_cur], axis=0)
    lx_ext = jnp.dot(h_ext, w_ref[:, 0:lw], preferred_element_type=F32)
    kv = jnp.dot(h_cur, w_ref[:, lw:], preferred_element_type=F32)
    lead = xprev_ref.shape[1]
    lx = lx_ext[lead:]
    lx_prev = jnp.where(chunk == 0, 0.0, lx_ext[0:lead])
    xc = _lru_conv(lx_prev, lx, nx_ref[...], cw_ref, cb_ref)
    nx_ref[...] = lx[0:nx_ref.shape[0]]
    xc_ref[0] = xc

    a, u = _lru_gates(xc, xc.astype(BF16), wg_ref, lp_ref, 1)
    _store_slabs(a_ref, a)
    _store_slabs(u_ref, u)
    _lru_scan(a_ref, u_ref, hbw_ref, (0,), hcar_ref, True, tc)

    k = kv[:, 0:RET_HEADS * dh] * dh ** -0.5
    v_bf = kv[:, RET_HEADS * dh:].astype(BF16)
    kb_ref[0] = k.astype(BF16)
    vb_ref[0] = v_bf
    for n in reversed(range(tc // RET_CHUNK)):
        rows = slice(n * RET_CHUNK, (n + 1) * RET_CHUNK)
        for hd in range(RET_HEADS):
            cols = slice(hd * dh, (hd + 1) * dh)
            sbw_ref[0, n, hd] = s_ref[hd].astype(BF16)
            ktf_ref[0, rows, cols] = (k[rows, cols] * tab_ref[TAB_TAIL_FW, hd]).astype(BF16)
            _kv_update(s_ref, tab_ref, (k[rows, cols] * tab_ref[TAB_TAIL_BW, hd]).astype(BF16), v_bf[rows, cols],
                       hd, TAB_STEP_BW)

    @pl.when(c == 0)
    def _():
        for k in range(nslab):
            finh_ref[0, :, k * LANES:(k + 1) * LANES] = hbw_ref[0, k, tc - 1:tc, :]

    @pl.when(c == nc - 1)
    def _():
        fins_ref[0] = s_ref[...]


def _backward_sweep(x, mod, mod_row, nw, w_bw, tab, cw, cb, wg, lp, h0, s0, tc):
    b, t, d = x.shape
    lw = cw.shape[1]
    nc = t // tc
    nlead = tc // LEAD_ROWS
    nsub = tc // RET_CHUNK
    nslab = lw // LANES
    dh = RET_CHUNK
    ch = lambda j: nc - 1 - j
    tok = lambda dt: jax.ShapeDtypeStruct((b, t, lw), dt)
    tok_spec = pl.BlockSpec((1, tc, lw), lambda i, j: (i, ch(j), 0))
    return pl.pallas_call(
        functools.partial(_bw_kernel, tc=tc, nc=nc),
        grid=(b, nc),
        in_specs=[
            pl.BlockSpec((1, LEAD_ROWS, d), lambda i, j: (i, jnp.maximum(ch(j) * nlead - 1, 0), 0)),
            pl.BlockSpec((1, tc, d), lambda i, j: (i, ch(j), 0)),
            pl.BlockSpec((1, N_MOD, d), lambda i, j: (mod_row(i), 0, 0)),
            _const_spec(nw.shape), _resident(w_bw), _resident(tab), _const_spec(cw.shape), _const_spec(cb.shape),
            _resident(wg), _const_spec(lp.shape),
            pl.BlockSpec((1, 1, lw), lambda i, j: (i, 0, 0)),
            pl.BlockSpec((1, RET_HEADS, dh, dh), lambda i, j: (i, 0, 0, 0)),
        ],
        out_specs=[
            tok_spec,
            pl.BlockSpec((1, nslab, tc, LANES), lambda i, j: (i, 0, ch(j), 0)),
            pl.BlockSpec((1, nsub, RET_HEADS, dh, dh), lambda i, j: (i, ch(j), 0, 0, 0)),
            tok_spec, tok_spec, tok_spec,
            pl.BlockSpec((1, tc, d), lambda i, j: (i, ch(j), 0)),
            pl.BlockSpec((1, 1, lw), lambda i, j: (i, 0, 0)),
            pl.BlockSpec((1, RET_HEADS, dh, dh), lambda i, j: (i, 0, 0, 0)),
        ],
        out_shape=[
            tok(F32),
            jax.ShapeDtypeStruct((b, nslab, t, LANES), F32),
            jax.ShapeDtypeStruct((b, t // RET_CHUNK, RET_HEADS, dh, dh), BF16),
            tok(BF16), tok(BF16), tok(BF16),
            jax.ShapeDtypeStruct((b, t, d), BF16),
            jax.ShapeDtypeStruct((b, 1, lw), F32),
            jax.ShapeDtypeStruct((b, RET_HEADS, dh, dh), F32),
        ],
        scratch_shapes=[pltpu.VMEM((nslab, tc, LANES), F32),
                        pltpu.VMEM((nslab, tc, LANES), F32),
                        pltpu.VMEM((nslab, 8, LANES), F32),
                        pltpu.VMEM((RET_HEADS, dh, dh), F32),
                        pltpu.VMEM((HALO_ROWS, lw), F32)],
        compiler_params=_params("arbitrary", "arbitrary"),
        name="backward_sweep",
    )(x, x, mod, nw, w_bw, tab, cw, cb, wg, lp, h0, s0)


def _fw_kernel(x_ref, mod_ref, hin_ref, w_ref, xc_ref, hbw_ref, sbw_ref, kb_ref, ktf_ref, vb_ref,
               tab_ref, wg_ref, lp_ref, gnw_ref, wout_ref, h0_ref, s0_ref,
               o_ref, finh_ref, fins_ref, a_ref, u_ref, h_ref, hcar_ref, s_ref, y_ref, *, tc, nc):
    c = pl.program_id(1)
    dh = RET_CHUNK
    lw = xc_ref.shape[2]
    nslab = a_ref.shape[0]

    @pl.when(c == 0)
    def _():
        for k in range(nslab):
            hcar_ref[k] = jnp.broadcast_to(h0_ref[0, :, k * LANES:(k + 1) * LANES], (8, LANES))
        s_ref[...] = s0_ref[0]

    m = mod_ref[0]
    h_in = hin_ref[0]

    xc = xc_ref[0]
    a, u = _lru_gates(xc, xc.astype(BF16), wg_ref, lp_ref, 0)
    _store_slabs(a_ref, a)
    _store_slabs(u_ref, u)
    _lru_scan(a_ref, u_ref, h_ref, (), hcar_ref, False, tc)

    lru_g = jnp.dot(h_in, w_ref[:, 0:lw], preferred_element_type=F32)
    qg = jnp.dot(h_in, w_ref[:, lw:], preferred_element_type=F32)
    q_all = qg[:, 0:RET_HEADS * dh]
    ret_g = qg[:, RET_HEADS * dh:]
    for k in range(nslab):
        cols = slice(k * LANES, (k + 1) * LANES)
        y_ref[:, cols] = ((h_ref[k] + hbw_ref[0, k]) * _gelu_tanh(lru_g[:, cols])).astype(BF16)

    gnw = gnw_ref[...]
    for n in range(tc // RET_CHUNK):
        rows = slice(n * RET_CHUNK, (n + 1) * RET_CHUNK)
        for hd in range(RET_HEADS):
            cols = slice(hd * dh, (hd + 1) * dh)
            q = q_all[rows, cols]
            v_bf = vb_ref[0, rows, cols]
            scores = lax.dot_general(q.astype(BF16), kb_ref[0, rows, cols], (((1,), (1,)), ((), ())),
                                     preferred_element_type=F32)
            intra = _dot(scores * tab_ref[TAB_INTRA, hd], v_bf)
            q_both = jnp.concatenate([q * tab_ref[TAB_HEAD_FW, hd], q * tab_ref[TAB_HEAD_BW, hd]], axis=1)
            s_both = jnp.concatenate([s_ref[hd].astype(BF16), sbw_ref[0, n, hd]], axis=0)
            o = intra + _dot(q_both, s_both)
            dev = o - jnp.mean(o, axis=-1, keepdims=True)
            on = dev * lax.rsqrt(jnp.mean(dev * dev, axis=-1, keepdims=True) + EPS) * gnw[:, cols]
            y_ref[rows, lw + hd * dh:lw + (hd + 1) * dh] = (on * _silu(ret_g[rows, cols])).astype(BF16)
            _kv_update(s_ref, tab_ref, ktf_ref[0, rows, cols], v_bf, hd, TAB_STEP_FW)

    o_ref[0] = x_ref[0] + m[2:3] * _dot(y_ref[...], wout_ref[...])

    @pl.when(c == 0)
    def _():
        for k in range(nslab):
            finh_ref[0, :, k * LANES:(k + 1) * LANES] = h_ref[k, 0:1, :]

    @pl.when(c == nc - 1)
    def _():
        fins_ref[0] = s_ref[...]


def _forward_sweep(x, mod, mod_row, hin, w_fw, xc, hbw, sbw, kb, ktf, vb, tab, wg, lp, gnw, wout_bf, h0, s0, tc):
    b, t, d = x.shape
    lw = xc.shape[2]
    nc = t // tc
    nsub = tc // RET_CHUNK
    nslab = lw // LANES
    dh = RET_CHUNK
    tok_spec = pl.BlockSpec((1, tc, lw), lambda i, j: (i, j, 0))
    return pl.pallas_call(
        functools.partial(_fw_kernel, tc=tc, nc=nc),
        grid=(b, nc),
        in_specs=[
            pl.BlockSpec((1, tc, d), lambda i, j: (i, j, 0)),
            pl.BlockSpec((1, N_MOD, d), lambda i, j: (mod_row(i), 0, 0)),
            pl.BlockSpec((1, tc, d), lambda i, j: (i, j, 0)), _resident(w_fw),
            tok_spec,
            pl.BlockSpec((1, nslab, tc, LANES), lambda i, j: (i, 0, j, 0)),
            pl.BlockSpec((1, nsub, RET_HEADS, dh, dh), lambda i, j: (i, j, 0, 0, 0)),
            tok_spec, tok_spec, tok_spec,
            _resident(tab),
            _resident(wg), _const_spec(lp.shape), _const_spec(gnw.shape), _resident(wout_bf),
            pl.BlockSpec((1, 1, lw), lambda i, j: (i, 0, 0)),
            pl.BlockSpec((1, RET_HEADS, dh, dh), lambda i, j: (i, 0, 0, 0)),
        ],
        out_specs=[
            pl.BlockSpec((1, tc, d), lambda i, j: (i, j, 0)),
            pl.BlockSpec((1, 1, lw), lambda i, j: (i, 0, 0)),
            pl.BlockSpec((1, RET_HEADS, dh, dh), lambda i, j: (i, 0, 0, 0)),
        ],
        out_shape=[
            jax.ShapeDtypeStruct((b, t, d), F32),
            jax.ShapeDtypeStruct((b, 1, lw), F32),
            jax.ShapeDtypeStruct((b, RET_HEADS, dh, dh), F32),
        ],
        scratch_shapes=[pltpu.VMEM((nslab, tc, LANES), F32),
                        pltpu.VMEM((nslab, tc, LANES), F32),
                        pltpu.VMEM((nslab, tc, LANES), F32),
                        pltpu.VMEM((nslab, 8, LANES), F32),
                        pltpu.VMEM((RET_HEADS, dh, dh), F32),
                        pltpu.VMEM((tc, wout_bf.shape[0]), BF16)],
        compiler_params=_params("arbitrary", "arbitrary"),
        name="forward_sweep",
    )(x, mod, hin, w_fw, xc, hbw, sbw, kb, ktf, vb, tab, wg, lp, gnw, wout_bf, h0, s0)


def _ffn_kernel(*refs, tf, halo, period, taps, final):
    if halo:
        xp_ref, x_ref, xn_ref = refs[:3]
        refs = refs[3:]
    else:
        x_ref = refs[0]
        refs = refs[1:]
    mod_ref, n2w_ref, wg_ref, wu_ref, wd_ref, cw_ref, cb_ref, fnw_ref, o_ref, h2_ref, act_ref = refs
    m = mod_ref[0]
    n2wm = n2w_ref[...] * (1.0 + m[4:5])

    def norm_mod(x):
        return _rms(x) * n2wm + m[3:4]

    x = x_ref[0]
    h2_ref[halo:halo + tf, :] = norm_mod(x).astype(BF16)
    if halo:
        j = pl.program_id(1)
        nj = pl.num_programs(1)
        h2_ref[0:halo, :] = jnp.where(j == 0, 0.0, norm_mod(xp_ref[0])).astype(BF16)
        h2_ref[halo + tf:2 * halo + tf, :] = jnp.where(j == nj - 1, 0.0, norm_mod(xn_ref[0])).astype(BF16)
    rows = tf + 2 * halo
    nper = rows // period
    sub = lax.broadcasted_iota(jnp.int32, (nper, 8, FFN_TILE), 1)

    def neighbour(g, step):
        r3 = pltpu.roll(g, (-step) % rows, 0).reshape(nper, period, FFN_TILE)
        if step < 0:
            fixed = [jnp.where(sub == 0, 0.0, r3[:, 0:8]), r3[:, 8:]]
        else:
            fixed = [r3[:, 0:period - 8], jnp.where(sub == 7, 0.0, r3[:, period - 8:])]
        return jnp.concatenate(fixed, axis=1).reshape(rows, FFN_TILE)

    for n in range(wg_ref.shape[1] // FFN_TILE):
        cs = slice(n * FFN_TILE, (n + 1) * FFN_TILE)
        g = jnp.dot(h2_ref[...], wg_ref[:, cs], preferred_element_type=F32)
        g_left = neighbour(g, -1)
        g_right = neighbour(g, 1)
        cw = cw_ref[:, cs]
        conv = cb_ref[:, cs]
        for dr, wrow in taps:
            lo = halo + dr * period
            conv = (conv + g_left[lo:lo + tf] * cw[3 * wrow:3 * wrow + 1]
                    + g[lo:lo + tf] * cw[3 * wrow + 1:3 * wrow + 2]
                    + g_right[lo:lo + tf] * cw[3 * wrow + 2:3 * wrow + 3])
        up = jnp.dot(h2_ref[halo:halo + tf, :], wu_ref[:, cs], preferred_element_type=F32)
        act_ref[:, cs] = (_gelu_tanh(conv) * up).astype(BF16)

    out = x + m[5:6] * jnp.dot(act_ref[...], wd_ref[...], preferred_element_type=F32)
    o_ref[0] = _rms(out) * fnw_ref[...] if final else out


def _ffn(x, mod, mod_row, n2w, wg, wu, wd, cw, cb, fnw, latent, final):
    b, t, d = x.shape
    if latent:
        tf, halo, period = FFN_ROWS, GRID_W, GRID_W
        taps = ((-1, 0), (0, 1), (1, 2))
        nb = tf // halo
        x_specs = [pl.BlockSpec((1, halo, d), lambda i, j: (i, jnp.maximum(j * nb - 1, 0), 0)),
                   pl.BlockSpec((1, tf, d), lambda i, j: (i, j, 0)),
                   pl.BlockSpec((1, halo, d), lambda i, j: (i, jnp.minimum((j + 1) * nb, t // halo - 1), 0))]
        x_args = (x, x, x)
    else:
        tf, halo, period = t, 0, t
        taps = ((0, 1),)
        x_specs = [pl.BlockSpec((1, tf, d), lambda i, j: (i, j, 0))]
        x_args = (x,)
    assert period & (period - 1) == 0 and t % tf == 0
    return pl.pallas_call(
        functools.partial(_ffn_kernel, tf=tf, halo=halo, period=period, taps=taps, final=final),
        grid=(b, t // tf),
        in_specs=x_specs + [
            pl.BlockSpec((1, N_MOD, d), lambda i, j: (mod_row(i), 0, 0)),
            _const_spec(n2w.shape), _resident(wg), _resident(wu), _resident(wd),
            _const_spec(cw.shape), _const_spec(cb.shape), _const_spec(fnw.shape)],
        out_specs=pl.BlockSpec((1, tf, d), lambda i, j: (i, j, 0)),
        out_shape=jax.ShapeDtypeStruct((b, t, d), F32),
        scratch_shapes=[pltpu.VMEM((tf + 2 * halo, d), BF16), pltpu.VMEM((tf, wd.shape[0]), BF16)],
        compiler_params=_params("arbitrary", "arbitrary"),
        name="ffn",
    )(*x_args, mod, n2w, wg, wu, wd, cw, cb, fnw)


def _block_diag_tiles(w):
    nb, c, _ = w.shape
    per = GATE_TILE // c
    eye = jnp.eye(per, dtype=w.dtype)
    w = w.reshape(nb // per, per, c, c)
    return jnp.einsum('mjab,jk->mjakb', w, eye).reshape(nb // per, GATE_TILE, GATE_TILE)


def _layer(x, mod, mod_row, p, tab, states, latent, final):
    b, t, d = x.shape
    tc = min(t, TIME_CHUNK)
    xc, hbw, sbw, kb, ktf, vb, hin, fin_h_bw, fin_s_bw = _backward_sweep(
        x, mod, mod_row, p['norm1_w'], p['w_bw'], tab, p['lru_conv_w'], p['lru_conv_b'], p['wg'], p['lp'],
        states[1], states[3], tc)
    x, fin_h_fw, fin_s_fw = _forward_sweep(
        x, mod, mod_row, hin, p['w_fw'], xc, hbw, sbw, kb, ktf, vb, tab, p['wg'], p['lp'],
        p['ret_gn_w'], p['w_out'], states[0], states[2], tc)
    x = _ffn(x, mod, mod_row, p['norm2_w'], p['ffn_wg'], p['ffn_wu'], p['ffn_wd'], p['ffn_cw'], p['ffn_cb'],
             p['final_norm_w'], latent, final)
    return x, (fin_h_fw, fin_h_bw, fin_s_fw, fin_s_bw)


def kernel(x_prompt, x_sample, state_lru_fw, state_lru_bw, state_ret_fw, state_ret_bw, c, c_ctx,
           norm1_w, w_mod, b_mod, w_in, lru_conv_w, lru_conv_b,
           lru_wa_fw, lru_ba_fw, lru_wx_fw, lru_bx_fw, lru_lambda_fw,
           lru_wa_bw, lru_ba_bw, lru_wx_bw, lru_bx_bw, lru_lambda_bw,
           ret_decay_fw, ret_decay_bw, ret_gn_w, w_out, norm2_w,
           ffn_w_gate, ffn_w_up, ffn_conv_w, ffn_conv_b, ffn_w_down, final_norm_w):
    depth = w_in.shape[0]
    bp = x_prompt.shape[0]
    bs, d = c.shape
    lw = lru_conv_w.shape[2]
    dh = RET_CHUNK
    rw = RET_HEADS * dh

    ctx_row = bs
    rows = -(-(bs + 1) // 8) * 8
    cvec = jnp.zeros((rows, d), F32).at[:bs].set(c).at[ctx_row].set(c_ctx)
    row_ctx = lambda i: ctx_row
    row_lat = lambda i: i

    x_p, x_s = x_prompt, x_sample
    fins = []
    for l in range(depth):
        mod = _modulation(cvec, w_mod[l], b_mod[l][None]).reshape(rows, N_MOD, d)
        tab = _decay_tables(jnp.stack([ret_decay_fw[l], ret_decay_bw[l]]).astype(F32))
        p = dict(
            norm1_w=norm1_w[l][None],
            w_bw=jnp.concatenate([w_in[l][:, 0:lw], w_in[l][:, 2 * lw + rw:2 * lw + 3 * rw]], axis=1).astype(BF16),
            w_fw=jnp.concatenate([w_in[l][:, lw:2 * lw + rw], w_in[l][:, 2 * lw + 3 * rw:]], axis=1).astype(BF16),
            lru_conv_w=lru_conv_w[l], lru_conv_b=lru_conv_b[l][None],
            wg=jnp.stack([jnp.stack([_block_diag_tiles(lru_wa_fw[l]), _block_diag_tiles(lru_wx_fw[l])]),
                          jnp.stack([_block_diag_tiles(lru_wa_bw[l]), _block_diag_tiles(lru_wx_bw[l])])]
                         ).astype(BF16),
            lp=jnp.stack([jnp.stack([lru_ba_fw[l], lru_bx_fw[l], lru_lambda_fw[l]]),
                          jnp.stack([lru_ba_bw[l], lru_bx_bw[l], lru_lambda_bw[l]])]).astype(F32),
            ret_gn_w=ret_gn_w[l][None], w_out=w_out[l].astype(BF16), norm2_w=norm2_w[l][None],
            ffn_wg=ffn_w_gate[l].astype(BF16), ffn_wu=ffn_w_up[l].astype(BF16), ffn_wd=ffn_w_down[l].astype(BF16),
            ffn_cw=ffn_conv_w[l].reshape(9, -1), ffn_cb=ffn_conv_b[l][None],
            final_norm_w=final_norm_w[None],
        )
        final = l == depth - 1
        zeros_h = jnp.zeros((bp, 1, lw), F32)
        zeros_s = jnp.zeros((bp, RET_HEADS, dh, dh), F32)
        x_p, fin = _layer(x_p, mod, row_ctx, p, tab, (zeros_h, zeros_h, zeros_s, zeros_s), False, final)
        fins.append(fin)
        init_lat = (state_lru_fw[:, l][:, None].astype(F32), state_lru_bw[:, l][:, None].astype(F32),
                    state_ret_fw[:, l].astype(F32), state_ret_bw[:, l].astype(F32))
        x_s, _ = _layer(x_s, mod, row_lat, p, tab, init_lat, True, final)

    dt = x_prompt.dtype
    new_lru_fw = jnp.concatenate([f[0] for f in fins], axis=1).astype(dt)
    new_lru_bw = jnp.concatenate([f[1] for f in fins], axis=1).astype(dt)
    new_ret_fw = jnp.stack([f[2] for f in fins], axis=1).astype(dt)
    new_ret_bw = jnp.stack([f[3] for f in fins], axis=1).astype(dt)
    return (x_p, x_s, new_lru_fw, new_lru_bw, new_ret_fw, new_ret_bw)
```

```python
import functools
import math

import jax
import jax.numpy as jnp
from jax import lax
from jax.experimental import pallas as pl
from jax.experimental.pallas import tpu as pltpu

F32 = jnp.float32
BF16 = jnp.bfloat16

N_MOD = 6
EPS = 1e-6
LRU_C = 8.0
LRU_BLOCKS = 8
LRU_CONV_WIDTH = 4
LRU_CONV_PAD_LEFT = 2
RET_HEADS = 4
RET_CHUNK = 128
GRID_W = 64
GATE_TILE = 256
FFN_TILE = 256
HALO_ROWS = 8
LEAD_ROWS = 16
LANES = 128
SEG_ROWS = 4
TIME_CHUNK = 512
FFN_ROWS = 512
VMEM_LIMIT = 56 * 1024 * 1024

def _dot(a, b):
    return jnp.dot(a.astype(BF16), b.astype(BF16), preferred_element_type=F32)


def _sigmoid(x):
    return 1.0 / (1.0 + jnp.exp(-x))


def _silu(x):
    return x * _sigmoid(x)


def _gelu_tanh(x):
    c = math.sqrt(2.0 / math.pi)
    return (0.5 * x) * (1.0 + jnp.tanh(x * (c + (c * 0.044715) * (x * x))))


def _softplus(x):
    return jnp.maximum(x, 0.0) + jnp.log1p(jnp.exp(-jnp.abs(x)))


def _rms(x):
    return x * lax.rsqrt(jnp.mean(x * x, axis=-1, keepdims=True) + EPS)


def _params(*sem):
    return pltpu.CompilerParams(dimension_semantics=sem, vmem_limit_bytes=VMEM_LIMIT)


def _const_spec(shape):
    nd = len(shape)
    return pl.BlockSpec(shape, lambda *_: (0,) * nd)


def _resident(a):
    return pl.BlockSpec(a.shape, lambda *_: (0,) * a.ndim, pipeline_mode=pl.Buffered(1))


def _mod_kernel(c_ref, w_ref, b_ref, o_ref):
    o_ref[...] = _dot(_silu(c_ref[...]), w_ref[...]) + b_ref[...]


def _modulation(cvec, w, b):
    rows, d = cvec.shape
    n = w.shape[1]
    tn = n // 4
    return pl.pallas_call(
        _mod_kernel,
        grid=(n // tn,),
        in_specs=[pl.BlockSpec((rows, d), lambda j: (0, 0)),
                  pl.BlockSpec((d, tn), lambda j: (0, j)),
                  pl.BlockSpec((1, tn), lambda j: (0, j))],
        out_specs=pl.BlockSpec((rows, tn), lambda j: (0, j)),
        out_shape=jax.ShapeDtypeStruct((rows, n), F32),
        compiler_params=_params("arbitrary"),
        name="modulation",
    )(cvec, w, b)


TAB_INTRA, TAB_HEAD_FW, TAB_HEAD_BW, TAB_TAIL_FW, TAB_TAIL_BW, TAB_STEP_FW, TAB_STEP_BW = range(7)


def _tab_kernel(dec_ref, tab_ref):
    c = RET_CHUNK
    row = lax.broadcasted_iota(jnp.int32, (c, c), 0).astype(F32)
    col = lax.broadcasted_iota(jnp.int32, (c, c), 1).astype(F32)
    rel = row - col
    for h in range(RET_HEADS):
        lf = -_softplus(-jnp.full((c, c), dec_ref[0, h], F32))
        lb = -_softplus(-jnp.full((c, c), dec_ref[1, h], F32))
        tab_ref[TAB_INTRA, h] = jnp.where(rel > 0, jnp.exp(lf * rel),
                                          jnp.where(rel < 0, jnp.exp(lb * (-rel)), 2.0))
        tab_ref[TAB_HEAD_FW, h] = jnp.exp(lf * (row + 1.0))
        tab_ref[TAB_HEAD_BW, h] = jnp.exp(lb * (c - row))
        tab_ref[TAB_TAIL_FW, h] = jnp.exp(lf * (c - 1.0 - row))
        tab_ref[TAB_TAIL_BW, h] = jnp.exp(lb * row)
        tab_ref[TAB_STEP_FW, h] = jnp.exp(lf * c)
        tab_ref[TAB_STEP_BW, h] = jnp.exp(lb * c)


def _decay_tables(dec):
    return pl.pallas_call(
        _tab_kernel,
        in_specs=[pl.BlockSpec(memory_space=pltpu.SMEM)],
        out_shape=jax.ShapeDtypeStruct((7, RET_HEADS, RET_CHUNK, RET_CHUNK), F32),
        name="decay_tables",
    )(dec)


def _norm_mod(x, m, nw):
    return _rms(x) * (nw * (1.0 + m[1:2])) + m[0:1]


def _lru_conv(prev, cur, nxt, cw_ref, cb_ref):
    tc = cur.shape[0]
    lead = prev.shape[0]
    ext = jnp.concatenate([prev, cur, nxt], axis=0)
    rows = ext.shape[0]
    cw = cw_ref[...]
    acc = cb_ref[...] + cur * cw[LRU_CONV_PAD_LEFT:LRU_CONV_PAD_LEFT + 1]
    for j in range(LRU_CONV_WIDTH):
        off = j - LRU_CONV_PAD_LEFT
        if off:
            acc = acc + pltpu.roll(ext, (-off) % rows, 0)[lead:lead + tc] * cw[j:j + 1]
    return acc


def _lru_gates(xc, xc_bf, wg_ref, lp_ref, direction):
    lp = lp_ref[direction]
    width = xc.shape[1]

    def gate(g):
        parts = [jnp.dot(xc_bf[:, GATE_TILE * m:GATE_TILE * (m + 1)], wg_ref[direction, g, m],
                         preferred_element_type=F32) for m in range(width // GATE_TILE)]
        return jnp.concatenate(parts, axis=1)

    r = _sigmoid(gate(0) + lp[0:1])
    i = _sigmoid(gate(1) + lp[1:2])
    log_a = (-LRU_C * _softplus(-lp[2:3])) * r
    a = jnp.exp(log_a)
    u = jnp.sqrt(-jnp.tanh(log_a) * (a * a + 1.0)) * (i * xc)
    return a, u


def _store_slabs(ref, val):
    for k in range(ref.shape[0]):
        ref[k] = val[:, k * LANES:(k + 1) * LANES]


def _lru_scan(a_ref, u_ref, h_ref, h_pre, carry_ref, reverse, tc):
    nslab = a_ref.shape[0]
    blk = 8 * SEG_ROWS
    nblk = tc // blk
    row = lax.broadcasted_iota(jnp.int32, (8, LANES), 0)
    order = tuple(range(SEG_ROWS - 1, -1, -1)) if reverse else tuple(range(SEG_ROWS))
    edge = 0 if reverse else 7

    def earlier(x, d, fill):
        if reverse:
            return jnp.where(row < 8 - d, pltpu.roll(x, 8 - d, 0), fill)
        return jnp.where(row >= d, pltpu.roll(x, d, 0), fill)

    carry = [carry_ref[k] for k in range(nslab)]
    for bi in range(nblk):
        base = ((nblk - 1 - bi) if reverse else bi) * blk
        for k in range(nslab):
            idx = [pl.ds(base + j, 8, stride=SEG_ROWS) for j in range(SEG_ROWS)]
            av = [a_ref[k, i, :] for i in idx]
            uv = [u_ref[k, i, :] for i in idx]
            h, p = uv[order[0]], av[order[0]]
            for j in order[1:]:
                h = av[j] * h + uv[j]
                p = av[j] * p
            d = 1
            while d < 8:
                h = p * earlier(h, d, 0.0) + h
                p = p * earlier(p, d, 1.0)
                d *= 2
            c_in = carry[k]
            state = earlier(h, 1, 0.0) + earlier(p, 1, 1.0) * c_in
            carry[k] = jnp.broadcast_to(h[edge:edge + 1] + p[edge:edge + 1] * c_in[edge:edge + 1], (8, LANES))
            for j in order:
                state = av[j] * state + uv[j]
                h_ref[(*h_pre, k, idx[j], slice(None))] = state
    for k in range(nslab):
        carry_ref[k] = carry[k]


def _bw_kernel(xprev_ref, x_ref, mod_ref, nw_ref, w_ref, tab_ref, cw_ref, cb_ref, wg_ref, lp_ref, h0_ref, s0_ref,
               xc_ref, hbw_ref, sbw_ref, kb_ref, ktf_ref, vb_ref, hin_ref, finh_ref, fins_ref,
               a_ref, u_ref, hcar_ref, s_ref, nx_ref, *, tc, nc):
    c = pl.program_id(1)
    chunk = nc - 1 - c
    dh = RET_CHUNK
    nslab = a_ref.shape[0]
    lw = nslab * LANES

    @pl.when(c == 0)
    def _():
        for k in range(nslab):
            hcar_ref[k] = jnp.broadcast_to(h0_ref[0, :, k * LANES:(k + 1) * LANES], (8, LANES))
        s_ref[...] = s0_ref[0]
        nx_ref[...] = jnp.zeros_like(nx_ref)

    m = mod_ref[0]
    nw = nw_ref[...]
    h_cur = _norm_mod(x_ref[0], m, nw).astype(BF16)
    hin_ref[0] = h_cur
    h_ext = jnp.concatenate([_norm_mod(xprev_ref[0], m, nw).astype(BF16), h_cur], axis=0)
    lx_ext = jnp.dot(h_ext, w_ref[:, 0:lw], preferred_element_type=F32)
    kv = jnp.dot(h_cur, w_ref[:, lw:], preferred_element_type=F32)
    lead = xprev_ref.shape[1]
    lx = lx_ext[lead:]
    lx_prev = jnp.where(chunk == 0, 0.0, lx_ext[0:lead])
    xc = _lru_conv(lx_prev, lx, nx_ref[...], cw_ref, cb_ref)
    nx_ref[...] = lx[0:nx_ref.shape[0]]
    xc_ref[0] = xc

    a, u = _lru_gates(xc, xc.astype(BF16), wg_ref, lp_ref, 1)
    _store_slabs(a_ref, a)
    _store_slabs(u_ref, u)
    _lru_scan(a_ref, u_ref, hbw_ref, (0,), hcar_ref, True, tc)

    k = kv[:, 0:RET_HEADS * dh] * dh ** -0.5
    v_bf = kv[:, RET_HEADS * dh:].astype(BF16)
    kb_ref[0] = k.astype(BF16)
    vb_ref[0] = v_bf
    pairs = [(n, hd) for n in range(tc // RET_CHUNK) for hd in range(RET_HEADS)]
    rows = lambda n: slice(n * RET_CHUNK, (n + 1) * RET_CHUNK)
    cols = lambda hd: slice(hd * dh, (hd + 1) * dh)
    for n, hd in pairs:
        ktf_ref[0, rows(n), cols(hd)] = (k[rows(n), cols(hd)] * tab_ref[TAB_TAIL_FW, hd]).astype(BF16)
    kvs = {(n, hd): lax.dot_general((k[rows(n), cols(hd)] * tab_ref[TAB_TAIL_BW, hd]).astype(BF16),
                                    v_bf[rows(n), cols(hd)], (((0,), (0,)), ((), ())), preferred_element_type=F32)
           for n, hd in pairs}
    for hd in range(RET_HEADS):
        s = s_ref[hd]
        for n in reversed(range(tc // RET_CHUNK)):
            sbw_ref[0, n, hd] = s.astype(BF16)
            s = tab_ref[TAB_STEP_BW, hd] * s + kvs[n, hd]
        s_ref[hd] = s

    @pl.when(c == 0)
    def _():
        for k in range(nslab):
            finh_ref[0, :, k * LANES:(k + 1) * LANES] = hbw_ref[0, k, tc - 1:tc, :]

    @pl.when(c == nc - 1)
    def _():
        fins_ref[0] = s_ref[...]


def _backward_sweep(x, mod, mod_row, nw, w_bw, tab, cw, cb, wg, lp, h0, s0, tc):
    b, t, d = x.shape
    lw = cw.shape[1]
    nc = t // tc
    nlead = tc // LEAD_ROWS
    nsub = tc // RET_CHUNK
    nslab = lw // LANES
    dh = RET_CHUNK
    ch = lambda j: nc - 1 - j
    tok = lambda dt: jax.ShapeDtypeStruct((b, t, lw), dt)
    tok_spec = pl.BlockSpec((1, tc, lw), lambda i, j: (i, ch(j), 0))
    return pl.pallas_call(
        functools.partial(_bw_kernel, tc=tc, nc=nc),
        grid=(b, nc),
        in_specs=[
            pl.BlockSpec((1, LEAD_ROWS, d), lambda i, j: (i, jnp.maximum(ch(j) * nlead - 1, 0), 0)),
            pl.BlockSpec((1, tc, d), lambda i, j: (i, ch(j), 0)),
            pl.BlockSpec((1, N_MOD, d), lambda i, j: (mod_row(i), 0, 0)),
            _const_spec(nw.shape), _resident(w_bw), _resident(tab), _const_spec(cw.shape), _const_spec(cb.shape),
            _resident(wg), _const_spec(lp.shape),
            pl.BlockSpec((1, 1, lw), lambda i, j: (i, 0, 0)),
            pl.BlockSpec((1, RET_HEADS, dh, dh), lambda i, j: (i, 0, 0, 0)),
        ],
        out_specs=[
            tok_spec,
            pl.BlockSpec((1, nslab, tc, LANES), lambda i, j: (i, 0, ch(j), 0)),
            pl.BlockSpec((1, nsub, RET_HEADS, dh, dh), lambda i, j: (i, ch(j), 0, 0, 0)),
            tok_spec, tok_spec, tok_spec,
            pl.BlockSpec((1, tc, d), lambda i, j: (i, ch(j), 0)),
            pl.BlockSpec((1, 1, lw), lambda i, j: (i, 0, 0)),
            pl.BlockSpec((1, RET_HEADS, dh, dh), lambda i, j: (i, 0, 0, 0)),
        ],
        out_shape=[
            tok(F32),
            jax.ShapeDtypeStruct((b, nslab, t, LANES), F32),
            jax.ShapeDtypeStruct((b, t // RET_CHUNK, RET_HEADS, dh, dh), BF16),
            tok(BF16), tok(BF16), tok(BF16),
            jax.ShapeDtypeStruct((b, t, d), BF16),
            jax.ShapeDtypeStruct((b, 1, lw), F32),
            jax.ShapeDtypeStruct((b, RET_HEADS, dh, dh), F32),
        ],
        scratch_shapes=[pltpu.VMEM((nslab, tc, LANES), F32),
                        pltpu.VMEM((nslab, tc, LANES), F32),
                        pltpu.VMEM((nslab, 8, LANES), F32),
                        pltpu.VMEM((RET_HEADS, dh, dh), F32),
                        pltpu.VMEM((HALO_ROWS, lw), F32)],
        compiler_params=_params("arbitrary", "arbitrary"),
        name="backward_sweep",
    )(x, x, mod, nw, w_bw, tab, cw, cb, wg, lp, h0, s0)


def _fw_kernel(x_ref, mod_ref, hin_ref, w_ref, xc_ref, hbw_ref, sbw_ref, kb_ref, ktf_ref, vb_ref,
               tab_ref, wg_ref, lp_ref, gnw_ref, wout_ref, h0_ref, s0_ref,
               o_ref, finh_ref, fins_ref, a_ref, u_ref, h_ref, hcar_ref, s_ref, y_ref, *, tc, nc):
    c = pl.program_id(1)
    dh = RET_CHUNK
    lw = xc_ref.shape[2]
    nslab = a_ref.shape[0]

    @pl.when(c == 0)
    def _():
        for k in range(nslab):
            hcar_ref[k] = jnp.broadcast_to(h0_ref[0, :, k * LANES:(k + 1) * LANES], (8, LANES))
        s_ref[...] = s0_ref[0]

    m = mod_ref[0]
    h_in = hin_ref[0]

    xc = xc_ref[0]
    a, u = _lru_gates(xc, xc.astype(BF16), wg_ref, lp_ref, 0)
    _store_slabs(a_ref, a)
    _store_slabs(u_ref, u)
    _lru_scan(a_ref, u_ref, h_ref, (), hcar_ref, False, tc)

    lru_g = jnp.dot(h_in, w_ref[:, 0:lw], preferred_element_type=F32)
    qg = jnp.dot(h_in, w_ref[:, lw:], preferred_element_type=F32)
    q_all = qg[:, 0:RET_HEADS * dh]
    ret_g = qg[:, RET_HEADS * dh:]
    for k in range(nslab):
        cols = slice(k * LANES, (k + 1) * LANES)
        y_ref[:, cols] = ((h_ref[k] + hbw_ref[0, k]) * _gelu_tanh(lru_g[:, cols])).astype(BF16)

    gnw = gnw_ref[...]
    pairs = [(n, hd) for n in range(tc // RET_CHUNK) for hd in range(RET_HEADS)]
    rows = lambda n: slice(n * RET_CHUNK, (n + 1) * RET_CHUNK)
    cols = lambda hd: slice(hd * dh, (hd + 1) * dh)
    kv = {(n, hd): lax.dot_general(ktf_ref[0, rows(n), cols(hd)], vb_ref[0, rows(n), cols(hd)],
                                   (((0,), (0,)), ((), ())), preferred_element_type=F32) for n, hd in pairs}
    state = {}
    for hd in range(RET_HEADS):
        s = s_ref[hd]
        for n in range(tc // RET_CHUNK):
            state[n, hd] = s
            s = tab_ref[TAB_STEP_FW, hd] * s + kv[n, hd]
        s_ref[hd] = s
    scores = {(n, hd): lax.dot_general(q_all[rows(n), cols(hd)].astype(BF16), kb_ref[0, rows(n), cols(hd)],
                                       (((1,), (1,)), ((), ())), preferred_element_type=F32) for n, hd in pairs}
    intra = {(n, hd): _dot(scores[n, hd] * tab_ref[TAB_INTRA, hd], vb_ref[0, rows(n), cols(hd)])
             for n, hd in pairs}
    out = {}
    for n, hd in pairs:
        q = q_all[rows(n), cols(hd)]
        q_both = jnp.concatenate([q * tab_ref[TAB_HEAD_FW, hd], q * tab_ref[TAB_HEAD_BW, hd]], axis=1)
        s_both = jnp.concatenate([state[n, hd].astype(BF16), sbw_ref[0, n, hd]], axis=0)
        out[n, hd] = intra[n, hd] + _dot(q_both, s_both)
    for n, hd in pairs:
        o = out[n, hd]
        dev = o - jnp.mean(o, axis=-1, keepdims=True)
        on = dev * lax.rsqrt(jnp.mean(dev * dev, axis=-1, keepdims=True) + EPS) * gnw[:, cols(hd)]
        y_ref[rows(n), lw + hd * dh:lw + (hd + 1) * dh] = (on * _silu(ret_g[rows(n), cols(hd)])).astype(BF16)

    o_ref[0] = x_ref[0] + m[2:3] * _dot(y_ref[...], wout_ref[...])

    @pl.when(c == 0)
    def _():
        for k in range(nslab):
            finh_ref[0, :, k * LANES:(k + 1) * LANES] = h_ref[k, 0:1, :]

    @pl.when(c == nc - 1)
    def _():
        fins_ref[0] = s_ref[...]


def _forward_sweep(x, mod, mod_row, hin, w_fw, xc, hbw, sbw, kb, ktf, vb, tab, wg, lp, gnw, wout_bf, h0, s0, tc):
    b, t, d = x.shape
    lw = xc.shape[2]
    nc = t // tc
    nsub = tc // RET_CHUNK
    nslab = lw // LANES
    dh = RET_CHUNK
    tok_spec = pl.BlockSpec((1, tc, lw), lambda i, j: (i, j, 0))
    return pl.pallas_call(
        functools.partial(_fw_kernel, tc=tc, nc=nc),
        grid=(b, nc),
        in_specs=[
            pl.BlockSpec((1, tc, d), lambda i, j: (i, j, 0)),
            pl.BlockSpec((1, N_MOD, d), lambda i, j: (mod_row(i), 0, 0)),
            pl.BlockSpec((1, tc, d), lambda i, j: (i, j, 0)), _resident(w_fw),
            tok_spec,
            pl.BlockSpec((1, nslab, tc, LANES), lambda i, j: (i, 0, j, 0)),
            pl.BlockSpec((1, nsub, RET_HEADS, dh, dh), lambda i, j: (i, j, 0, 0, 0)),
            tok_spec, tok_spec, tok_spec,
            _resident(tab),
            _resident(wg), _const_spec(lp.shape), _const_spec(gnw.shape), _resident(wout_bf),
            pl.BlockSpec((1, 1, lw), lambda i, j: (i, 0, 0)),
            pl.BlockSpec((1, RET_HEADS, dh, dh), lambda i, j: (i, 0, 0, 0)),
        ],
        out_specs=[
            pl.BlockSpec((1, tc, d), lambda i, j: (i, j, 0)),
            pl.BlockSpec((1, 1, lw), lambda i, j: (i, 0, 0)),
            pl.BlockSpec((1, RET_HEADS, dh, dh), lambda i, j: (i, 0, 0, 0)),
        ],
        out_shape=[
            jax.ShapeDtypeStruct((b, t, d), F32),
            jax.ShapeDtypeStruct((b, 1, lw), F32),
            jax.ShapeDtypeStruct((b, RET_HEADS, dh, dh), F32),
        ],
        scratch_shapes=[pltpu.VMEM((nslab, tc, LANES), F32),
                        pltpu.VMEM((nslab, tc, LANES), F32),
                        pltpu.VMEM((nslab, tc, LANES), F32),
                        pltpu.VMEM((nslab, 8, LANES), F32),
                        pltpu.VMEM((RET_HEADS, dh, dh), F32),
                        pltpu.VMEM((tc, wout_bf.shape[0]), BF16)],
        compiler_params=_params("arbitrary", "arbitrary"),
        name="forward_sweep",
    )(x, mod, hin, w_fw, xc, hbw, sbw, kb, ktf, vb, tab, wg, lp, gnw, wout_bf, h0, s0)


def _ffn_kernel(*refs, tf, halo, period, taps, final):
    if halo:
        xp_ref, x_ref, xn_ref = refs[:3]
        refs = refs[3:]
    else:
        x_ref = refs[0]
        refs = refs[1:]
    mod_ref, n2w_ref, wg_ref, wu_ref, wd_ref, cw_ref, cb_ref, fnw_ref, o_ref = refs
    m = mod_ref[0]
    n2wm = n2w_ref[...] * (1.0 + m[4:5])

    def norm_mod(x):
        return _rms(x) * n2wm + m[3:4]

    x = x_ref[0]
    h2_cur = norm_mod(x).astype(BF16)
    h2 = h2_cur
    if halo:
        j = pl.program_id(1)
        nj = pl.num_programs(1)
        h2 = jnp.concatenate([jnp.where(j == 0, 0.0, norm_mod(xp_ref[0])).astype(BF16), h2_cur,
                              jnp.where(j == nj - 1, 0.0, norm_mod(xn_ref[0])).astype(BF16)], axis=0)
    rows = tf + 2 * halo
    nper = rows // period
    sub = lax.broadcasted_iota(jnp.int32, (nper, 8, FFN_TILE), 1)

    def neighbour(g, step):
        r3 = pltpu.roll(g, (-step) % rows, 0).reshape(nper, period, FFN_TILE)
        if step < 0:
            fixed = [jnp.where(sub == 0, 0.0, r3[:, 0:8]), r3[:, 8:]]
        else:
            fixed = [r3[:, 0:period - 8], jnp.where(sub == 7, 0.0, r3[:, period - 8:])]
        return jnp.concatenate(fixed, axis=1).reshape(rows, FFN_TILE)

    acts = []
    for n in range(wg_ref.shape[1] // FFN_TILE):
        cs = slice(n * FFN_TILE, (n + 1) * FFN_TILE)
        g = jnp.dot(h2, wg_ref[:, cs], preferred_element_type=F32)
        g_left = neighbour(g, -1)
        g_right = neighbour(g, 1)
        cw = cw_ref[:, cs]
        conv = cb_ref[:, cs]
        for dr, wrow in taps:
            lo = halo + dr * period
            conv = (conv + g_left[lo:lo + tf] * cw[3 * wrow:3 * wrow + 1]
                    + g[lo:lo + tf] * cw[3 * wrow + 1:3 * wrow + 2]
                    + g_right[lo:lo + tf] * cw[3 * wrow + 2:3 * wrow + 3])
        up = jnp.dot(h2_cur, wu_ref[:, cs], preferred_element_type=F32)
        acts.append((_gelu_tanh(conv) * up).astype(BF16))

    out = x + m[5:6] * jnp.dot(jnp.concatenate(acts, axis=1), wd_ref[...], preferred_element_type=F32)
    o_ref[0] = _rms(out) * fnw_ref[...] if final else out


def _ffn(x, mod, mod_row, n2w, wg, wu, wd, cw, cb, fnw, latent, final):
    b, t, d = x.shape
    if latent:
        tf, halo, period = FFN_ROWS, GRID_W, GRID_W
        taps = ((-1, 0), (0, 1), (1, 2))
        nb = tf // halo
        x_specs = [pl.BlockSpec((1, halo, d), lambda i, j: (i, jnp.maximum(j * nb - 1, 0), 0)),
                   pl.BlockSpec((1, tf, d), lambda i, j: (i, j, 0)),
                   pl.BlockSpec((1, halo, d), lambda i, j: (i, jnp.minimum((j + 1) * nb, t // halo - 1), 0))]
        x_args = (x, x, x)
    else:
        tf, halo, period = t, 0, t
        taps = ((0, 1),)
        x_specs = [pl.BlockSpec((1, tf, d), lambda i, j: (i, j, 0))]
        x_args = (x,)
    assert period & (period - 1) == 0 and t % tf == 0
    return pl.pallas_call(
        functools.partial(_ffn_kernel, tf=tf, halo=halo, period=period, taps=taps, final=final),
        grid=(b, t // tf),
        in_specs=x_specs + [
            pl.BlockSpec((1, N_MOD, d), lambda i, j: (mod_row(i), 0, 0)),
            _const_spec(n2w.shape), _resident(wg), _resident(wu), _resident(wd),
            _const_spec(cw.shape), _const_spec(cb.shape), _const_spec(fnw.shape)],
        out_specs=pl.BlockSpec((1, tf, d), lambda i, j: (i, j, 0)),
        out_shape=jax.ShapeDtypeStruct((b, t, d), F32),
        compiler_params=_params("arbitrary", "arbitrary"),
        name="ffn",
    )(*x_args, mod, n2w, wg, wu, wd, cw, cb, fnw)


def _block_diag_tiles(w):
    nb, c, _ = w.shape
    per = GATE_TILE // c
    eye = jnp.eye(per, dtype=w.dtype)
    w = w.reshape(nb // per, per, c, c)
    return jnp.einsum('mjab,jk->mjakb', w, eye).reshape(nb // per, GATE_TILE, GATE_TILE)


def _layer(x, mod, mod_row, p, tab, states, latent, final):
    b, t, d = x.shape
    tc = min(t, TIME_CHUNK)
    xc, hbw, sbw, kb, ktf, vb, hin, fin_h_bw, fin_s_bw = _backward_sweep(
        x, mod, mod_row, p['norm1_w'], p['w_bw'], tab, p['lru_conv_w'], p['lru_conv_b'], p['wg'], p['lp'],
        states[1], states[3], tc)
    x, fin_h_fw, fin_s_fw = _forward_sweep(
        x, mod, mod_row, hin, p['w_fw'], xc, hbw, sbw, kb, ktf, vb, tab, p['wg'], p['lp'],
        p['ret_gn_w'], p['w_out'], states[0], states[2], tc)
    x = _ffn(x, mod, mod_row, p['norm2_w'], p['ffn_wg'], p['ffn_wu'], p['ffn_wd'], p['ffn_cw'], p['ffn_cb'],
             p['final_norm_w'], latent, final)
    return x, (fin_h_fw, fin_h_bw, fin_s_fw, fin_s_bw)


def kernel(x_prompt, x_sample, state_lru_fw, state_lru_bw, state_ret_fw, state_ret_bw, c, c_ctx,
           norm1_w, w_mod, b_mod, w_in, lru_conv_w, lru_conv_b,
           lru_wa_fw, lru_ba_fw, lru_wx_fw, lru_bx_fw, lru_lambda_fw,
           lru_wa_bw, lru_ba_bw, lru_wx_bw, lru_bx_bw, lru_lambda_bw,
           ret_decay_fw, ret_decay_bw, ret_gn_w, w_out, norm2_w,
           ffn_w_gate, ffn_w_up, ffn_conv_w, ffn_conv_b, ffn_w_down, final_norm_w):
    depth = w_in.shape[0]
    bp = x_prompt.shape[0]
    bs, d = c.shape
    lw = lru_conv_w.shape[2]
    dh = RET_CHUNK
    rw = RET_HEADS * dh

    ctx_row = bs
    rows = -(-(bs + 1) // 8) * 8
    cvec = jnp.zeros((rows, d), F32).at[:bs].set(c).at[ctx_row].set(c_ctx)
    row_ctx = lambda i: ctx_row
    row_lat = lambda i: i

    x_p, x_s = x_prompt, x_sample
    fins = []
    for l in range(depth):
        mod = _modulation(cvec, w_mod[l], b_mod[l][None]).reshape(rows, N_MOD, d)
        tab = _decay_tables(jnp.stack([ret_decay_fw[l], ret_decay_bw[l]]).astype(F32))
        p = dict(
            norm1_w=norm1_w[l][None],
            w_bw=jnp.concatenate([w_in[l][:, 0:lw], w_in[l][:, 2 * lw + rw:2 * lw + 3 * rw]], axis=1).astype(BF16),
            w_fw=jnp.concatenate([w_in[l][:, lw:2 * lw + rw], w_in[l][:, 2 * lw + 3 * rw:]], axis=1).astype(BF16),
            lru_conv_w=lru_conv_w[l], lru_conv_b=lru_conv_b[l][None],
            wg=jnp.stack([jnp.stack([_block_diag_tiles(lru_wa_fw[l]), _block_diag_tiles(lru_wx_fw[l])]),
                          jnp.stack([_block_diag_tiles(lru_wa_bw[l]), _block_diag_tiles(lru_wx_bw[l])])]
                         ).astype(BF16),
            lp=jnp.stack([jnp.stack([lru_ba_fw[l], lru_bx_fw[l], lru_lambda_fw[l]]),
                          jnp.stack([lru_ba_bw[l], lru_bx_bw[l], lru_lambda_bw[l]])]).astype(F32),
            ret_gn_w=ret_gn_w[l][None], w_out=w_out[l].astype(BF16), norm2_w=norm2_w[l][None],
            ffn_wg=ffn_w_gate[l].astype(BF16), ffn_wu=ffn_w_up[l].astype(BF16), ffn_wd=ffn_w_down[l].astype(BF16),
            ffn_cw=ffn_conv_w[l].reshape(9, -1), ffn_cb=ffn_conv_b[l][None],
            final_norm_w=final_norm_w[None],
        )
        final = l == depth - 1
        zeros_h = jnp.zeros((bp, 1, lw), F32)
        zeros_s = jnp.zeros((bp, RET_HEADS, dh, dh), F32)
        x_p, fin = _layer(x_p, mod, row_ctx, p, tab, (zeros_h, zeros_h, zeros_s, zeros_s), False, final)
        fins.append(fin)
        init_lat = (state_lru_fw[:, l][:, None].astype(F32), state_lru_bw[:, l][:, None].astype(F32),
                    state_ret_fw[:, l].astype(F32), state_ret_bw[:, l].astype(F32))
        x_s, _ = _layer(x_s, mod, row_lat, p, tab, init_lat, True, final)

    dt = x_prompt.dtype
    new_lru_fw = jnp.concatenate([f[0] for f in fins], axis=1).astype(dt)
    new_lru_bw = jnp.concatenate([f[1] for f in fins], axis=1).astype(dt)
    new_ret_fw = jnp.stack([f[2] for f in fins], axis=1).astype(dt)
    new_ret_bw = jnp.stack([f[3] for f in fins], axis=1).astype(dt)
    return (x_p, x_s, new_lru_fw, new_lru_bw, new_ret_fw, new_ret_bw)
```

```python
import functools
import math

import jax
import jax.numpy as jnp
from jax import lax
from jax.experimental import pallas as pl
from jax.experimental.pallas import tpu as pltpu

F32 = jnp.float32
BF16 = jnp.bfloat16

N_MOD = 6
EPS = 1e-6
LRU_C = 8.0
LRU_BLOCKS = 8
LRU_CONV_WIDTH = 4
LRU_CONV_PAD_LEFT = 2
RET_HEADS = 4
RET_CHUNK = 128
GRID_W = 64
GATE_TILE = 256
FFN_TILE = 256
TILE_GROUP = 4
HALO_ROWS = 8
LEAD_ROWS = 16
LANES = 128
SEG_ROWS = 4
TIME_CHUNK = 512
FFN_ROWS = 512
VMEM_LIMIT = 56 * 1024 * 1024
def _dot(a, b):
    return jnp.dot(a.astype(BF16), b.astype(BF16), preferred_element_type=F32)


def _silu(x):
    h = 0.5 * x
    return h * jnp.tanh(h) + h


def _gelu_tanh(x):
    c = math.sqrt(2.0 / math.pi)
    return (0.5 * x) * (1.0 + jnp.tanh(x * (c + (c * 0.044715) * (x * x))))


def _softplus(x):
    return jnp.maximum(x, 0.0) + jnp.log1p(jnp.exp(-jnp.abs(x)))


def _rms(x):
    return x * lax.rsqrt(jnp.mean(x * x, axis=-1, keepdims=True) + EPS)


def _params(*sem):
    return pltpu.CompilerParams(dimension_semantics=sem, vmem_limit_bytes=VMEM_LIMIT)


def _const_spec(shape):
    nd = len(shape)
    return pl.BlockSpec(shape, lambda *_: (0,) * nd)


def _resident(a):
    return pl.BlockSpec(a.shape, lambda *_: (0,) * a.ndim, pipeline_mode=pl.Buffered(1))


def _mod_kernel(c_ref, w_ref, b_ref, o_ref):
    o_ref[...] = _dot(_silu(c_ref[...]), w_ref[...]) + b_ref[...]


def _modulation(cvec, w, b):
    rows, d = cvec.shape
    n = w.shape[1]
    tn = n // 4
    return pl.pallas_call(
        _mod_kernel,
        grid=(n // tn,),
        in_specs=[pl.BlockSpec((rows, d), lambda j: (0, 0)),
                  pl.BlockSpec((d, tn), lambda j: (0, j)),
                  pl.BlockSpec((1, tn), lambda j: (0, j))],
        out_specs=pl.BlockSpec((rows, tn), lambda j: (0, j)),
        out_shape=jax.ShapeDtypeStruct((rows, n), F32),
        compiler_params=_params("arbitrary"),
        name="modulation",
    )(cvec, w, b)


TAB_INTRA, TAB_HEAD_FW, TAB_HEAD_BW, TAB_TAIL_FW, TAB_TAIL_BW, TAB_STEP_FW, TAB_STEP_BW = range(7)


def _tab_kernel(dec_ref, tab_ref):
    c = RET_CHUNK
    row = lax.broadcasted_iota(jnp.int32, (c, c), 0).astype(F32)
    col = lax.broadcasted_iota(jnp.int32, (c, c), 1).astype(F32)
    rel = row - col
    for h in range(RET_HEADS):
        lf = -_softplus(-jnp.full((c, c), dec_ref[0, h], F32))
        lb = -_softplus(-jnp.full((c, c), dec_ref[1, h], F32))
        tab_ref[TAB_INTRA, h] = jnp.where(rel > 0, jnp.exp(lf * rel),
                                          jnp.where(rel < 0, jnp.exp(lb * (-rel)), 2.0))
        tab_ref[TAB_HEAD_FW, h] = jnp.exp(lf * (row + 1.0))
        tab_ref[TAB_HEAD_BW, h] = jnp.exp(lb * (c - row))
        tab_ref[TAB_TAIL_FW, h] = jnp.exp(lf * (c - 1.0 - row))
        tab_ref[TAB_TAIL_BW, h] = jnp.exp(lb * row)
        tab_ref[TAB_STEP_FW, h] = jnp.exp(lf * c)
        tab_ref[TAB_STEP_BW, h] = jnp.exp(lb * c)


def _decay_tables(dec):
    return pl.pallas_call(
        _tab_kernel,
        in_specs=[pl.BlockSpec(memory_space=pltpu.SMEM)],
        out_shape=jax.ShapeDtypeStruct((7, RET_HEADS, RET_CHUNK, RET_CHUNK), F32),
        name="decay_tables",
    )(dec)


def _norm_mod(x, m, nw):
    return _rms(x) * (nw * (1.0 + m[1:2])) + m[0:1]


def _lru_conv(prev, cur, nxt, cw_ref, cb_ref):
    tc = cur.shape[0]
    lead = prev.shape[0]
    ext = jnp.concatenate([prev, cur, nxt], axis=0)
    rows = ext.shape[0]
    cw = cw_ref[...]
    acc = cb_ref[...] + cur * cw[LRU_CONV_PAD_LEFT:LRU_CONV_PAD_LEFT + 1]
    for j in range(LRU_CONV_WIDTH):
        off = j - LRU_CONV_PAD_LEFT
        if off:
            acc = acc + pltpu.roll(ext, (-off) % rows, 0)[lead:lead + tc] * cw[j:j + 1]
    return acc


def _lru_gates(xc, xc_bf, wg_ref, lp_ref, direction):
    lp = lp_ref[direction]
    width = xc.shape[1]

    def gate(g):
        parts = [jnp.dot(xc_bf[:, GATE_TILE * m:GATE_TILE * (m + 1)], wg_ref[direction, g, m],
                         preferred_element_type=F32) for m in range(width // GATE_TILE)]
        return jnp.concatenate(parts, axis=1)

    tanh_r = jnp.tanh(0.5 * gate(0) + 0.5 * lp[0:1])
    tanh_i = jnp.tanh(0.5 * gate(1) + 0.5 * lp[1:2])
    half = (-0.5 * LRU_C) * _softplus(-lp[2:3])
    log_a = half * tanh_r + half
    a = jnp.exp(log_a)
    gain2 = -jnp.tanh(log_a) * (a * a + 1.0)
    gain = jnp.where(gain2 == 0.0, 0.0, gain2 * lax.rsqrt(gain2))
    u = gain * ((0.5 * tanh_i + 0.5) * xc)
    return a, u


def _store_slabs(ref, val):
    for k in range(ref.shape[0]):
        ref[k] = val[:, k * LANES:(k + 1) * LANES]


def _lru_scan(a_ref, u_ref, h_ref, h_pre, carry_ref, reverse, tc):
    nslab = a_ref.shape[0]
    blk = 8 * SEG_ROWS
    nblk = tc // blk
    row = lax.broadcasted_iota(jnp.int32, (8, LANES), 0)
    order = tuple(range(SEG_ROWS - 1, -1, -1)) if reverse else tuple(range(SEG_ROWS))
    edge = 0 if reverse else 7

    def earlier(x, d, fill):
        if reverse:
            return jnp.where(row < 8 - d, pltpu.roll(x, 8 - d, 0), fill)
        return jnp.where(row >= d, pltpu.roll(x, d, 0), fill)

    carry = [carry_ref[k] for k in range(nslab)]
    for bi in range(nblk):
        base = ((nblk - 1 - bi) if reverse else bi) * blk
        for k in range(nslab):
            idx = [pl.ds(base + j, 8, stride=SEG_ROWS) for j in range(SEG_ROWS)]
            av = [a_ref[k, i, :] for i in idx]
            uv = [u_ref[k, i, :] for i in idx]
            h, p = uv[order[0]], av[order[0]]
            for j in order[1:]:
                h = av[j] * h + uv[j]
                p = av[j] * p
            d = 1
            while d < 8:
                h = p * earlier(h, d, 0.0) + h
                p = p * earlier(p, d, 1.0)
                d *= 2
            c_in = carry[k]
            state = earlier(h, 1, 0.0) + earlier(p, 1, 1.0) * c_in
            carry[k] = jnp.broadcast_to(h[edge:edge + 1] + p[edge:edge + 1] * c_in[edge:edge + 1], (8, LANES))
            for j in order:
                state = av[j] * state + uv[j]
                h_ref[(*h_pre, k, idx[j], slice(None))] = state
    for k in range(nslab):
        carry_ref[k] = carry[k]


def _bw_kernel(xprev_ref, x_ref, mod_ref, nw_ref, w_ref, tab_ref, cw_ref, cb_ref, wg_ref, lp_ref, h0_ref, s0_ref,
               xc_ref, hbw_ref, sbw_ref, kb_ref, ktf_ref, vb_ref, hin_ref, finh_ref, fins_ref,
               a_ref, u_ref, hcar_ref, s_ref, nx_ref, *, tc, nc):
    c = pl.program_id(1)
    chunk = nc - 1 - c
    dh = RET_CHUNK
    nslab = a_ref.shape[0]
    lw = nslab * LANES

    @pl.when(c == 0)
    def _():
        for k in range(nslab):
            hcar_ref[k] = jnp.broadcast_to(h0_ref[0, :, k * LANES:(k + 1) * LANES], (8, LANES))
        s_ref[...] = s0_ref[0]
        nx_ref[...] = jnp.zeros_like(nx_ref)

    m = mod_ref[0]
    nw = nw_ref[...]
    h_cur = _norm_mod(x_ref[0], m, nw).astype(BF16)
    hin_ref[0] = h_cur
    h_ext = jnp.concatenate([_norm_mod(xprev_ref[0], m, nw).astype(BF16), h_cur], axis=0)
    lx_ext = jnp.dot(h_ext, w_ref[:, 0:lw], preferred_element_type=F32)
    kv = jnp.dot(h_cur, w_ref[:, lw:], preferred_element_type=F32)
    lead = xprev_ref.shape[1]
    lx = lx_ext[lead:]
    lx_prev = jnp.where(chunk == 0, 0.0, lx_ext[0:lead])
    xc = _lru_conv(lx_prev, lx, nx_ref[...], cw_ref, cb_ref)
    nx_ref[...] = lx[0:nx_ref.shape[0]]
    xc_ref[0] = xc

    a, u = _lru_gates(xc, xc.astype(BF16), wg_ref, lp_ref, 1)
    _store_slabs(a_ref, a)
    _store_slabs(u_ref, u)
    _lru_scan(a_ref, u_ref, hbw_ref, (0,), hcar_ref, True, tc)

    k = kv[:, 0:RET_HEADS * dh] * dh ** -0.5
    v_bf = kv[:, RET_HEADS * dh:].astype(BF16)
    kb_ref[0] = k.astype(BF16)
    vb_ref[0] = v_bf
    pairs = [(n, hd) for n in range(tc // RET_CHUNK) for hd in range(RET_HEADS)]
    rows = lambda n: slice(n * RET_CHUNK, (n + 1) * RET_CHUNK)
    cols = lambda hd: slice(hd * dh, (hd + 1) * dh)
    for n, hd in pairs:
        ktf_ref[0, rows(n), cols(hd)] = (k[rows(n), cols(hd)] * tab_ref[TAB_TAIL_FW, hd]).astype(BF16)
    kvs = {(n, hd): lax.dot_general((k[rows(n), cols(hd)] * tab_ref[TAB_TAIL_BW, hd]).astype(BF16),
                                    v_bf[rows(n), cols(hd)], (((0,), (0,)), ((), ())), preferred_element_type=F32)
           for n, hd in pairs}
    for hd in range(RET_HEADS):
        s = s_ref[hd]
        for n in reversed(range(tc // RET_CHUNK)):
            sbw_ref[0, n, hd] = s.astype(BF16)
            s = tab_ref[TAB_STEP_BW, hd] * s + kvs[n, hd]
        s_ref[hd] = s

    @pl.when(c == 0)
    def _():
        for k in range(nslab):
            finh_ref[0, :, k * LANES:(k + 1) * LANES] = hbw_ref[0, k, tc - 1:tc, :]

    @pl.when(c == nc - 1)
    def _():
        fins_ref[0] = s_ref[...]


def _backward_sweep(x, mod, mod_row, nw, w_bw, tab, cw, cb, wg, lp, h0, s0, tc):
    b, t, d = x.shape
    lw = cw.shape[1]
    nc = t // tc
    nlead = tc // LEAD_ROWS
    nsub = tc // RET_CHUNK
    nslab = lw // LANES
    dh = RET_CHUNK
    ch = lambda j: nc - 1 - j
    tok = lambda dt: jax.ShapeDtypeStruct((b, t, lw), dt)
    tok_spec = pl.BlockSpec((1, tc, lw), lambda i, j: (i, ch(j), 0))
    return pl.pallas_call(
        functools.partial(_bw_kernel, tc=tc, nc=nc),
        grid=(b, nc),
        in_specs=[
            pl.BlockSpec((1, LEAD_ROWS, d), lambda i, j: (i, jnp.maximum(ch(j) * nlead - 1, 0), 0)),
            pl.BlockSpec((1, tc, d), lambda i, j: (i, ch(j), 0)),
            pl.BlockSpec((1, N_MOD, d), lambda i, j: (mod_row(i), 0, 0)),
            _const_spec(nw.shape), _resident(w_bw), _resident(tab), _const_spec(cw.shape), _const_spec(cb.shape),
            _resident(wg), _const_spec(lp.shape),
            pl.BlockSpec((1, 1, lw), lambda i, j: (i, 0, 0)),
            pl.BlockSpec((1, RET_HEADS, dh, dh), lambda i, j: (i, 0, 0, 0)),
        ],
        out_specs=[
            tok_spec,
            pl.BlockSpec((1, nslab, tc, LANES), lambda i, j: (i, 0, ch(j), 0)),
            pl.BlockSpec((1, nsub, RET_HEADS, dh, dh), lambda i, j: (i, ch(j), 0, 0, 0)),
            tok_spec, tok_spec, tok_spec,
            pl.BlockSpec((1, tc, d), lambda i, j: (i, ch(j), 0)),
            pl.BlockSpec((1, 1, lw), lambda i, j: (i, 0, 0)),
            pl.BlockSpec((1, RET_HEADS, dh, dh), lambda i, j: (i, 0, 0, 0)),
        ],
        out_shape=[
            tok(F32),
            jax.ShapeDtypeStruct((b, nslab, t, LANES), F32),
            jax.ShapeDtypeStruct((b, t // RET_CHUNK, RET_HEADS, dh, dh), BF16),
            tok(BF16), tok(BF16), tok(BF16),
            jax.ShapeDtypeStruct((b, t, d), BF16),
            jax.ShapeDtypeStruct((b, 1, lw), F32),
            jax.ShapeDtypeStruct((b, RET_HEADS, dh, dh), F32),
        ],
        scratch_shapes=[pltpu.VMEM((nslab, tc, LANES), F32),
                        pltpu.VMEM((nslab, tc, LANES), F32),
                        pltpu.VMEM((nslab, 8, LANES), F32),
                        pltpu.VMEM((RET_HEADS, dh, dh), F32),
                        pltpu.VMEM((HALO_ROWS, lw), F32)],
        compiler_params=_params("arbitrary", "arbitrary"),
        name="backward_sweep",
    )(x, x, mod, nw, w_bw, tab, cw, cb, wg, lp, h0, s0)


def _fw_kernel(x_ref, mod_ref, hin_ref, w_ref, xc_ref, hbw_ref, sbw_ref, kb_ref, ktf_ref, vb_ref,
               tab_ref, wg_ref, lp_ref, gnw_ref, wout_ref, h0_ref, s0_ref,
               o_ref, finh_ref, fins_ref, a_ref, u_ref, h_ref, hcar_ref, s_ref, y_ref, *, tc, nc):
    c = pl.program_id(1)
    dh = RET_CHUNK
    lw = xc_ref.shape[2]
    nslab = a_ref.shape[0]

    @pl.when(c == 0)
    def _():
        for k in range(nslab):
            hcar_ref[k] = jnp.broadcast_to(h0_ref[0, :, k * LANES:(k + 1) * LANES], (8, LANES))
        s_ref[...] = s0_ref[0]

    m = mod_ref[0]
    h_in = hin_ref[0]

    xc = xc_ref[0]
    a, u = _lru_gates(xc, xc.astype(BF16), wg_ref, lp_ref, 0)
    _store_slabs(a_ref, a)
    _store_slabs(u_ref, u)
    _lru_scan(a_ref, u_ref, h_ref, (), hcar_ref, False, tc)

    lru_g = jnp.dot(h_in, w_ref[:, 0:lw], preferred_element_type=F32)
    qg = jnp.dot(h_in, w_ref[:, lw:], preferred_element_type=F32)
    q_all = qg[:, 0:RET_HEADS * dh]
    ret_g = qg[:, RET_HEADS * dh:]
    for k in range(nslab):
        cols = slice(k * LANES, (k + 1) * LANES)
        y_ref[:, cols] = ((h_ref[k] + hbw_ref[0, k]) * _gelu_tanh(lru_g[:, cols])).astype(BF16)

    gnw = gnw_ref[...]
    pairs = [(n, hd) for n in range(tc // RET_CHUNK) for hd in range(RET_HEADS)]
    rows = lambda n: slice(n * RET_CHUNK, (n + 1) * RET_CHUNK)
    cols = lambda hd: slice(hd * dh, (hd + 1) * dh)
    kv = {(n, hd): lax.dot_general(ktf_ref[0, rows(n), cols(hd)], vb_ref[0, rows(n), cols(hd)],
                                   (((0,), (0,)), ((), ())), preferred_element_type=F32) for n, hd in pairs}
    state = {}
    for hd in range(RET_HEADS):
        s = s_ref[hd]
        for n in range(tc // RET_CHUNK):
            state[n, hd] = s
            s = tab_ref[TAB_STEP_FW, hd] * s + kv[n, hd]
        s_ref[hd] = s
    scores = {(n, hd): lax.dot_general(q_all[rows(n), cols(hd)].astype(BF16), kb_ref[0, rows(n), cols(hd)],
                                       (((1,), (1,)), ((), ())), preferred_element_type=F32) for n, hd in pairs}
    intra = {(n, hd): _dot(scores[n, hd] * tab_ref[TAB_INTRA, hd], vb_ref[0, rows(n), cols(hd)])
             for n, hd in pairs}
    out = {}
    for n, hd in pairs:
        q = q_all[rows(n), cols(hd)]
        q_both = jnp.concatenate([q * tab_ref[TAB_HEAD_FW, hd], q * tab_ref[TAB_HEAD_BW, hd]], axis=1)
        s_both = jnp.concatenate([state[n, hd].astype(BF16), sbw_ref[0, n, hd]], axis=0)
        out[n, hd] = intra[n, hd] + _dot(q_both, s_both)
    for n, hd in pairs:
        o = out[n, hd]
        dev = o - jnp.mean(o, axis=-1, keepdims=True)
        on = dev * lax.rsqrt(jnp.mean(dev * dev, axis=-1, keepdims=True) + EPS) * gnw[:, cols(hd)]
        y_ref[rows(n), lw + hd * dh:lw + (hd + 1) * dh] = (on * _silu(ret_g[rows(n), cols(hd)])).astype(BF16)

    o_ref[0] = x_ref[0] + m[2:3] * _dot(y_ref[...], wout_ref[...])

    @pl.when(c == 0)
    def _():
        for k in range(nslab):
            finh_ref[0, :, k * LANES:(k + 1) * LANES] = h_ref[k, 0:1, :]

    @pl.when(c == nc - 1)
    def _():
        fins_ref[0] = s_ref[...]


def _forward_sweep(x, mod, mod_row, hin, w_fw, xc, hbw, sbw, kb, ktf, vb, tab, wg, lp, gnw, wout_bf, h0, s0, tc):
    b, t, d = x.shape
    lw = xc.shape[2]
    nc = t // tc
    nsub = tc // RET_CHUNK
    nslab = lw // LANES
    dh = RET_CHUNK
    tok_spec = pl.BlockSpec((1, tc, lw), lambda i, j: (i, j, 0))
    return pl.pallas_call(
        functools.partial(_fw_kernel, tc=tc, nc=nc),
        grid=(b, nc),
        in_specs=[
            pl.BlockSpec((1, tc, d), lambda i, j: (i, j, 0)),
            pl.BlockSpec((1, N_MOD, d), lambda i, j: (mod_row(i), 0, 0)),
            pl.BlockSpec((1, tc, d), lambda i, j: (i, j, 0)), _resident(w_fw),
            tok_spec,
            pl.BlockSpec((1, nslab, tc, LANES), lambda i, j: (i, 0, j, 0)),
            pl.BlockSpec((1, nsub, RET_HEADS, dh, dh), lambda i, j: (i, j, 0, 0, 0)),
            tok_spec, tok_spec, tok_spec,
            _resident(tab),
            _resident(wg), _const_spec(lp.shape), _const_spec(gnw.shape), _resident(wout_bf),
            pl.BlockSpec((1, 1, lw), lambda i, j: (i, 0, 0)),
            pl.BlockSpec((1, RET_HEADS, dh, dh), lambda i, j: (i, 0, 0, 0)),
        ],
        out_specs=[
            pl.BlockSpec((1, tc, d), lambda i, j: (i, j, 0)),
            pl.BlockSpec((1, 1, lw), lambda i, j: (i, 0, 0)),
            pl.BlockSpec((1, RET_HEADS, dh, dh), lambda i, j: (i, 0, 0, 0)),
        ],
        out_shape=[
            jax.ShapeDtypeStruct((b, t, d), F32),
            jax.ShapeDtypeStruct((b, 1, lw), F32),
            jax.ShapeDtypeStruct((b, RET_HEADS, dh, dh), F32),
        ],
        scratch_shapes=[pltpu.VMEM((nslab, tc, LANES), F32),
                        pltpu.VMEM((nslab, tc, LANES), F32),
                        pltpu.VMEM((nslab, tc, LANES), F32),
                        pltpu.VMEM((nslab, 8, LANES), F32),
                        pltpu.VMEM((RET_HEADS, dh, dh), F32),
                        pltpu.VMEM((tc, wout_bf.shape[0]), BF16)],
        compiler_params=_params("arbitrary", "arbitrary"),
        name="forward_sweep",
    )(x, mod, hin, w_fw, xc, hbw, sbw, kb, ktf, vb, tab, wg, lp, gnw, wout_bf, h0, s0)


def _ffn_kernel(*refs, tf, halo, period, taps, final):
    if halo:
        xp_ref, x_ref, xn_ref = refs[:3]
        refs = refs[3:]
    else:
        x_ref = refs[0]
        refs = refs[1:]
    mod_ref, n2w_ref, wg_ref, wu_ref, wd_ref, cw_ref, cb_ref, fnw_ref, o_ref = refs
    m = mod_ref[0]
    n2wm = n2w_ref[...] * (1.0 + m[4:5])

    def norm_mod(x):
        return _rms(x) * n2wm + m[3:4]

    x = x_ref[0]
    h2_cur = norm_mod(x).astype(BF16)
    h2 = h2_cur
    if halo:
        j = pl.program_id(1)
        nj = pl.num_programs(1)
        h2 = jnp.concatenate([jnp.where(j == 0, 0.0, norm_mod(xp_ref[0])).astype(BF16), h2_cur,
                              jnp.where(j == nj - 1, 0.0, norm_mod(xn_ref[0])).astype(BF16)], axis=0)
    rows = tf + 2 * halo
    nper = rows // period
    sub = lax.broadcasted_iota(jnp.int32, (nper, 8, FFN_TILE), 1)

    def neighbour(g, step):
        r3 = pltpu.roll(g, (-step) % rows, 0).reshape(nper, period, FFN_TILE)
        if step < 0:
            fixed = [jnp.where(sub == 0, 0.0, r3[:, 0:8]), r3[:, 8:]]
        else:
            fixed = [r3[:, 0:period - 8], jnp.where(sub == 7, 0.0, r3[:, period - 8:])]
        return jnp.concatenate(fixed, axis=1).reshape(rows, FFN_TILE)

    acts = []
    ntiles = wg_ref.shape[1] // FFN_TILE
    for n0 in range(0, ntiles, TILE_GROUP):
        group = range(n0, min(n0 + TILE_GROUP, ntiles))
        cs = {n: slice(n * FFN_TILE, (n + 1) * FFN_TILE) for n in group}
        g = {n: jnp.dot(h2, wg_ref[:, cs[n]], preferred_element_type=F32) for n in group}
        up = {n: jnp.dot(h2_cur, wu_ref[:, cs[n]], preferred_element_type=F32) for n in group}
        g_left = {n: neighbour(g[n], -1) for n in group}
        g_right = {n: neighbour(g[n], 1) for n in group}
        for n in group:
            cw = cw_ref[:, cs[n]]
            conv = cb_ref[:, cs[n]]
            for dr, wrow in taps:
                lo = halo + dr * period
                conv = (conv + g_left[n][lo:lo + tf] * cw[3 * wrow:3 * wrow + 1]
                        + g[n][lo:lo + tf] * cw[3 * wrow + 1:3 * wrow + 2]
                        + g_right[n][lo:lo + tf] * cw[3 * wrow + 2:3 * wrow + 3])
            acts.append((_gelu_tanh(conv) * up[n]).astype(BF16))

    out = x + m[5:6] * jnp.dot(jnp.concatenate(acts, axis=1), wd_ref[...], preferred_element_type=F32)
    o_ref[0] = _rms(out) * fnw_ref[...] if final else out


def _ffn(x, mod, mod_row, n2w, wg, wu, wd, cw, cb, fnw, latent, final):
    b, t, d = x.shape
    if latent:
        tf, halo, period = FFN_ROWS, GRID_W, GRID_W
        taps = ((-1, 0), (0, 1), (1, 2))
        nb = tf // halo
        x_specs = [pl.BlockSpec((1, halo, d), lambda i, j: (i, jnp.maximum(j * nb - 1, 0), 0)),
                   pl.BlockSpec((1, tf, d), lambda i, j: (i, j, 0)),
                   pl.BlockSpec((1, halo, d), lambda i, j: (i, jnp.minimum((j + 1) * nb, t // halo - 1), 0))]
        x_args = (x, x, x)
    else:
        tf, halo, period = t, 0, t
        taps = ((0, 1),)
        x_specs = [pl.BlockSpec((1, tf, d), lambda i, j: (i, j, 0))]
        x_args = (x,)
    assert period & (period - 1) == 0 and t % tf == 0
    return pl.pallas_call(
        functools.partial(_ffn_kernel, tf=tf, halo=halo, period=period, taps=taps, final=final),
        grid=(b, t // tf),
        in_specs=x_specs + [
            pl.BlockSpec((1, N_MOD, d), lambda i, j: (mod_row(i), 0, 0)),
            _const_spec(n2w.shape), _resident(wg), _resident(wu), _resident(wd),
            _const_spec(cw.shape), _const_spec(cb.shape), _const_spec(fnw.shape)],
        out_specs=pl.BlockSpec((1, tf, d), lambda i, j: (i, j, 0)),
        out_shape=jax.ShapeDtypeStruct((b, t, d), F32),
        compiler_params=_params("arbitrary", "arbitrary"),
        name="ffn",
    )(*x_args, mod, n2w, wg, wu, wd, cw, cb, fnw)


def _block_diag_tiles(w):
    nb, c, _ = w.shape
    per = GATE_TILE // c
    eye = jnp.eye(per, dtype=w.dtype)
    w = w.reshape(nb // per, per, c, c)
    return jnp.einsum('mjab,jk->mjakb', w, eye).reshape(nb // per, GATE_TILE, GATE_TILE)


def _layer(x, mod, mod_row, p, tab, states, latent, final):
    b, t, d = x.shape
    tc = min(t, TIME_CHUNK)
    xc, hbw, sbw, kb, ktf, vb, hin, fin_h_bw, fin_s_bw = _backward_sweep(
        x, mod, mod_row, p['norm1_w'], p['w_bw'], tab, p['lru_conv_w'], p['lru_conv_b'], p['wg'], p['lp'],
        states[1], states[3], tc)
    x, fin_h_fw, fin_s_fw = _forward_sweep(
        x, mod, mod_row, hin, p['w_fw'], xc, hbw, sbw, kb, ktf, vb, tab, p['wg'], p['lp'],
        p['ret_gn_w'], p['w_out'], states[0], states[2], tc)
    x = _ffn(x, mod, mod_row, p['norm2_w'], p['ffn_wg'], p['ffn_wu'], p['ffn_wd'], p['ffn_cw'], p['ffn_cb'],
             p['final_norm_w'], latent, final)
    return x, (fin_h_fw, fin_h_bw, fin_s_fw, fin_s_bw)


def kernel(x_prompt, x_sample, state_lru_fw, state_lru_bw, state_ret_fw, state_ret_bw, c, c_ctx,
           norm1_w, w_mod, b_mod, w_in, lru_conv_w, lru_conv_b,
           lru_wa_fw, lru_ba_fw, lru_wx_fw, lru_bx_fw, lru_lambda_fw,
           lru_wa_bw, lru_ba_bw, lru_wx_bw, lru_bx_bw, lru_lambda_bw,
           ret_decay_fw, ret_decay_bw, ret_gn_w, w_out, norm2_w,
           ffn_w_gate, ffn_w_up, ffn_conv_w, ffn_conv_b, ffn_w_down, final_norm_w):
    depth = w_in.shape[0]
    bp = x_prompt.shape[0]
    bs, d = c.shape
    lw = lru_conv_w.shape[2]
    dh = RET_CHUNK
    rw = RET_HEADS * dh

    ctx_row = bs
    rows = -(-(bs + 1) // 8) * 8
    cvec = jnp.zeros((rows, d), F32).at[:bs].set(c).at[ctx_row].set(c_ctx)
    row_ctx = lambda i: ctx_row
    row_lat = lambda i: i

    x_p, x_s = x_prompt, x_sample
    fins = []
    for l in range(depth):
        mod = _modulation(cvec, w_mod[l], b_mod[l][None]).reshape(rows, N_MOD, d)
        tab = _decay_tables(jnp.stack([ret_decay_fw[l], ret_decay_bw[l]]).astype(F32))
        p = dict(
            norm1_w=norm1_w[l][None],
            w_bw=jnp.concatenate([w_in[l][:, 0:lw], w_in[l][:, 2 * lw + rw:2 * lw + 3 * rw]], axis=1).astype(BF16),
            w_fw=jnp.concatenate([w_in[l][:, lw:2 * lw + rw], w_in[l][:, 2 * lw + 3 * rw:]], axis=1).astype(BF16),
            lru_conv_w=lru_conv_w[l], lru_conv_b=lru_conv_b[l][None],
            wg=jnp.stack([jnp.stack([_block_diag_tiles(lru_wa_fw[l]), _block_diag_tiles(lru_wx_fw[l])]),
                          jnp.stack([_block_diag_tiles(lru_wa_bw[l]), _block_diag_tiles(lru_wx_bw[l])])]
                         ).astype(BF16),
            lp=jnp.stack([jnp.stack([lru_ba_fw[l], lru_bx_fw[l], lru_lambda_fw[l]]),
                          jnp.stack([lru_ba_bw[l], lru_bx_bw[l], lru_lambda_bw[l]])]).astype(F32),
            ret_gn_w=ret_gn_w[l][None], w_out=w_out[l].astype(BF16), norm2_w=norm2_w[l][None],
            ffn_wg=ffn_w_gate[l].astype(BF16), ffn_wu=ffn_w_up[l].astype(BF16), ffn_wd=ffn_w_down[l].astype(BF16),
            ffn_cw=ffn_conv_w[l].reshape(9, -1), ffn_cb=ffn_conv_b[l][None],
            final_norm_w=final_norm_w[None],
        )
        final = l == depth - 1
        zeros_h = jnp.zeros((bp, 1, lw), F32)
        zeros_s = jnp.zeros((bp, RET_HEADS, dh, dh), F32)
        x_p, fin = _layer(x_p, mod, row_ctx, p, tab, (zeros_h, zeros_h, zeros_s, zeros_s), False, final)
        fins.append(fin)
        init_lat = (state_lru_fw[:, l][:, None].astype(F32), state_lru_bw[:, l][:, None].astype(F32),
                    state_ret_fw[:, l].astype(F32), state_ret_bw[:, l].astype(F32))
        x_s, _ = _layer(x_s, mod, row_lat, p, tab, init_lat, True, final)

    dt = x_prompt.dtype
    new_lru_fw = jnp.concatenate([f[0] for f in fins], axis=1).astype(dt)
    new_lru_bw = jnp.concatenate([f[1] for f in fins], axis=1).astype(dt)
    new_ret_fw = jnp.stack([f[2] for f in fins], axis=1).astype(dt)
    new_ret_bw = jnp.stack([f[3] for f in fins], axis=1).astype(dt)
    return (x_p, x_s, new_lru_fw, new_lru_bw, new_ret_fw, new_ret_bw)
```

```python
import functools
import math

import jax
import jax.numpy as jnp
from jax import lax
from jax.experimental import pallas as pl
from jax.experimental.pallas import tpu as pltpu

F32 = jnp.float32
BF16 = jnp.bfloat16

N_MOD = 6
EPS = 1e-6
LRU_C = 8.0
LRU_BLOCKS = 8
LRU_CONV_WIDTH = 4
LRU_CONV_PAD_LEFT = 2
RET_HEADS = 4
RET_CHUNK = 128
GRID_W = 64
GATE_TILE = 256
FFN_TILE = 256
TILE_GROUP = 4
HALO_ROWS = 8
LEAD_ROWS = 16
LANES = 128
SEG_ROWS = 4
TIME_CHUNK = 512
FFN_ROWS = 512
VMEM_LIMIT = 56 * 1024 * 1024
def _dot(a, b):
    return jnp.dot(a.astype(BF16), b.astype(BF16), preferred_element_type=F32)


def _silu(x):
    h = 0.5 * x
    return h * jnp.tanh(h) + h


def _gelu_tanh(x):
    c = math.sqrt(2.0 / math.pi)
    return (0.5 * x) * (1.0 + jnp.tanh(x * (c + (c * 0.044715) * (x * x))))


def _softplus(x):
    return jnp.maximum(x, 0.0) + jnp.log1p(jnp.exp(-jnp.abs(x)))


def _rms(x):
    return x * lax.rsqrt(jnp.mean(x * x, axis=-1, keepdims=True) + EPS)


def _params(*sem):
    return pltpu.CompilerParams(dimension_semantics=sem, vmem_limit_bytes=VMEM_LIMIT)


def _const_spec(shape):
    nd = len(shape)
    return pl.BlockSpec(shape, lambda *_: (0,) * nd)


def _resident(a):
    return pl.BlockSpec(a.shape, lambda *_: (0,) * a.ndim, pipeline_mode=pl.Buffered(1))


def _mod_kernel(c_ref, w_ref, b_ref, o_ref):
    o_ref[...] = _dot(_silu(c_ref[...]), w_ref[...]) + b_ref[...]


def _modulation(cvec, w, b):
    rows, d = cvec.shape
    n = w.shape[1]
    tn = n // 4
    return pl.pallas_call(
        _mod_kernel,
        grid=(n // tn,),
        in_specs=[pl.BlockSpec((rows, d), lambda j: (0, 0)),
                  pl.BlockSpec((d, tn), lambda j: (0, j)),
                  pl.BlockSpec((1, tn), lambda j: (0, j))],
        out_specs=pl.BlockSpec((rows, tn), lambda j: (0, j)),
        out_shape=jax.ShapeDtypeStruct((rows, n), F32),
        compiler_params=_params("arbitrary"),
        name="modulation",
    )(cvec, w, b)


TAB_INTRA, TAB_HEAD_FW, TAB_HEAD_BW, TAB_TAIL_FW, TAB_TAIL_BW, TAB_STEP_FW, TAB_STEP_BW = range(7)


def _tab_kernel(dec_ref, tab_ref):
    c = RET_CHUNK
    row = lax.broadcasted_iota(jnp.int32, (c, c), 0).astype(F32)
    col = lax.broadcasted_iota(jnp.int32, (c, c), 1).astype(F32)
    rel = row - col
    for h in range(RET_HEADS):
        lf = -_softplus(-jnp.full((c, c), dec_ref[0, h], F32))
        lb = -_softplus(-jnp.full((c, c), dec_ref[1, h], F32))
        tab_ref[TAB_INTRA, h] = jnp.where(rel > 0, jnp.exp(lf * rel),
                                          jnp.where(rel < 0, jnp.exp(lb * (-rel)), 2.0))
        tab_ref[TAB_HEAD_FW, h] = jnp.exp(lf * (row + 1.0))
        tab_ref[TAB_HEAD_BW, h] = jnp.exp(lb * (c - row))
        tab_ref[TAB_TAIL_FW, h] = jnp.exp(lf * (c - 1.0 - row))
        tab_ref[TAB_TAIL_BW, h] = jnp.exp(lb * row)
        tab_ref[TAB_STEP_FW, h] = jnp.exp(lf * c)
        tab_ref[TAB_STEP_BW, h] = jnp.exp(lb * c)


def _decay_tables(dec):
    return pl.pallas_call(
        _tab_kernel,
        in_specs=[pl.BlockSpec(memory_space=pltpu.SMEM)],
        out_shape=jax.ShapeDtypeStruct((7, RET_HEADS, RET_CHUNK, RET_CHUNK), F32),
        name="decay_tables",
    )(dec)


def _norm_mod(x, m, nw):
    return _rms(x) * (nw * (1.0 + m[1:2])) + m[0:1]


def _lru_conv(prev, cur, nxt, cw_ref, cb_ref):
    tc = cur.shape[0]
    lead = prev.shape[0]
    ext = jnp.concatenate([prev, cur, nxt], axis=0)
    rows = ext.shape[0]
    cw = cw_ref[...]
    acc = cb_ref[...] + cur * cw[LRU_CONV_PAD_LEFT:LRU_CONV_PAD_LEFT + 1]
    for j in range(LRU_CONV_WIDTH):
        off = j - LRU_CONV_PAD_LEFT
        if off:
            acc = acc + pltpu.roll(ext, (-off) % rows, 0)[lead:lead + tc] * cw[j:j + 1]
    return acc


def _lru_gates(xc, xc_bf, wg_ref, lp_ref, direction):
    lp = lp_ref[direction]
    width = xc.shape[1]

    def gate(g):
        parts = [jnp.dot(xc_bf[:, GATE_TILE * m:GATE_TILE * (m + 1)], wg_ref[direction, g, m],
                         preferred_element_type=F32) for m in range(width // GATE_TILE)]
        return jnp.concatenate(parts, axis=1)

    tanh_r = jnp.tanh(0.5 * gate(0) + 0.5 * lp[0:1])
    tanh_i = jnp.tanh(0.5 * gate(1) + 0.5 * lp[1:2])
    half = (-0.5 * LRU_C) * _softplus(-lp[2:3])
    log_a = half * tanh_r + half
    a = jnp.exp(log_a)
    gain2 = -jnp.tanh(log_a) * (a * a + 1.0)
    gain = jnp.where(gain2 == 0.0, 0.0, gain2 * lax.rsqrt(gain2))
    u = gain * ((0.5 * tanh_i + 0.5) * xc)
    return a, u


def _store_slabs(ref, val):
    for k in range(ref.shape[0]):
        ref[k] = val[:, k * LANES:(k + 1) * LANES]


def _lru_scan(a_ref, u_ref, h_ref, h_pre, carry_ref, reverse, tc):
    nslab = a_ref.shape[0]
    blk = 8 * SEG_ROWS
    nblk = tc // blk
    row = lax.broadcasted_iota(jnp.int32, (8, LANES), 0)
    order = tuple(range(SEG_ROWS - 1, -1, -1)) if reverse else tuple(range(SEG_ROWS))
    edge = 0 if reverse else 7

    def earlier(x, d, fill):
        if reverse:
            return jnp.where(row < 8 - d, pltpu.roll(x, 8 - d, 0), fill)
        return jnp.where(row >= d, pltpu.roll(x, d, 0), fill)

    carry = [carry_ref[k] for k in range(nslab)]
    for bi in range(nblk):
        base = ((nblk - 1 - bi) if reverse else bi) * blk
        for k in range(nslab):
            idx = [pl.ds(base + j, 8, stride=SEG_ROWS) for j in range(SEG_ROWS)]
            av = [a_ref[k, i, :] for i in idx]
            uv = [u_ref[k, i, :] for i in idx]
            h, p = uv[order[0]], av[order[0]]
            for j in order[1:]:
                h = av[j] * h + uv[j]
                p = av[j] * p
            d = 1
            while d < 8:
                h = p * earlier(h, d, 0.0) + h
                p = p * earlier(p, d, 1.0)
                d *= 2
            c_in = carry[k]
            state = earlier(h, 1, 0.0) + earlier(p, 1, 1.0) * c_in
            carry[k] = jnp.broadcast_to(h[edge:edge + 1] + p[edge:edge + 1] * c_in[edge:edge + 1], (8, LANES))
            for j in order:
                state = av[j] * state + uv[j]
                h_ref[(*h_pre, k, idx[j], slice(None))] = state
    for k in range(nslab):
        carry_ref[k] = carry[k]


def _bw_kernel(xprev_ref, x_ref, mod_ref, nw_ref, w_ref, tab_ref, cw_ref, cb_ref, wg_ref, lp_ref, h0_ref, s0_ref,
               xc_ref, hbw_ref, sbw_ref, kb_ref, ktf_ref, vb_ref, hin_ref, finh_ref, fins_ref,
               a_ref, u_ref, hcar_ref, s_ref, nx_ref, *, tc, nc):
    c = pl.program_id(1)
    chunk = nc - 1 - c
    dh = RET_CHUNK
    nslab = a_ref.shape[0]
    lw = nslab * LANES

    @pl.when(c == 0)
    def _():
        for k in range(nslab):
            hcar_ref[k] = jnp.broadcast_to(h0_ref[0, :, k * LANES:(k + 1) * LANES], (8, LANES))
        s_ref[...] = s0_ref[0]
        nx_ref[...] = jnp.zeros_like(nx_ref)

    m = mod_ref[0]
    nw = nw_ref[...]
    h_cur = _norm_mod(x_ref[0], m, nw).astype(BF16)
    hin_ref[0] = h_cur
    h_ext = jnp.concatenate([_norm_mod(xprev_ref[0], m, nw).astype(BF16), h_cur], axis=0)
    lx_ext = jnp.dot(h_ext, w_ref[:, 0:lw], preferred_element_type=F32)
    kv = jnp.dot(h_cur, w_ref[:, lw:], preferred_element_type=F32)
    lead = xprev_ref.shape[1]
    lx = lx_ext[lead:]
    lx_prev = jnp.where(chunk == 0, 0.0, lx_ext[0:lead])
    xc = _lru_conv(lx_prev, lx, nx_ref[...], cw_ref, cb_ref)
    nx_ref[...] = lx[0:nx_ref.shape[0]]
    xc_ref[0] = xc

    a, u = _lru_gates(xc, xc.astype(BF16), wg_ref, lp_ref, 1)
    _store_slabs(a_ref, a)
    _store_slabs(u_ref, u)
    _lru_scan(a_ref, u_ref, hbw_ref, (0,), hcar_ref, True, tc)

    k = kv[:, 0:RET_HEADS * dh] * dh ** -0.5
    v_bf = kv[:, RET_HEADS * dh:].astype(BF16)
    kb_ref[0] = k.astype(BF16)
    vb_ref[0] = v_bf
    pairs = [(n, hd) for n in range(tc // RET_CHUNK) for hd in range(RET_HEADS)]
    rows = lambda n: slice(n * RET_CHUNK, (n + 1) * RET_CHUNK)
    cols = lambda hd: slice(hd * dh, (hd + 1) * dh)
    for n, hd in pairs:
        ktf_ref[0, rows(n), cols(hd)] = (k[rows(n), cols(hd)] * tab_ref[TAB_TAIL_FW, hd]).astype(BF16)
    kvs = {(n, hd): lax.dot_general((k[rows(n), cols(hd)] * tab_ref[TAB_TAIL_BW, hd]).astype(BF16),
                                    v_bf[rows(n), cols(hd)], (((0,), (0,)), ((), ())), preferred_element_type=F32)
           for n, hd in pairs}
    for hd in range(RET_HEADS):
        s = s_ref[hd]
        for n in reversed(range(tc // RET_CHUNK)):
            sbw_ref[0, n, hd] = s.astype(BF16)
            s = tab_ref[TAB_STEP_BW, hd] * s + kvs[n, hd]
        s_ref[hd] = s

    @pl.when(c == 0)
    def _():
        for k in range(nslab):
            finh_ref[0, :, k * LANES:(k + 1) * LANES] = hbw_ref[0, k, tc - 1:tc, :]

    @pl.when(c == nc - 1)
    def _():
        fins_ref[0] = s_ref[...]


def _backward_sweep(x, mod, mod_row, nw, w_bw, tab, cw, cb, wg, lp, h0, s0, tc):
    b, t, d = x.shape
    lw = cw.shape[1]
    nc = t // tc
    nlead = tc // LEAD_ROWS
    nsub = tc // RET_CHUNK
    nslab = lw // LANES
    dh = RET_CHUNK
    ch = lambda j: nc - 1 - j
    tok = lambda dt: jax.ShapeDtypeStruct((b, t, lw), dt)
    tok_spec = pl.BlockSpec((1, tc, lw), lambda i, j: (i, ch(j), 0))
    return pl.pallas_call(
        functools.partial(_bw_kernel, tc=tc, nc=nc),
        grid=(b, nc),
        in_specs=[
            pl.BlockSpec((1, LEAD_ROWS, d), lambda i, j: (i, jnp.maximum(ch(j) * nlead - 1, 0), 0)),
            pl.BlockSpec((1, tc, d), lambda i, j: (i, ch(j), 0)),
            pl.BlockSpec((1, N_MOD, d), lambda i, j: (mod_row(i), 0, 0)),
            _const_spec(nw.shape), _resident(w_bw), _resident(tab), _const_spec(cw.shape), _const_spec(cb.shape),
            _resident(wg), _const_spec(lp.shape),
            pl.BlockSpec((1, 1, lw), lambda i, j: (i, 0, 0)),
            pl.BlockSpec((1, RET_HEADS, dh, dh), lambda i, j: (i, 0, 0, 0)),
        ],
        out_specs=[
            tok_spec,
            pl.BlockSpec((1, nslab, tc, LANES), lambda i, j: (i, 0, ch(j), 0)),
            pl.BlockSpec((1, nsub, RET_HEADS, dh, dh), lambda i, j: (i, ch(j), 0, 0, 0)),
            tok_spec, tok_spec, tok_spec,
            pl.BlockSpec((1, tc, d), lambda i, j: (i, ch(j), 0)),
            pl.BlockSpec((1, 1, lw), lambda i, j: (i, 0, 0)),
            pl.BlockSpec((1, RET_HEADS, dh, dh), lambda i, j: (i, 0, 0, 0)),
        ],
        out_shape=[
            tok(F32),
            jax.ShapeDtypeStruct((b, nslab, t, LANES), F32),
            jax.ShapeDtypeStruct((b, t // RET_CHUNK, RET_HEADS, dh, dh), BF16),
            tok(BF16), tok(BF16), tok(BF16),
            jax.ShapeDtypeStruct((b, t, d), BF16),
            jax.ShapeDtypeStruct((b, 1, lw), F32),
            jax.ShapeDtypeStruct((b, RET_HEADS, dh, dh), F32),
        ],
        scratch_shapes=[pltpu.VMEM((nslab, tc, LANES), F32),
                        pltpu.VMEM((nslab, tc, LANES), F32),
                        pltpu.VMEM((nslab, 8, LANES), F32),
                        pltpu.VMEM((RET_HEADS, dh, dh), F32),
                        pltpu.VMEM((HALO_ROWS, lw), F32)],
        compiler_params=_params("arbitrary", "arbitrary"),
        name="backward_sweep",
    )(x, x, mod, nw, w_bw, tab, cw, cb, wg, lp, h0, s0)


def _fw_kernel(mod_ref, hin_ref, w_ref, xc_ref, hbw_ref, sbw_ref, kb_ref, ktf_ref, vb_ref,
               tab_ref, wg_ref, lp_ref, gnw_ref, wout_ref, h0_ref, s0_ref,
               o_ref, finh_ref, fins_ref, a_ref, u_ref, h_ref, hcar_ref, s_ref, y_ref, *, tc, nc):
    c = pl.program_id(1)
    dh = RET_CHUNK
    lw = xc_ref.shape[2]
    nslab = a_ref.shape[0]

    @pl.when(c == 0)
    def _():
        for k in range(nslab):
            hcar_ref[k] = jnp.broadcast_to(h0_ref[0, :, k * LANES:(k + 1) * LANES], (8, LANES))
        s_ref[...] = s0_ref[0]

    m = mod_ref[0]
    h_in = hin_ref[0]

    xc = xc_ref[0]
    a, u = _lru_gates(xc, xc.astype(BF16), wg_ref, lp_ref, 0)
    _store_slabs(a_ref, a)
    _store_slabs(u_ref, u)
    _lru_scan(a_ref, u_ref, h_ref, (), hcar_ref, False, tc)

    lru_g = jnp.dot(h_in, w_ref[:, 0:lw], preferred_element_type=F32)
    qg = jnp.dot(h_in, w_ref[:, lw:], preferred_element_type=F32)
    q_all = qg[:, 0:RET_HEADS * dh]
    ret_g = qg[:, RET_HEADS * dh:]
    for k in range(nslab):
        cols = slice(k * LANES, (k + 1) * LANES)
        y_ref[:, cols] = ((h_ref[k] + hbw_ref[0, k]) * _gelu_tanh(lru_g[:, cols])).astype(BF16)

    gnw = gnw_ref[...]
    pairs = [(n, hd) for n in range(tc // RET_CHUNK) for hd in range(RET_HEADS)]
    rows = lambda n: slice(n * RET_CHUNK, (n + 1) * RET_CHUNK)
    cols = lambda hd: slice(hd * dh, (hd + 1) * dh)
    kv = {(n, hd): lax.dot_general(ktf_ref[0, rows(n), cols(hd)], vb_ref[0, rows(n), cols(hd)],
                                   (((0,), (0,)), ((), ())), preferred_element_type=F32) for n, hd in pairs}
    state = {}
    for hd in range(RET_HEADS):
        s = s_ref[hd]
        for n in range(tc // RET_CHUNK):
            state[n, hd] = s
            s = tab_ref[TAB_STEP_FW, hd] * s + kv[n, hd]
        s_ref[hd] = s
    scores = {(n, hd): lax.dot_general(q_all[rows(n), cols(hd)].astype(BF16), kb_ref[0, rows(n), cols(hd)],
                                       (((1,), (1,)), ((), ())), preferred_element_type=F32) for n, hd in pairs}
    intra = {(n, hd): _dot(scores[n, hd] * tab_ref[TAB_INTRA, hd], vb_ref[0, rows(n), cols(hd)])
             for n, hd in pairs}
    out = {}
    for n, hd in pairs:
        q = q_all[rows(n), cols(hd)]
        q_both = jnp.concatenate([q * tab_ref[TAB_HEAD_FW, hd], q * tab_ref[TAB_HEAD_BW, hd]], axis=1)
        s_both = jnp.concatenate([state[n, hd].astype(BF16), sbw_ref[0, n, hd]], axis=0)
        out[n, hd] = intra[n, hd] + _dot(q_both, s_both)
    for n, hd in pairs:
        o = out[n, hd]
        dev = o - jnp.mean(o, axis=-1, keepdims=True)
        on = dev * lax.rsqrt(jnp.mean(dev * dev, axis=-1, keepdims=True) + EPS) * gnw[:, cols(hd)]
        y_ref[rows(n), lw + hd * dh:lw + (hd + 1) * dh] = (on * _silu(ret_g[rows(n), cols(hd)])).astype(BF16)

    o_ref[0] = m[2:3] * _dot(y_ref[...], wout_ref[...])

    @pl.when(c == 0)
    def _():
        for k in range(nslab):
            finh_ref[0, :, k * LANES:(k + 1) * LANES] = h_ref[k, 0:1, :]

    @pl.when(c == nc - 1)
    def _():
        fins_ref[0] = s_ref[...]


def _forward_sweep(mod, mod_row, hin, w_fw, xc, hbw, sbw, kb, ktf, vb, tab, wg, lp, gnw, wout_bf, h0, s0, tc):
    b, t, d = hin.shape
    lw = xc.shape[2]
    nc = t // tc
    nsub = tc // RET_CHUNK
    nslab = lw // LANES
    dh = RET_CHUNK
    tok_spec = pl.BlockSpec((1, tc, lw), lambda i, j: (i, j, 0))
    return pl.pallas_call(
        functools.partial(_fw_kernel, tc=tc, nc=nc),
        grid=(b, nc),
        in_specs=[
            pl.BlockSpec((1, N_MOD, d), lambda i, j: (mod_row(i), 0, 0)),
            pl.BlockSpec((1, tc, d), lambda i, j: (i, j, 0)), _resident(w_fw),
            tok_spec,
            pl.BlockSpec((1, nslab, tc, LANES), lambda i, j: (i, 0, j, 0)),
            pl.BlockSpec((1, nsub, RET_HEADS, dh, dh), lambda i, j: (i, j, 0, 0, 0)),
            tok_spec, tok_spec, tok_spec,
            _resident(tab),
            _resident(wg), _const_spec(lp.shape), _const_spec(gnw.shape), _resident(wout_bf),
            pl.BlockSpec((1, 1, lw), lambda i, j: (i, 0, 0)),
            pl.BlockSpec((1, RET_HEADS, dh, dh), lambda i, j: (i, 0, 0, 0)),
        ],
        out_specs=[
            pl.BlockSpec((1, tc, d), lambda i, j: (i, j, 0)),
            pl.BlockSpec((1, 1, lw), lambda i, j: (i, 0, 0)),
            pl.BlockSpec((1, RET_HEADS, dh, dh), lambda i, j: (i, 0, 0, 0)),
        ],
        out_shape=[
            jax.ShapeDtypeStruct((b, t, d), F32),
            jax.ShapeDtypeStruct((b, 1, lw), F32),
            jax.ShapeDtypeStruct((b, RET_HEADS, dh, dh), F32),
        ],
        scratch_shapes=[pltpu.VMEM((nslab, tc, LANES), F32),
                        pltpu.VMEM((nslab, tc, LANES), F32),
                        pltpu.VMEM((nslab, tc, LANES), F32),
                        pltpu.VMEM((nslab, 8, LANES), F32),
                        pltpu.VMEM((RET_HEADS, dh, dh), F32),
                        pltpu.VMEM((tc, wout_bf.shape[0]), BF16)],
        compiler_params=_params("arbitrary", "arbitrary"),
        name="forward_sweep",
    )(mod, hin, w_fw, xc, hbw, sbw, kb, ktf, vb, tab, wg, lp, gnw, wout_bf, h0, s0)


def _ffn_kernel(*refs, tf, halo, period, taps, final):
    if halo:
        xp_ref, x_ref, xn_ref, dp_ref, d_ref, dn_ref = refs[:6]
        refs = refs[6:]
    else:
        x_ref, d_ref = refs[:2]
        refs = refs[2:]
    mod_ref, n2w_ref, wg_ref, wu_ref, wd_ref, cw_ref, cb_ref, fnw_ref, o_ref = refs
    m = mod_ref[0]
    n2wm = n2w_ref[...] * (1.0 + m[4:5])

    def norm_mod(x):
        return _rms(x) * n2wm + m[3:4]

    x = x_ref[0] + d_ref[0]
    h2_cur = norm_mod(x).astype(BF16)
    h2 = h2_cur
    if halo:
        j = pl.program_id(1)
        nj = pl.num_programs(1)
        h2 = jnp.concatenate([jnp.where(j == 0, 0.0, norm_mod(xp_ref[0] + dp_ref[0])).astype(BF16), h2_cur,
                              jnp.where(j == nj - 1, 0.0, norm_mod(xn_ref[0] + dn_ref[0])).astype(BF16)], axis=0)
    rows = tf + 2 * halo
    nper = rows // period
    sub = lax.broadcasted_iota(jnp.int32, (nper, 8, FFN_TILE), 1)

    def neighbour(g, step):
        r3 = pltpu.roll(g, (-step) % rows, 0).reshape(nper, period, FFN_TILE)
        if step < 0:
            fixed = [jnp.where(sub == 0, 0.0, r3[:, 0:8]), r3[:, 8:]]
        else:
            fixed = [r3[:, 0:period - 8], jnp.where(sub == 7, 0.0, r3[:, period - 8:])]
        return jnp.concatenate(fixed, axis=1).reshape(rows, FFN_TILE)

    acts = []
    ntiles = wg_ref.shape[1] // FFN_TILE
    for n0 in range(0, ntiles, TILE_GROUP):
        group = range(n0, min(n0 + TILE_GROUP, ntiles))
        cs = {n: slice(n * FFN_TILE, (n + 1) * FFN_TILE) for n in group}
        g = {n: jnp.dot(h2, wg_ref[:, cs[n]], preferred_element_type=F32) for n in group}
        up = {n: jnp.dot(h2_cur, wu_ref[:, cs[n]], preferred_element_type=F32) for n in group}
        g_left = {n: neighbour(g[n], -1) for n in group}
        g_right = {n: neighbour(g[n], 1) for n in group}
        for n in group:
            cw = cw_ref[:, cs[n]]
            conv = cb_ref[:, cs[n]]
            for dr, wrow in taps:
                lo = halo + dr * period
                conv = (conv + g_left[n][lo:lo + tf] * cw[3 * wrow:3 * wrow + 1]
                        + g[n][lo:lo + tf] * cw[3 * wrow + 1:3 * wrow + 2]
                        + g_right[n][lo:lo + tf] * cw[3 * wrow + 2:3 * wrow + 3])
            acts.append((_gelu_tanh(conv) * up[n]).astype(BF16))

    out = x + m[5:6] * jnp.dot(jnp.concatenate(acts, axis=1), wd_ref[...], preferred_element_type=F32)
    o_ref[0] = _rms(out) * fnw_ref[...] if final else out


def _ffn(x, delta, mod, mod_row, n2w, wg, wu, wd, cw, cb, fnw, latent, final):
    b, t, d = x.shape
    if latent:
        tf, halo, period = FFN_ROWS, GRID_W, GRID_W
        taps = ((-1, 0), (0, 1), (1, 2))
        nb = tf // halo
        x_specs = [pl.BlockSpec((1, halo, d), lambda i, j: (i, jnp.maximum(j * nb - 1, 0), 0)),
                   pl.BlockSpec((1, tf, d), lambda i, j: (i, j, 0)),
                   pl.BlockSpec((1, halo, d), lambda i, j: (i, jnp.minimum((j + 1) * nb, t // halo - 1), 0))]
        x_args = (x, x, x, delta, delta, delta)
    else:
        tf, halo, period = t, 0, t
        taps = ((0, 1),)
        x_specs = [pl.BlockSpec((1, tf, d), lambda i, j: (i, j, 0))]
        x_args = (x, delta)
    assert period & (period - 1) == 0 and t % tf == 0
    return pl.pallas_call(
        functools.partial(_ffn_kernel, tf=tf, halo=halo, period=period, taps=taps, final=final),
        grid=(b, t // tf),
        in_specs=x_specs + x_specs + [
            pl.BlockSpec((1, N_MOD, d), lambda i, j: (mod_row(i), 0, 0)),
            _const_spec(n2w.shape), _resident(wg), _resident(wu), _resident(wd),
            _const_spec(cw.shape), _const_spec(cb.shape), _const_spec(fnw.shape)],
        out_specs=pl.BlockSpec((1, tf, d), lambda i, j: (i, j, 0)),
        out_shape=jax.ShapeDtypeStruct((b, t, d), F32),
        compiler_params=_params("arbitrary", "arbitrary"),
        name="ffn",
    )(*x_args, mod, n2w, wg, wu, wd, cw, cb, fnw)


def _block_diag_tiles(w):
    nb, c, _ = w.shape
    per = GATE_TILE // c
    eye = jnp.eye(per, dtype=w.dtype)
    w = w.reshape(nb // per, per, c, c)
    return jnp.einsum('mjab,jk->mjakb', w, eye).reshape(nb // per, GATE_TILE, GATE_TILE)


def _layer(x, mod, mod_row, p, tab, states, latent, final):
    b, t, d = x.shape
    tc = min(t, TIME_CHUNK)
    xc, hbw, sbw, kb, ktf, vb, hin, fin_h_bw, fin_s_bw = _backward_sweep(
        x, mod, mod_row, p['norm1_w'], p['w_bw'], tab, p['lru_conv_w'], p['lru_conv_b'], p['wg'], p['lp'],
        states[1], states[3], tc)
    delta, fin_h_fw, fin_s_fw = _forward_sweep(
        mod, mod_row, hin, p['w_fw'], xc, hbw, sbw, kb, ktf, vb, tab, p['wg'], p['lp'],
        p['ret_gn_w'], p['w_out'], states[0], states[2], tc)
    x = _ffn(x, delta, mod, mod_row, p['norm2_w'], p['ffn_wg'], p['ffn_wu'], p['ffn_wd'], p['ffn_cw'], p['ffn_cb'],
             p['final_norm_w'], latent, final)
    return x, (fin_h_fw, fin_h_bw, fin_s_fw, fin_s_bw)


def kernel(x_prompt, x_sample, state_lru_fw, state_lru_bw, state_ret_fw, state_ret_bw, c, c_ctx,
           norm1_w, w_mod, b_mod, w_in, lru_conv_w, lru_conv_b,
           lru_wa_fw, lru_ba_fw, lru_wx_fw, lru_bx_fw, lru_lambda_fw,
           lru_wa_bw, lru_ba_bw, lru_wx_bw, lru_bx_bw, lru_lambda_bw,
           ret_decay_fw, ret_decay_bw, ret_gn_w, w_out, norm2_w,
           ffn_w_gate, ffn_w_up, ffn_conv_w, ffn_conv_b, ffn_w_down, final_norm_w):
    depth = w_in.shape[0]
    bp = x_prompt.shape[0]
    bs, d = c.shape
    lw = lru_conv_w.shape[2]
    dh = RET_CHUNK
    rw = RET_HEADS * dh

    ctx_row = bs
    rows = -(-(bs + 1) // 8) * 8
    cvec = jnp.zeros((rows, d), F32).at[:bs].set(c).at[ctx_row].set(c_ctx)
    row_ctx = lambda i: ctx_row
    row_lat = lambda i: i

    x_p, x_s = x_prompt, x_sample
    fins = []
    for l in range(depth):
        mod = _modulation(cvec, w_mod[l], b_mod[l][None]).reshape(rows, N_MOD, d)
        tab = _decay_tables(jnp.stack([ret_decay_fw[l], ret_decay_bw[l]]).astype(F32))
        p = dict(
            norm1_w=norm1_w[l][None],
            w_bw=jnp.concatenate([w_in[l][:, 0:lw], w_in[l][:, 2 * lw + rw:2 * lw + 3 * rw]], axis=1).astype(BF16),
            w_fw=jnp.concatenate([w_in[l][:, lw:2 * lw + rw], w_in[l][:, 2 * lw + 3 * rw:]], axis=1).astype(BF16),
            lru_conv_w=lru_conv_w[l], lru_conv_b=lru_conv_b[l][None],
            wg=jnp.stack([jnp.stack([_block_diag_tiles(lru_wa_fw[l]), _block_diag_tiles(lru_wx_fw[l])]),
                          jnp.stack([_block_diag_tiles(lru_wa_bw[l]), _block_diag_tiles(lru_wx_bw[l])])]
                         ).astype(BF16),
            lp=jnp.stack([jnp.stack([lru_ba_fw[l], lru_bx_fw[l], lru_lambda_fw[l]]),
                          jnp.stack([lru_ba_bw[l], lru_bx_bw[l], lru_lambda_bw[l]])]).astype(F32),
            ret_gn_w=ret_gn_w[l][None], w_out=w_out[l].astype(BF16), norm2_w=norm2_w[l][None],
            ffn_wg=ffn_w_gate[l].astype(BF16), ffn_wu=ffn_w_up[l].astype(BF16), ffn_wd=ffn_w_down[l].astype(BF16),
            ffn_cw=ffn_conv_w[l].reshape(9, -1), ffn_cb=ffn_conv_b[l][None],
            final_norm_w=final_norm_w[None],
        )
        final = l == depth - 1
        zeros_h = jnp.zeros((bp, 1, lw), F32)
        zeros_s = jnp.zeros((bp, RET_HEADS, dh, dh), F32)
        x_p, fin = _layer(x_p, mod, row_ctx, p, tab, (zeros_h, zeros_h, zeros_s, zeros_s), False, final)
        fins.append(fin)
        init_lat = (state_lru_fw[:, l][:, None].astype(F32), state_lru_bw[:, l][:, None].astype(F32),
                    state_ret_fw[:, l].astype(F32), state_ret_bw[:, l].astype(F32))
        x_s, _ = _layer(x_s, mod, row_lat, p, tab, init_lat, True, final)

    dt = x_prompt.dtype
    new_lru_fw = jnp.concatenate([f[0] for f in fins], axis=1).astype(dt)
    new_lru_bw = jnp.concatenate([f[1] for f in fins], axis=1).astype(dt)
    new_ret_fw = jnp.stack([f[2] for f in fins], axis=1).astype(dt)
    new_ret_bw = jnp.stack([f[3] for f in fins], axis=1).astype(dt)
    return (x_p, x_s, new_lru_fw, new_lru_bw, new_ret_fw, new_ret_bw)
```

```python
import functools
import math

import jax
import jax.numpy as jnp
from jax import lax
from jax.experimental import pallas as pl
from jax.experimental.pallas import tpu as pltpu

F32 = jnp.float32
BF16 = jnp.bfloat16

N_MOD = 6
EPS = 1e-6
LRU_C = 8.0
LRU_BLOCKS = 8
LRU_CONV_WIDTH = 4
LRU_CONV_PAD_LEFT = 2
RET_HEADS = 4
RET_CHUNK = 128
GRID_W = 64
GATE_TILE = 256
FFN_TILE = 256
TILE_GROUP = 4
HALO_ROWS = 8
LEAD_ROWS = 16
LANES = 128
SEG_ROWS = 4
TIME_CHUNK = 512
FFN_ROWS = 512
VMEM_LIMIT = 56 * 1024 * 1024
def _dot(a, b):
    return jnp.dot(a.astype(BF16), b.astype(BF16), preferred_element_type=F32)


def _silu(x):
    h = 0.5 * x
    return h * jnp.tanh(h) + h


def _gelu_tanh(x):
    c = math.sqrt(2.0 / math.pi)
    return (0.5 * x) * (1.0 + jnp.tanh(x * (c + (c * 0.044715) * (x * x))))


def _softplus(x):
    return jnp.maximum(x, 0.0) + jnp.log1p(jnp.exp(-jnp.abs(x)))


def _rms(x):
    return x * lax.rsqrt(jnp.mean(x * x, axis=-1, keepdims=True) + EPS)


def _params(*sem):
    return pltpu.CompilerParams(dimension_semantics=sem, vmem_limit_bytes=VMEM_LIMIT)


def _const_spec(shape):
    nd = len(shape)
    return pl.BlockSpec(shape, lambda *_: (0,) * nd)


def _resident(a):
    return pl.BlockSpec(a.shape, lambda *_: (0,) * a.ndim, pipeline_mode=pl.Buffered(1))


def _mod_kernel(c_ref, w_ref, b_ref, o_ref):
    o_ref[...] = _dot(_silu(c_ref[...]), w_ref[...]) + b_ref[...]


def _modulation(cvec, w, b):
    rows, d = cvec.shape
    n = w.shape[1]
    tn = n // 4
    return pl.pallas_call(
        _mod_kernel,
        grid=(n // tn,),
        in_specs=[pl.BlockSpec((rows, d), lambda j: (0, 0)),
                  pl.BlockSpec((d, tn), lambda j: (0, j)),
                  pl.BlockSpec((1, tn), lambda j: (0, j))],
        out_specs=pl.BlockSpec((rows, tn), lambda j: (0, j)),
        out_shape=jax.ShapeDtypeStruct((rows, n), F32),
        compiler_params=_params("arbitrary"),
        name="modulation",
    )(cvec, w, b)


TAB_INTRA, TAB_HEAD_FW, TAB_HEAD_BW, TAB_TAIL_FW, TAB_TAIL_BW, TAB_STEP_FW, TAB_STEP_BW = range(7)


def _tab_kernel(dec_ref, tab_ref):
    c = RET_CHUNK
    row = lax.broadcasted_iota(jnp.int32, (c, c), 0).astype(F32)
    col = lax.broadcasted_iota(jnp.int32, (c, c), 1).astype(F32)
    rel = row - col
    for h in range(RET_HEADS):
        lf = -_softplus(-jnp.full((c, c), dec_ref[0, h], F32))
        lb = -_softplus(-jnp.full((c, c), dec_ref[1, h], F32))
        tab_ref[TAB_INTRA, h] = jnp.where(rel > 0, jnp.exp(lf * rel),
                                          jnp.where(rel < 0, jnp.exp(lb * (-rel)), 2.0))
        tab_ref[TAB_HEAD_FW, h] = jnp.exp(lf * (row + 1.0))
        tab_ref[TAB_HEAD_BW, h] = jnp.exp(lb * (c - row))
        tab_ref[TAB_TAIL_FW, h] = jnp.exp(lf * (c - 1.0 - row))
        tab_ref[TAB_TAIL_BW, h] = jnp.exp(lb * row)
        tab_ref[TAB_STEP_FW, h] = jnp.exp(lf * c)
        tab_ref[TAB_STEP_BW, h] = jnp.exp(lb * c)


def _decay_tables(dec):
    return pl.pallas_call(
        _tab_kernel,
        in_specs=[pl.BlockSpec(memory_space=pltpu.SMEM)],
        out_shape=jax.ShapeDtypeStruct((7, RET_HEADS, RET_CHUNK, RET_CHUNK), F32),
        name="decay_tables",
    )(dec)


def _norm_mod(x, m, nw):
    return _rms(x) * (nw * (1.0 + m[1:2])) + m[0:1]


def _lru_conv(prev, cur, nxt, cw_ref, cb_ref):
    tc = cur.shape[0]
    lead = prev.shape[0]
    ext = jnp.concatenate([prev, cur, nxt], axis=0)
    rows = ext.shape[0]
    cw = cw_ref[...]
    acc = cb_ref[...] + cur * cw[LRU_CONV_PAD_LEFT:LRU_CONV_PAD_LEFT + 1]
    for j in range(LRU_CONV_WIDTH):
        off = j - LRU_CONV_PAD_LEFT
        if off:
            acc = acc + pltpu.roll(ext, (-off) % rows, 0)[lead:lead + tc] * cw[j:j + 1]
    return acc


def _lru_gates(xc, xc_bf, wg_ref, lp_ref, direction):
    lp = lp_ref[direction]
    width = xc.shape[1]

    def gate(g):
        parts = [jnp.dot(xc_bf[:, GATE_TILE * m:GATE_TILE * (m + 1)], wg_ref[direction, g, m],
                         preferred_element_type=F32) for m in range(width // GATE_TILE)]
        return jnp.concatenate(parts, axis=1)

    tanh_r = jnp.tanh(0.5 * gate(0) + 0.5 * lp[0:1])
    tanh_i = jnp.tanh(0.5 * gate(1) + 0.5 * lp[1:2])
    half = (-0.5 * LRU_C) * _softplus(-lp[2:3])
    log_a = half * tanh_r + half
    a = jnp.exp(log_a)
    gain2 = -jnp.tanh(log_a) * (a * a + 1.0)
    gain = jnp.where(gain2 == 0.0, 0.0, gain2 * lax.rsqrt(gain2))
    u = gain * ((0.5 * tanh_i + 0.5) * xc)
    return a, u


def _store_slabs(ref, val):
    for k in range(ref.shape[0]):
        ref[k] = val[:, k * LANES:(k + 1) * LANES]


def _lru_scan(a_ref, u_ref, h_ref, h_pre, carry_ref, reverse, tc):
    nslab = a_ref.shape[0]
    blk = 8 * SEG_ROWS
    nblk = tc // blk
    row = lax.broadcasted_iota(jnp.int32, (8, LANES), 0)
    order = tuple(range(SEG_ROWS - 1, -1, -1)) if reverse else tuple(range(SEG_ROWS))
    edge = 0 if reverse else 7

    def earlier(x, d, fill):
        if reverse:
            return jnp.where(row < 8 - d, pltpu.roll(x, 8 - d, 0), fill)
        return jnp.where(row >= d, pltpu.roll(x, d, 0), fill)

    carry = [carry_ref[k] for k in range(nslab)]
    for bi in range(nblk):
        base = ((nblk - 1 - bi) if reverse else bi) * blk
        for k in range(nslab):
            idx = [pl.ds(base + j, 8, stride=SEG_ROWS) for j in range(SEG_ROWS)]
            av = [a_ref[k, i, :] for i in idx]
            uv = [u_ref[k, i, :] for i in idx]
            h, p = uv[order[0]], av[order[0]]
            for j in order[1:]:
                h = av[j] * h + uv[j]
                p = av[j] * p
            d = 1
            while d < 8:
                h = p * earlier(h, d, 0.0) + h
                p = p * earlier(p, d, 1.0)
                d *= 2
            c_in = carry[k]
            state = earlier(h, 1, 0.0) + earlier(p, 1, 1.0) * c_in
            carry[k] = jnp.broadcast_to(h[edge:edge + 1] + p[edge:edge + 1] * c_in[edge:edge + 1], (8, LANES))
            for j in order:
                state = av[j] * state + uv[j]
                h_ref[(*h_pre, k, idx[j], slice(None))] = state
    for k in range(nslab):
        carry_ref[k] = carry[k]


def _bw_kernel(xprev_ref, x_ref, mod_ref, nw_ref, w_ref, tab_ref, cw_ref, cb_ref, wg_ref, lp_ref, h0_ref, s0_ref,
               xc_ref, hbw_ref, sbw_ref, kb_ref, ktf_ref, vb_ref, hin_ref, finh_ref, fins_ref,
               a_ref, u_ref, hcar_ref, s_ref, nx_ref, *, tc, nc):
    c = pl.program_id(1)
    chunk = nc - 1 - c
    dh = RET_CHUNK
    nslab = a_ref.shape[0]
    lw = nslab * LANES

    @pl.when(c == 0)
    def _():
        for k in range(nslab):
            hcar_ref[k] = jnp.broadcast_to(h0_ref[0, :, k * LANES:(k + 1) * LANES], (8, LANES))
        s_ref[...] = s0_ref[0]
        nx_ref[...] = jnp.zeros_like(nx_ref)

    m = mod_ref[0]
    nw = nw_ref[...]
    h_cur = _norm_mod(x_ref[0], m, nw).astype(BF16)
    hin_ref[0] = h_cur
    h_ext = jnp.concatenate([_norm_mod(xprev_ref[0], m, nw).astype(BF16), h_cur], axis=0)
    lx_ext = jnp.dot(h_ext, w_ref[:, 0:lw], preferred_element_type=F32)
    kv = jnp.dot(h_cur, w_ref[:, lw:], preferred_element_type=F32)
    lead = xprev_ref.shape[1]
    lx = lx_ext[lead:]
    lx_prev = jnp.where(chunk == 0, 0.0, lx_ext[0:lead])
    xc = _lru_conv(lx_prev, lx, nx_ref[...], cw_ref, cb_ref)
    nx_ref[...] = lx[0:nx_ref.shape[0]]
    xc_ref[0] = xc

    a, u = _lru_gates(xc, xc.astype(BF16), wg_ref, lp_ref, 1)
    _store_slabs(a_ref, a)
    _store_slabs(u_ref, u)
    _lru_scan(a_ref, u_ref, hbw_ref, (0,), hcar_ref, True, tc)

    k = kv[:, 0:RET_HEADS * dh] * dh ** -0.5
    v_bf = kv[:, RET_HEADS * dh:].astype(BF16)
    vb_ref[0] = v_bf
    pairs = [(n, hd) for n in range(tc // RET_CHUNK) for hd in range(RET_HEADS)]
    rows = lambda n: slice(n * RET_CHUNK, (n + 1) * RET_CHUNK)
    cols = lambda hd: slice(hd * dh, (hd + 1) * dh)
    for n, hd in pairs:
        k_nh = k[rows(n), cols(hd)]
        kb_ref[0, n, hd] = k_nh.T.astype(BF16)
        ktf_ref[0, n, hd] = (k_nh * tab_ref[TAB_TAIL_FW, hd]).T.astype(BF16)
    kvs = {(n, hd): lax.dot_general((k[rows(n), cols(hd)] * tab_ref[TAB_TAIL_BW, hd]).astype(BF16),
                                    v_bf[rows(n), cols(hd)], (((0,), (0,)), ((), ())), preferred_element_type=F32)
           for n, hd in pairs}
    for hd in range(RET_HEADS):
        s = s_ref[hd]
        for n in reversed(range(tc // RET_CHUNK)):
            sbw_ref[0, n, hd] = s.astype(BF16)
            s = tab_ref[TAB_STEP_BW, hd] * s + kvs[n, hd]
        s_ref[hd] = s

    @pl.when(c == 0)
    def _():
        for k in range(nslab):
            finh_ref[0, :, k * LANES:(k + 1) * LANES] = hbw_ref[0, k, tc - 1:tc, :]

    @pl.when(c == nc - 1)
    def _():
        fins_ref[0] = s_ref[...]


def _backward_sweep(x, mod, mod_row, nw, w_bw, tab, cw, cb, wg, lp, h0, s0, tc):
    b, t, d = x.shape
    lw = cw.shape[1]
    nc = t // tc
    nlead = tc // LEAD_ROWS
    nsub = tc // RET_CHUNK
    nslab = lw // LANES
    dh = RET_CHUNK
    ch = lambda j: nc - 1 - j
    tok = lambda dt: jax.ShapeDtypeStruct((b, t, lw), dt)
    tok_spec = pl.BlockSpec((1, tc, lw), lambda i, j: (i, ch(j), 0))
    sub = jax.ShapeDtypeStruct((b, t // RET_CHUNK, RET_HEADS, dh, dh), BF16)
    sub_spec = pl.BlockSpec((1, nsub, RET_HEADS, dh, dh), lambda i, j: (i, ch(j), 0, 0, 0))
    return pl.pallas_call(
        functools.partial(_bw_kernel, tc=tc, nc=nc),
        grid=(b, nc),
        in_specs=[
            pl.BlockSpec((1, LEAD_ROWS, d), lambda i, j: (i, jnp.maximum(ch(j) * nlead - 1, 0), 0)),
            pl.BlockSpec((1, tc, d), lambda i, j: (i, ch(j), 0)),
            pl.BlockSpec((1, N_MOD, d), lambda i, j: (mod_row(i), 0, 0)),
            _const_spec(nw.shape), _resident(w_bw), _resident(tab), _const_spec(cw.shape), _const_spec(cb.shape),
            _resident(wg), _const_spec(lp.shape),
            pl.BlockSpec((1, 1, lw), lambda i, j: (i, 0, 0)),
            pl.BlockSpec((1, RET_HEADS, dh, dh), lambda i, j: (i, 0, 0, 0)),
        ],
        out_specs=[
            tok_spec,
            pl.BlockSpec((1, nslab, tc, LANES), lambda i, j: (i, 0, ch(j), 0)),
            sub_spec, sub_spec, sub_spec, tok_spec,
            pl.BlockSpec((1, tc, d), lambda i, j: (i, ch(j), 0)),
            pl.BlockSpec((1, 1, lw), lambda i, j: (i, 0, 0)),
            pl.BlockSpec((1, RET_HEADS, dh, dh), lambda i, j: (i, 0, 0, 0)),
        ],
        out_shape=[
            tok(F32),
            jax.ShapeDtypeStruct((b, nslab, t, LANES), F32),
            sub, sub, sub, tok(BF16),
            jax.ShapeDtypeStruct((b, t, d), BF16),
            jax.ShapeDtypeStruct((b, 1, lw), F32),
            jax.ShapeDtypeStruct((b, RET_HEADS, dh, dh), F32),
        ],
        scratch_shapes=[pltpu.VMEM((nslab, tc, LANES), F32),
                        pltpu.VMEM((nslab, tc, LANES), F32),
                        pltpu.VMEM((nslab, 8, LANES), F32),
                        pltpu.VMEM((RET_HEADS, dh, dh), F32),
                        pltpu.VMEM((HALO_ROWS, lw), F32)],
        compiler_params=_params("arbitrary", "arbitrary"),
        name="backward_sweep",
    )(x, x, mod, nw, w_bw, tab, cw, cb, wg, lp, h0, s0)


def _fw_kernel(x_ref, mod_ref, hin_ref, w_ref, xc_ref, hbw_ref, sbw_ref, kb_ref, ktf_ref, vb_ref,
               tab_ref, wg_ref, lp_ref, gnw_ref, wout_ref, h0_ref, s0_ref,
               o_ref, finh_ref, fins_ref, a_ref, u_ref, h_ref, hcar_ref, s_ref, y_ref, *, tc, nc):
    c = pl.program_id(1)
    dh = RET_CHUNK
    lw = xc_ref.shape[2]
    nslab = a_ref.shape[0]

    @pl.when(c == 0)
    def _():
        for k in range(nslab):
            hcar_ref[k] = jnp.broadcast_to(h0_ref[0, :, k * LANES:(k + 1) * LANES], (8, LANES))
        s_ref[...] = s0_ref[0]

    m = mod_ref[0]
    h_in = hin_ref[0]

    xc = xc_ref[0]
    a, u = _lru_gates(xc, xc.astype(BF16), wg_ref, lp_ref, 0)
    _store_slabs(a_ref, a)
    _store_slabs(u_ref, u)
    _lru_scan(a_ref, u_ref, h_ref, (), hcar_ref, False, tc)

    lru_g = jnp.dot(h_in, w_ref[:, 0:lw], preferred_element_type=F32)
    qg = jnp.dot(h_in, w_ref[:, lw:], preferred_element_type=F32)
    q_all = qg[:, 0:RET_HEADS * dh]
    ret_g = qg[:, RET_HEADS * dh:]
    for k in range(nslab):
        cols = slice(k * LANES, (k + 1) * LANES)
        y_ref[:, cols] = ((h_ref[k] + hbw_ref[0, k]) * _gelu_tanh(lru_g[:, cols])).astype(BF16)

    gnw = gnw_ref[...]
    pairs = [(n, hd) for n in range(tc // RET_CHUNK) for hd in range(RET_HEADS)]
    rows = lambda n: slice(n * RET_CHUNK, (n + 1) * RET_CHUNK)
    cols = lambda hd: slice(hd * dh, (hd + 1) * dh)
    kv = {(n, hd): jnp.dot(ktf_ref[0, n, hd], vb_ref[0, rows(n), cols(hd)], preferred_element_type=F32)
          for n, hd in pairs}
    state = {}
    for hd in range(RET_HEADS):
        s = s_ref[hd]
        for n in range(tc // RET_CHUNK):
            state[n, hd] = s
            s = tab_ref[TAB_STEP_FW, hd] * s + kv[n, hd]
        s_ref[hd] = s
    scores = {(n, hd): jnp.dot(q_all[rows(n), cols(hd)].astype(BF16), kb_ref[0, n, hd],
                               preferred_element_type=F32) for n, hd in pairs}
    intra = {(n, hd): _dot(scores[n, hd] * tab_ref[TAB_INTRA, hd], vb_ref[0, rows(n), cols(hd)])
             for n, hd in pairs}
    out = {}
    for n, hd in pairs:
        q = q_all[rows(n), cols(hd)]
        q_both = jnp.concatenate([q * tab_ref[TAB_HEAD_FW, hd], q * tab_ref[TAB_HEAD_BW, hd]], axis=1)
        s_both = jnp.concatenate([state[n, hd].astype(BF16), sbw_ref[0, n, hd]], axis=0)
        out[n, hd] = intra[n, hd] + _dot(q_both, s_both)
    for n, hd in pairs:
        o = out[n, hd]
        dev = o - jnp.mean(o, axis=-1, keepdims=True)
        on = dev * lax.rsqrt(jnp.mean(dev * dev, axis=-1, keepdims=True) + EPS) * gnw[:, cols(hd)]
        y_ref[rows(n), lw + hd * dh:lw + (hd + 1) * dh] = (on * _silu(ret_g[rows(n), cols(hd)])).astype(BF16)

    o_ref[0] = x_ref[0] + m[2:3] * _dot(y_ref[...], wout_ref[...])

    @pl.when(c == 0)
    def _():
        for k in range(nslab):
            finh_ref[0, :, k * LANES:(k + 1) * LANES] = h_ref[k, 0:1, :]

    @pl.when(c == nc - 1)
    def _():
        fins_ref[0] = s_ref[...]


def _forward_sweep(x, mod, mod_row, hin, w_fw, xc, hbw, sbw, kb, ktf, vb, tab, wg, lp, gnw, wout_bf, h0, s0, tc):
    b, t, d = x.shape
    lw = xc.shape[2]
    nc = t // tc
    nsub = tc // RET_CHUNK
    nslab = lw // LANES
    dh = RET_CHUNK
    tok_spec = pl.BlockSpec((1, tc, lw), lambda i, j: (i, j, 0))
    sub_spec = pl.BlockSpec((1, nsub, RET_HEADS, dh, dh), lambda i, j: (i, j, 0, 0, 0))
    return pl.pallas_call(
        functools.partial(_fw_kernel, tc=tc, nc=nc),
        grid=(b, nc),
        in_specs=[
            pl.BlockSpec((1, tc, d), lambda i, j: (i, j, 0)),
            pl.BlockSpec((1, N_MOD, d), lambda i, j: (mod_row(i), 0, 0)),
            pl.BlockSpec((1, tc, d), lambda i, j: (i, j, 0)), _resident(w_fw),
            tok_spec,
            pl.BlockSpec((1, nslab, tc, LANES), lambda i, j: (i, 0, j, 0)),
            sub_spec, sub_spec, sub_spec, tok_spec,
            _resident(tab),
            _resident(wg), _const_spec(lp.shape), _const_spec(gnw.shape), _resident(wout_bf),
            pl.BlockSpec((1, 1, lw), lambda i, j: (i, 0, 0)),
            pl.BlockSpec((1, RET_HEADS, dh, dh), lambda i, j: (i, 0, 0, 0)),
        ],
        out_specs=[
            pl.BlockSpec((1, tc, d), lambda i, j: (i, j, 0)),
            pl.BlockSpec((1, 1, lw), lambda i, j: (i, 0, 0)),
            pl.BlockSpec((1, RET_HEADS, dh, dh), lambda i, j: (i, 0, 0, 0)),
        ],
        out_shape=[
            jax.ShapeDtypeStruct((b, t, d), F32),
            jax.ShapeDtypeStruct((b, 1, lw), F32),
            jax.ShapeDtypeStruct((b, RET_HEADS, dh, dh), F32),
        ],
        scratch_shapes=[pltpu.VMEM((nslab, tc, LANES), F32),
                        pltpu.VMEM((nslab, tc, LANES), F32),
                        pltpu.VMEM((nslab, tc, LANES), F32),
                        pltpu.VMEM((nslab, 8, LANES), F32),
                        pltpu.VMEM((RET_HEADS, dh, dh), F32),
                        pltpu.VMEM((tc, wout_bf.shape[0]), BF16)],
        compiler_params=_params("arbitrary", "arbitrary"),
        name="forward_sweep",
    )(x, mod, hin, w_fw, xc, hbw, sbw, kb, ktf, vb, tab, wg, lp, gnw, wout_bf, h0, s0)


def _ffn_kernel(*refs, tf, halo, period, taps, final):
    if halo:
        xp_ref, x_ref, xn_ref = refs[:3]
        refs = refs[3:]
    else:
        x_ref = refs[0]
        refs = refs[1:]
    mod_ref, n2w_ref, wg_ref, wu_ref, wd_ref, cw_ref, cb_ref, fnw_ref, o_ref = refs
    m = mod_ref[0]
    n2wm = n2w_ref[...] * (1.0 + m[4:5])

    def norm_mod(x):
        return _rms(x) * n2wm + m[3:4]

    x = x_ref[0]
    h2_cur = norm_mod(x).astype(BF16)
    h2 = h2_cur
    if halo:
        j = pl.program_id(1)
        nj = pl.num_programs(1)
        h2 = jnp.concatenate([jnp.where(j == 0, 0.0, norm_mod(xp_ref[0])).astype(BF16), h2_cur,
                              jnp.where(j == nj - 1, 0.0, norm_mod(xn_ref[0])).astype(BF16)], axis=0)
    rows = tf + 2 * halo
    nper = rows // period
    sub = lax.broadcasted_iota(jnp.int32, (nper, 8, FFN_TILE), 1)

    def neighbour(g, step):
        r3 = pltpu.roll(g, (-step) % rows, 0).reshape(nper, period, FFN_TILE)
        if step < 0:
            fixed = [jnp.where(sub == 0, 0.0, r3[:, 0:8]), r3[:, 8:]]
        else:
            fixed = [r3[:, 0:period - 8], jnp.where(sub == 7, 0.0, r3[:, period - 8:])]
        return jnp.concatenate(fixed, axis=1).reshape(rows, FFN_TILE)

    acts = []
    ntiles = wg_ref.shape[1] // FFN_TILE
    for n0 in range(0, ntiles, TILE_GROUP):
        group = range(n0, min(n0 + TILE_GROUP, ntiles))
        cs = {n: slice(n * FFN_TILE, (n + 1) * FFN_TILE) for n in group}
        g = {n: jnp.dot(h2, wg_ref[:, cs[n]], preferred_element_type=F32) for n in group}
        up = {n: jnp.dot(h2_cur, wu_ref[:, cs[n]], preferred_element_type=F32) for n in group}
        g_left = {n: neighbour(g[n], -1) for n in group}
        g_right = {n: neighbour(g[n], 1) for n in group}
        for n in group:
            cw = cw_ref[:, cs[n]]
            conv = cb_ref[:, cs[n]]
            for dr, wrow in taps:
                lo = halo + dr * period
                conv = (conv + g_left[n][lo:lo + tf] * cw[3 * wrow:3 * wrow + 1]
                        + g[n][lo:lo + tf] * cw[3 * wrow + 1:3 * wrow + 2]
                        + g_right[n][lo:lo + tf] * cw[3 * wrow + 2:3 * wrow + 3])
            acts.append((_gelu_tanh(conv) * up[n]).astype(BF16))

    out = x + m[5:6] * jnp.dot(jnp.concatenate(acts, axis=1), wd_ref[...], preferred_element_type=F32)
    o_ref[0] = _rms(out) * fnw_ref[...] if final else out


def _ffn(x, mod, mod_row, n2w, wg, wu, wd, cw, cb, fnw, latent, final):
    b, t, d = x.shape
    if latent:
        tf, halo, period = FFN_ROWS, GRID_W, GRID_W
        taps = ((-1, 0), (0, 1), (1, 2))
        nb = tf // halo
        x_specs = [pl.BlockSpec((1, halo, d), lambda i, j: (i, jnp.maximum(j * nb - 1, 0), 0)),
                   pl.BlockSpec((1, tf, d), lambda i, j: (i, j, 0)),
                   pl.BlockSpec((1, halo, d), lambda i, j: (i, jnp.minimum((j + 1) * nb, t // halo - 1), 0))]
        x_args = (x, x, x)
    else:
        tf, halo, period = t, 0, t
        taps = ((0, 1),)
        x_specs = [pl.BlockSpec((1, tf, d), lambda i, j: (i, j, 0))]
        x_args = (x,)
    assert period & (period - 1) == 0 and t % tf == 0
    return pl.pallas_call(
        functools.partial(_ffn_kernel, tf=tf, halo=halo, period=period, taps=taps, final=final),
        grid=(b, t // tf),
        in_specs=x_specs + [
            pl.BlockSpec((1, N_MOD, d), lambda i, j: (mod_row(i), 0, 0)),
            _const_spec(n2w.shape), _resident(wg), _resident(wu), _resident(wd),
            _const_spec(cw.shape), _const_spec(cb.shape), _const_spec(fnw.shape)],
        out_specs=pl.BlockSpec((1, tf, d), lambda i, j: (i, j, 0)),
        out_shape=jax.ShapeDtypeStruct((b, t, d), F32),
        compiler_params=_params("arbitrary", "arbitrary"),
        name="ffn",
    )(*x_args, mod, n2w, wg, wu, wd, cw, cb, fnw)


def _block_diag_tiles(w):
    nb, c, _ = w.shape
    per = GATE_TILE // c
    eye = jnp.eye(per, dtype=w.dtype)
    w = w.reshape(nb // per, per, c, c)
    return jnp.einsum('mjab,jk->mjakb', w, eye).reshape(nb // per, GATE_TILE, GATE_TILE)


def _layer(x, mod, mod_row, p, tab, states, latent, final):
    b, t, d = x.shape
    tc = min(t, TIME_CHUNK)
    xc, hbw, sbw, kb, ktf, vb, hin, fin_h_bw, fin_s_bw = _backward_sweep(
        x, mod, mod_row, p['norm1_w'], p['w_bw'], tab, p['lru_conv_w'], p['lru_conv_b'], p['wg'], p['lp'],
        states[1], states[3], tc)
    x, fin_h_fw, fin_s_fw = _forward_sweep(
        x, mod, mod_row, hin, p['w_fw'], xc, hbw, sbw, kb, ktf, vb, tab, p['wg'], p['lp'],
        p['ret_gn_w'], p['w_out'], states[0], states[2], tc)
    x = _ffn(x, mod, mod_row, p['norm2_w'], p['ffn_wg'], p['ffn_wu'], p['ffn_wd'], p['ffn_cw'], p['ffn_cb'],
             p['final_norm_w'], latent, final)
    return x, (fin_h_fw, fin_h_bw, fin_s_fw, fin_s_bw)


def kernel(x_prompt, x_sample, state_lru_fw, state_lru_bw, state_ret_fw, state_ret_bw, c, c_ctx,
           norm1_w, w_mod, b_mod, w_in, lru_conv_w, lru_conv_b,
           lru_wa_fw, lru_ba_fw, lru_wx_fw, lru_bx_fw, lru_lambda_fw,
           lru_wa_bw, lru_ba_bw, lru_wx_bw, lru_bx_bw, lru_lambda_bw,
           ret_decay_fw, ret_decay_bw, ret_gn_w, w_out, norm2_w,
           ffn_w_gate, ffn_w_up, ffn_conv_w, ffn_conv_b, ffn_w_down, final_norm_w):
    depth = w_in.shape[0]
    bp = x_prompt.shape[0]
    bs, d = c.shape
    lw = lru_conv_w.shape[2]
    dh = RET_CHUNK
    rw = RET_HEADS * dh

    ctx_row = bs
    rows = -(-(bs + 1) // 8) * 8
    cvec = jnp.zeros((rows, d), F32).at[:bs].set(c).at[ctx_row].set(c_ctx)
    row_ctx = lambda i: ctx_row
    row_lat = lambda i: i

    x_p, x_s = x_prompt, x_sample
    fins = []
    for l in range(depth):
        mod = _modulation(cvec, w_mod[l], b_mod[l][None]).reshape(rows, N_MOD, d)
        tab = _decay_tables(jnp.stack([ret_decay_fw[l], ret_decay_bw[l]]).astype(F32))
        p = dict(
            norm1_w=norm1_w[l][None],
            w_bw=jnp.concatenate([w_in[l][:, 0:lw], w_in[l][:, 2 * lw + rw:2 * lw + 3 * rw]], axis=1).astype(BF16),
            w_fw=jnp.concatenate([w_in[l][:, lw:2 * lw + rw], w_in[l][:, 2 * lw + 3 * rw:]], axis=1).astype(BF16),
            lru_conv_w=lru_conv_w[l], lru_conv_b=lru_conv_b[l][None],
            wg=jnp.stack([jnp.stack([_block_diag_tiles(lru_wa_fw[l]), _block_diag_tiles(lru_wx_fw[l])]),
                          jnp.stack([_block_diag_tiles(lru_wa_bw[l]), _block_diag_tiles(lru_wx_bw[l])])]
                         ).astype(BF16),
            lp=jnp.stack([jnp.stack([lru_ba_fw[l], lru_bx_fw[l], lru_lambda_fw[l]]),
                          jnp.stack([lru_ba_bw[l], lru_bx_bw[l], lru_lambda_bw[l]])]).astype(F32),
            ret_gn_w=ret_gn_w[l][None], w_out=w_out[l].astype(BF16), norm2_w=norm2_w[l][None],
            ffn_wg=ffn_w_gate[l].astype(BF16), ffn_wu=ffn_w_up[l].astype(BF16), ffn_wd=ffn_w_down[l].astype(BF16),
            ffn_cw=ffn_conv_w[l].reshape(9, -1), ffn_cb=ffn_conv_b[l][None],
            final_norm_w=final_norm_w[None],
        )
        final = l == depth - 1
        zeros_h = jnp.zeros((bp, 1, lw), F32)
        zeros_s = jnp.zeros((bp, RET_HEADS, dh, dh), F32)
        x_p, fin = _layer(x_p, mod, row_ctx, p, tab, (zeros_h, zeros_h, zeros_s, zeros_s), False, final)
        fins.append(fin)
        init_lat = (state_lru_fw[:, l][:, None].astype(F32), state_lru_bw[:, l][:, None].astype(F32),
                    state_ret_fw[:, l].astype(F32), state_ret_bw[:, l].astype(F32))
        x_s, _ = _layer(x_s, mod, row_lat, p, tab, init_lat, True, final)

    dt = x_prompt.dtype
    new_lru_fw = jnp.concatenate([f[0] for f in fins], axis=1).astype(dt)
    new_lru_bw = jnp.concatenate([f[1] for f in fins], axis=1).astype(dt)
    new_ret_fw = jnp.stack([f[2] for f in fins], axis=1).astype(dt)
    new_ret_bw = jnp.stack([f[3] for f in fins], axis=1).astype(dt)
    return (x_p, x_s, new_lru_fw, new_lru_bw, new_ret_fw, new_ret_bw)
```

```python
import functools
import math

import jax
import jax.numpy as jnp
from jax import lax
from jax.experimental import pallas as pl
from jax.experimental.pallas import tpu as pltpu

F32 = jnp.float32
BF16 = jnp.bfloat16

N_MOD = 6
EPS = 1e-6
LRU_C = 8.0
LRU_CONV_WIDTH = 4
LRU_CONV_PAD_LEFT = 2
RET_HEADS = 4
RET_CHUNK = 128
GRID_W = 64
GATE_TILE = 256
FFN_TILE = 256
TILE_GROUP = 4
SUBLANES = 8
LANES = 128
HALO_ROWS = SUBLANES
LEAD_ROWS = 2 * SUBLANES
MOD_COL_BLOCKS = 4
SEG_ROWS = 4
TIME_CHUNK = 512
FFN_ROWS = 512
VMEM_LIMIT = 56 * 1024 * 1024
def _dot(a, b):
    return jnp.dot(a.astype(BF16), b.astype(BF16), preferred_element_type=F32)


def _silu(x):
    h = 0.5 * x
    return h * jnp.tanh(h) + h


def _gelu_tanh(x):
    c = math.sqrt(2.0 / math.pi)
    return (0.5 * x) * (1.0 + jnp.tanh(x * (c + (c * 0.044715) * (x * x))))


def _softplus(x):
    return jnp.maximum(x, 0.0) + jnp.log1p(jnp.exp(-jnp.abs(x)))


def _rms(x):
    return x * lax.rsqrt(jnp.mean(x * x, axis=-1, keepdims=True) + EPS)


def _params(*sem):
    return pltpu.CompilerParams(dimension_semantics=sem, vmem_limit_bytes=VMEM_LIMIT)


def _const_spec(shape):
    nd = len(shape)
    return pl.BlockSpec(shape, lambda *_: (0,) * nd)


def _resident(a):
    return pl.BlockSpec(a.shape, lambda *_: (0,) * a.ndim, pipeline_mode=pl.Buffered(1))


def _mod_kernel(c_ref, w_ref, b_ref, o_ref):
    o_ref[...] = _dot(_silu(c_ref[...]), w_ref[...]) + b_ref[...]


def _modulation(cvec, w, b):
    rows, d = cvec.shape
    n = w.shape[1]
    tn = n // MOD_COL_BLOCKS
    return pl.pallas_call(
        _mod_kernel,
        grid=(n // tn,),
        in_specs=[pl.BlockSpec((rows, d), lambda j: (0, 0)),
                  pl.BlockSpec((d, tn), lambda j: (0, j)),
                  pl.BlockSpec((1, tn), lambda j: (0, j))],
        out_specs=pl.BlockSpec((rows, tn), lambda j: (0, j)),
        out_shape=jax.ShapeDtypeStruct((rows, n), F32),
        compiler_params=_params("arbitrary"),
        name="modulation",
    )(cvec, w, b)


TAB_INTRA, TAB_HEAD_FW, TAB_HEAD_BW, TAB_TAIL_FW, TAB_TAIL_BW, TAB_STEP_FW, TAB_STEP_BW = range(7)


def _tab_kernel(dec_ref, tab_ref):
    c = RET_CHUNK
    row = lax.broadcasted_iota(jnp.int32, (c, c), 0).astype(F32)
    col = lax.broadcasted_iota(jnp.int32, (c, c), 1).astype(F32)
    rel = row - col
    for h in range(RET_HEADS):
        lf = -_softplus(-jnp.full((c, c), dec_ref[0, h], F32))
        lb = -_softplus(-jnp.full((c, c), dec_ref[1, h], F32))
        tab_ref[TAB_INTRA, h] = jnp.where(rel > 0, jnp.exp(lf * rel),
                                          jnp.where(rel < 0, jnp.exp(lb * (-rel)), 2.0))
        tab_ref[TAB_HEAD_FW, h] = jnp.exp(lf * (row + 1.0))
        tab_ref[TAB_HEAD_BW, h] = jnp.exp(lb * (c - row))
        tab_ref[TAB_TAIL_FW, h] = jnp.exp(lf * (c - 1.0 - row))
        tab_ref[TAB_TAIL_BW, h] = jnp.exp(lb * row)
        tab_ref[TAB_STEP_FW, h] = jnp.exp(lf * c)
        tab_ref[TAB_STEP_BW, h] = jnp.exp(lb * c)


def _decay_tables(dec):
    return pl.pallas_call(
        _tab_kernel,
        in_specs=[pl.BlockSpec(memory_space=pltpu.SMEM)],
        out_shape=jax.ShapeDtypeStruct((7, RET_HEADS, RET_CHUNK, RET_CHUNK), F32),
        name="decay_tables",
    )(dec)


def _norm_mod(x, m, nw):
    return _rms(x) * (nw * (1.0 + m[1:2])) + m[0:1]


def _lru_conv(prev, cur, nxt, cw_ref, cb_ref):
    tc = cur.shape[0]
    lead = prev.shape[0]
    ext = jnp.concatenate([prev, cur, nxt], axis=0)
    rows = ext.shape[0]
    cw = cw_ref[...]
    acc = cb_ref[...] + cur * cw[LRU_CONV_PAD_LEFT:LRU_CONV_PAD_LEFT + 1]
    for j in range(LRU_CONV_WIDTH):
        off = j - LRU_CONV_PAD_LEFT
        if off:
            acc = acc + pltpu.roll(ext, (-off) % rows, 0)[lead:lead + tc] * cw[j:j + 1]
    return acc


def _lru_gates(xc, xc_bf, wg_ref, lp_ref, direction):
    lp = lp_ref[direction]
    width = xc.shape[1]

    def gate(g):
        parts = [jnp.dot(xc_bf[:, GATE_TILE * m:GATE_TILE * (m + 1)], wg_ref[direction, g, m],
                         preferred_element_type=F32) for m in range(width // GATE_TILE)]
        return jnp.concatenate(parts, axis=1)

    tanh_r = jnp.tanh(0.5 * gate(0) + 0.5 * lp[0:1])
    tanh_i = jnp.tanh(0.5 * gate(1) + 0.5 * lp[1:2])
    half = (-0.5 * LRU_C) * _softplus(-lp[2:3])
    log_a = half * tanh_r + half
    a = jnp.exp(log_a)
    gain2 = -jnp.tanh(log_a) * (a * a + 1.0)
    gain = jnp.where(gain2 == 0.0, 0.0, gain2 * lax.rsqrt(gain2))
    u = gain * ((0.5 * tanh_i + 0.5) * xc)
    return a, u


def _store_slabs(ref, val):
    for k in range(ref.shape[0]):
        ref[k] = val[:, k * LANES:(k + 1) * LANES]


def _lru_scan(a_ref, u_ref, h_ref, h_pre, carry_ref, reverse, tc):
    nslab = a_ref.shape[0]
    blk = SUBLANES * SEG_ROWS
    nblk = tc // blk
    row = lax.broadcasted_iota(jnp.int32, (SUBLANES, LANES), 0)
    order = tuple(range(SEG_ROWS - 1, -1, -1)) if reverse else tuple(range(SEG_ROWS))
    edge = 0 if reverse else SUBLANES - 1

    def earlier(x, d, fill):
        if reverse:
            return jnp.where(row < SUBLANES - d, pltpu.roll(x, SUBLANES - d, 0), fill)
        return jnp.where(row >= d, pltpu.roll(x, d, 0), fill)

    carry = [carry_ref[k] for k in range(nslab)]
    for bi in range(nblk):
        base = ((nblk - 1 - bi) if reverse else bi) * blk
        for k in range(nslab):
            idx = [pl.ds(base + j, SUBLANES, stride=SEG_ROWS) for j in range(SEG_ROWS)]
            av = [a_ref[k, i, :] for i in idx]
            uv = [u_ref[k, i, :] for i in idx]
            h, p = uv[order[0]], av[order[0]]
            for j in order[1:]:
                h = av[j] * h + uv[j]
                p = av[j] * p
            d = 1
            while d < SUBLANES:
                h = p * earlier(h, d, 0.0) + h
                p = p * earlier(p, d, 1.0)
                d *= 2
            c_in = carry[k]
            state = earlier(h, 1, 0.0) + earlier(p, 1, 1.0) * c_in
            carry[k] = jnp.broadcast_to(h[edge:edge + 1] + p[edge:edge + 1] * c_in[edge:edge + 1], (SUBLANES, LANES))
            for j in order:
                state = av[j] * state + uv[j]
                h_ref[(*h_pre, k, idx[j], slice(None))] = state
    for k in range(nslab):
        carry_ref[k] = carry[k]


def _bw_kernel(xprev_ref, x_ref, mod_ref, nw_ref, w_ref, tab_ref, cw_ref, cb_ref, wg_ref, lp_ref, h0_ref, s0_ref,
               xc_ref, hbw_ref, sbw_ref, kb_ref, ktf_ref, vb_ref, hin_ref, finh_ref, fins_ref,
               a_ref, u_ref, h_ref, hcar_ref, s_ref, nx_ref, *, tc, nc):
    c = pl.program_id(1)
    chunk = nc - 1 - c
    dh = RET_CHUNK
    nslab = a_ref.shape[0]
    lw = nslab * LANES

    @pl.when(c == 0)
    def _():
        for k in range(nslab):
            hcar_ref[k] = jnp.broadcast_to(h0_ref[0, :, k * LANES:(k + 1) * LANES], (SUBLANES, LANES))
        s_ref[...] = s0_ref[0]
        nx_ref[...] = jnp.zeros_like(nx_ref)

    m = mod_ref[0]
    nw = nw_ref[...]
    h_cur = _norm_mod(x_ref[0], m, nw).astype(BF16)
    hin_ref[0] = h_cur
    h_ext = jnp.concatenate([_norm_mod(xprev_ref[0], m, nw).astype(BF16), h_cur], axis=0)
    lx_ext = jnp.dot(h_ext, w_ref[:, 0:lw], preferred_element_type=F32)
    kv = jnp.dot(h_cur, w_ref[:, lw:], preferred_element_type=F32)
    lead = xprev_ref.shape[1]
    lx = lx_ext[lead:]
    lx_prev = jnp.where(chunk == 0, 0.0, lx_ext[0:lead])
    xc = _lru_conv(lx_prev, lx, nx_ref[...], cw_ref, cb_ref)
    nx_ref[...] = lx[0:nx_ref.shape[0]]
    xc_ref[0] = xc

    a, u = _lru_gates(xc, xc.astype(BF16), wg_ref, lp_ref, 1)
    _store_slabs(a_ref, a)
    _store_slabs(u_ref, u)
    _lru_scan(a_ref, u_ref, h_ref, (), hcar_ref, True, tc)
    for k in range(nslab):
        hbw_ref[0, :, k * LANES:(k + 1) * LANES] = h_ref[k]

    k = kv[:, 0:RET_HEADS * dh] * dh ** -0.5
    v_bf = kv[:, RET_HEADS * dh:].astype(BF16)
    vb_ref[0] = v_bf
    pairs = [(n, hd) for n in range(tc // RET_CHUNK) for hd in range(RET_HEADS)]
    rows = lambda n: slice(n * RET_CHUNK, (n + 1) * RET_CHUNK)
    cols = lambda hd: slice(hd * dh, (hd + 1) * dh)
    for n, hd in pairs:
        k_nh = k[rows(n), cols(hd)]
        kb_ref[0, n, hd] = k_nh.T.astype(BF16)
        ktf_ref[0, n, hd] = (k_nh * tab_ref[TAB_TAIL_FW, hd]).T.astype(BF16)
    kvs = {(n, hd): lax.dot_general((k[rows(n), cols(hd)] * tab_ref[TAB_TAIL_BW, hd]).astype(BF16),
                                    v_bf[rows(n), cols(hd)], (((0,), (0,)), ((), ())), preferred_element_type=F32)
           for n, hd in pairs}
    for hd in range(RET_HEADS):
        s = s_ref[hd]
        for n in reversed(range(tc // RET_CHUNK)):
            sbw_ref[0, n, hd] = s.astype(BF16)
            s = tab_ref[TAB_STEP_BW, hd] * s + kvs[n, hd]
        s_ref[hd] = s

    @pl.when(c == 0)
    def _():
        for k in range(nslab):
            finh_ref[0, :, k * LANES:(k + 1) * LANES] = h_ref[k, tc - 1:tc, :]

    @pl.when(c == nc - 1)
    def _():
        fins_ref[0] = s_ref[...]


def _backward_sweep(x, mod, mod_row, nw, w_bw, tab, cw, cb, wg, lp, h0, s0, tc):
    b, t, d = x.shape
    lw = cw.shape[1]
    nc = t // tc
    nlead = tc // LEAD_ROWS
    nsub = tc // RET_CHUNK
    nslab = lw // LANES
    dh = RET_CHUNK
    ch = lambda j: nc - 1 - j
    tok = lambda dt: jax.ShapeDtypeStruct((b, t, lw), dt)
    tok_spec = pl.BlockSpec((1, tc, lw), lambda i, j: (i, ch(j), 0))
    sub = jax.ShapeDtypeStruct((b, t // RET_CHUNK, RET_HEADS, dh, dh), BF16)
    sub_spec = pl.BlockSpec((1, nsub, RET_HEADS, dh, dh), lambda i, j: (i, ch(j), 0, 0, 0))
    return pl.pallas_call(
        functools.partial(_bw_kernel, tc=tc, nc=nc),
        grid=(b, nc),
        in_specs=[
            pl.BlockSpec((1, LEAD_ROWS, d), lambda i, j: (i, jnp.maximum(ch(j) * nlead - 1, 0), 0)),
            pl.BlockSpec((1, tc, d), lambda i, j: (i, ch(j), 0)),
            pl.BlockSpec((1, N_MOD, d), lambda i, j: (mod_row(i), 0, 0)),
            _const_spec(nw.shape), _resident(w_bw), _resident(tab), _const_spec(cw.shape), _const_spec(cb.shape),
            _resident(wg), _const_spec(lp.shape),
            pl.BlockSpec((1, 1, lw), lambda i, j: (i, 0, 0)),
            pl.BlockSpec((1, RET_HEADS, dh, dh), lambda i, j: (i, 0, 0, 0)),
        ],
        out_specs=[
            tok_spec, tok_spec,
            sub_spec, sub_spec, sub_spec, tok_spec,
            pl.BlockSpec((1, tc, d), lambda i, j: (i, ch(j), 0)),
            pl.BlockSpec((1, 1, lw), lambda i, j: (i, 0, 0)),
            pl.BlockSpec((1, RET_HEADS, dh, dh), lambda i, j: (i, 0, 0, 0)),
        ],
        out_shape=[
            tok(F32), tok(F32),
            sub, sub, sub, tok(BF16),
            jax.ShapeDtypeStruct((b, t, d), BF16),
            jax.ShapeDtypeStruct((b, 1, lw), F32),
            jax.ShapeDtypeStruct((b, RET_HEADS, dh, dh), F32),
        ],
        scratch_shapes=[pltpu.VMEM((nslab, tc, LANES), F32),
                        pltpu.VMEM((nslab, tc, LANES), F32),
                        pltpu.VMEM((nslab, tc, LANES), F32),
                        pltpu.VMEM((nslab, SUBLANES, LANES), F32),
                        pltpu.VMEM((RET_HEADS, dh, dh), F32),
                        pltpu.VMEM((HALO_ROWS, lw), F32)],
        compiler_params=_params("arbitrary", "arbitrary"),
        name="backward_sweep",
    )(x, x, mod, nw, w_bw, tab, cw, cb, wg, lp, h0, s0)


def _fw_kernel(x_ref, mod_ref, hin_ref, w_ref, xc_ref, hbw_ref, sbw_ref, kb_ref, ktf_ref, vb_ref,
               tab_ref, wg_ref, lp_ref, gnw_ref, wout_ref, h0_ref, s0_ref,
               o_ref, finh_ref, fins_ref, a_ref, u_ref, h_ref, hcar_ref, s_ref, y_ref, *, tc, nc):
    c = pl.program_id(1)
    dh = RET_CHUNK
    lw = xc_ref.shape[2]
    nslab = a_ref.shape[0]

    @pl.when(c == 0)
    def _():
        for k in range(nslab):
            hcar_ref[k] = jnp.broadcast_to(h0_ref[0, :, k * LANES:(k + 1) * LANES], (SUBLANES, LANES))
        s_ref[...] = s0_ref[0]

    m = mod_ref[0]
    h_in = hin_ref[0]

    xc = xc_ref[0]
    a, u = _lru_gates(xc, xc.astype(BF16), wg_ref, lp_ref, 0)
    _store_slabs(a_ref, a)
    _store_slabs(u_ref, u)
    _lru_scan(a_ref, u_ref, h_ref, (), hcar_ref, False, tc)

    lru_g = jnp.dot(h_in, w_ref[:, 0:lw], preferred_element_type=F32)
    qg = jnp.dot(h_in, w_ref[:, lw:], preferred_element_type=F32)
    q_all = qg[:, 0:RET_HEADS * dh]
    ret_g = qg[:, RET_HEADS * dh:]
    for k in range(nslab):
        cols = slice(k * LANES, (k + 1) * LANES)
        y_ref[:, cols] = ((h_ref[k] + hbw_ref[0, :, cols]) * _gelu_tanh(lru_g[:, cols])).astype(BF16)

    gnw = gnw_ref[...]
    pairs = [(n, hd) for n in range(tc // RET_CHUNK) for hd in range(RET_HEADS)]
    rows = lambda n: slice(n * RET_CHUNK, (n + 1) * RET_CHUNK)
    cols = lambda hd: slice(hd * dh, (hd + 1) * dh)
    kv = {(n, hd): jnp.dot(ktf_ref[0, n, hd], vb_ref[0, rows(n), cols(hd)], preferred_element_type=F32)
          for n, hd in pairs}
    state = {}
    for hd in range(RET_HEADS):
        s = s_ref[hd]
        for n in range(tc // RET_CHUNK):
            state[n, hd] = s
            s = tab_ref[TAB_STEP_FW, hd] * s + kv[n, hd]
        s_ref[hd] = s
    scores = {(n, hd): jnp.dot(q_all[rows(n), cols(hd)].astype(BF16), kb_ref[0, n, hd],
                               preferred_element_type=F32) for n, hd in pairs}
    intra = {(n, hd): _dot(scores[n, hd] * tab_ref[TAB_INTRA, hd], vb_ref[0, rows(n), cols(hd)])
             for n, hd in pairs}
    out = {}
    for n, hd in pairs:
        q = q_all[rows(n), cols(hd)]
        q_both = jnp.concatenate([q * tab_ref[TAB_HEAD_FW, hd], q * tab_ref[TAB_HEAD_BW, hd]], axis=1)
        s_both = jnp.concatenate([state[n, hd].astype(BF16), sbw_ref[0, n, hd]], axis=0)
        out[n, hd] = intra[n, hd] + _dot(q_both, s_both)
    for n, hd in pairs:
        o = out[n, hd]
        dev = o - jnp.mean(o, axis=-1, keepdims=True)
        on = dev * lax.rsqrt(jnp.mean(dev * dev, axis=-1, keepdims=True) + EPS) * gnw[:, cols(hd)]
        y_ref[rows(n), lw + hd * dh:lw + (hd + 1) * dh] = (on * _silu(ret_g[rows(n), cols(hd)])).astype(BF16)

    o_ref[0] = x_ref[0] + m[2:3] * _dot(y_ref[...], wout_ref[...])

    @pl.when(c == 0)
    def _():
        for k in range(nslab):
            finh_ref[0, :, k * LANES:(k + 1) * LANES] = h_ref[k, 0:1, :]

    @pl.when(c == nc - 1)
    def _():
        fins_ref[0] = s_ref[...]


def _forward_sweep(x, mod, mod_row, hin, w_fw, xc, hbw, sbw, kb, ktf, vb, tab, wg, lp, gnw, wout_bf, h0, s0, tc):
    b, t, d = x.shape
    lw = xc.shape[2]
    nc = t // tc
    nsub = tc // RET_CHUNK
    nslab = lw // LANES
    dh = RET_CHUNK
    tok_spec = pl.BlockSpec((1, tc, lw), lambda i, j: (i, j, 0))
    sub_spec = pl.BlockSpec((1, nsub, RET_HEADS, dh, dh), lambda i, j: (i, j, 0, 0, 0))
    return pl.pallas_call(
        functools.partial(_fw_kernel, tc=tc, nc=nc),
        grid=(b, nc),
        in_specs=[
            pl.BlockSpec((1, tc, d), lambda i, j: (i, j, 0)),
            pl.BlockSpec((1, N_MOD, d), lambda i, j: (mod_row(i), 0, 0)),
            pl.BlockSpec((1, tc, d), lambda i, j: (i, j, 0)), _resident(w_fw),
            tok_spec, tok_spec,
            sub_spec, sub_spec, sub_spec, tok_spec,
            _resident(tab),
            _resident(wg), _const_spec(lp.shape), _const_spec(gnw.shape), _resident(wout_bf),
            pl.BlockSpec((1, 1, lw), lambda i, j: (i, 0, 0)),
            pl.BlockSpec((1, RET_HEADS, dh, dh), lambda i, j: (i, 0, 0, 0)),
        ],
        out_specs=[
            pl.BlockSpec((1, tc, d), lambda i, j: (i, j, 0)),
            pl.BlockSpec((1, 1, lw), lambda i, j: (i, 0, 0)),
            pl.BlockSpec((1, RET_HEADS, dh, dh), lambda i, j: (i, 0, 0, 0)),
        ],
        out_shape=[
            jax.ShapeDtypeStruct((b, t, d), F32),
            jax.ShapeDtypeStruct((b, 1, lw), F32),
            jax.ShapeDtypeStruct((b, RET_HEADS, dh, dh), F32),
        ],
        scratch_shapes=[pltpu.VMEM((nslab, tc, LANES), F32),
                        pltpu.VMEM((nslab, tc, LANES), F32),
                        pltpu.VMEM((nslab, tc, LANES), F32),
                        pltpu.VMEM((nslab, SUBLANES, LANES), F32),
                        pltpu.VMEM((RET_HEADS, dh, dh), F32),
                        pltpu.VMEM((tc, wout_bf.shape[0]), BF16)],
        compiler_params=_params("arbitrary", "arbitrary"),
        name="forward_sweep",
    )(x, mod, hin, w_fw, xc, hbw, sbw, kb, ktf, vb, tab, wg, lp, gnw, wout_bf, h0, s0)


def _ffn_kernel(*refs, tf, halo, period, taps, final):
    if halo:
        xp_ref, x_ref, xn_ref = refs[:3]
        refs = refs[3:]
    else:
        x_ref = refs[0]
        refs = refs[1:]
    mod_ref, n2w_ref, wg_ref, wu_ref, wd_ref, cw_ref, cb_ref, fnw_ref, o_ref = refs
    m = mod_ref[0]
    n2wm = n2w_ref[...] * (1.0 + m[4:5])

    def norm_mod(x):
        return _rms(x) * n2wm + m[3:4]

    x = x_ref[0]
    h2_cur = norm_mod(x).astype(BF16)
    h2 = h2_cur
    if halo:
        j = pl.program_id(1)
        nj = pl.num_programs(1)
        h2 = jnp.concatenate([jnp.where(j == 0, 0.0, norm_mod(xp_ref[0])).astype(BF16), h2_cur,
                              jnp.where(j == nj - 1, 0.0, norm_mod(xn_ref[0])).astype(BF16)], axis=0)
    rows = tf + 2 * halo
    nper = rows // period
    sub = lax.broadcasted_iota(jnp.int32, (nper, SUBLANES, FFN_TILE), 1)

    def neighbour(g, step):
        r3 = pltpu.roll(g, (-step) % rows, 0).reshape(nper, period, FFN_TILE)
        if step < 0:
            fixed = [jnp.where(sub == 0, 0.0, r3[:, 0:SUBLANES]), r3[:, SUBLANES:]]
        else:
            fixed = [r3[:, 0:period - SUBLANES], jnp.where(sub == SUBLANES - 1, 0.0, r3[:, period - SUBLANES:])]
        return jnp.concatenate(fixed, axis=1).reshape(rows, FFN_TILE)

    acts = []
    ntiles = wg_ref.shape[1] // FFN_TILE
    for n0 in range(0, ntiles, TILE_GROUP):
        group = range(n0, min(n0 + TILE_GROUP, ntiles))
        cs = {n: slice(n * FFN_TILE, (n + 1) * FFN_TILE) for n in group}
        g = {n: jnp.dot(h2, wg_ref[:, cs[n]], preferred_element_type=F32) for n in group}
        up = {n: jnp.dot(h2_cur, wu_ref[:, cs[n]], preferred_element_type=F32) for n in group}
        g_left = {n: neighbour(g[n], -1) for n in group}
        g_right = {n: neighbour(g[n], 1) for n in group}
        for n in group:
            cw = cw_ref[:, cs[n]]
            conv = cb_ref[:, cs[n]]
            for dr, wrow in taps:
                lo = halo + dr * period
                conv = (conv + g_left[n][lo:lo + tf] * cw[3 * wrow:3 * wrow + 1]
                        + g[n][lo:lo + tf] * cw[3 * wrow + 1:3 * wrow + 2]
                        + g_right[n][lo:lo + tf] * cw[3 * wrow + 2:3 * wrow + 3])
            acts.append((_gelu_tanh(conv) * up[n]).astype(BF16))

    out = x + m[5:6] * jnp.dot(jnp.concatenate(acts, axis=1), wd_ref[...], preferred_element_type=F32)
    o_ref[0] = _rms(out) * fnw_ref[...] if final else out


def _ffn(x, mod, mod_row, n2w, wg, wu, wd, cw, cb, fnw, latent, final):
    b, t, d = x.shape
    if latent:
        tf, halo, period = FFN_ROWS, GRID_W, GRID_W
        taps = ((-1, 0), (0, 1), (1, 2))
        nb = tf // halo
        x_specs = [pl.BlockSpec((1, halo, d), lambda i, j: (i, jnp.maximum(j * nb - 1, 0), 0)),
                   pl.BlockSpec((1, tf, d), lambda i, j: (i, j, 0)),
                   pl.BlockSpec((1, halo, d), lambda i, j: (i, jnp.minimum((j + 1) * nb, t // halo - 1), 0))]
        x_args = (x, x, x)
    else:
        tf, halo, period = t, 0, t
        taps = ((0, 1),)
        x_specs = [pl.BlockSpec((1, tf, d), lambda i, j: (i, j, 0))]
        x_args = (x,)
    assert period & (period - 1) == 0 and t % tf == 0
    return pl.pallas_call(
        functools.partial(_ffn_kernel, tf=tf, halo=halo, period=period, taps=taps, final=final),
        grid=(b, t // tf),
        in_specs=x_specs + [
            pl.BlockSpec((1, N_MOD, d), lambda i, j: (mod_row(i), 0, 0)),
            _const_spec(n2w.shape), _resident(wg), _resident(wu), _resident(wd),
            _const_spec(cw.shape), _const_spec(cb.shape), _const_spec(fnw.shape)],
        out_specs=pl.BlockSpec((1, tf, d), lambda i, j: (i, j, 0)),
        out_shape=jax.ShapeDtypeStruct((b, t, d), F32),
        compiler_params=_params("arbitrary", "arbitrary"),
        name="ffn",
    )(*x_args, mod, n2w, wg, wu, wd, cw, cb, fnw)


def _block_diag_tiles(w):
    nb, c, _ = w.shape
    per = GATE_TILE // c
    eye = jnp.eye(per, dtype=w.dtype)
    w = w.reshape(nb // per, per, c, c)
    return jnp.einsum('mjab,jk->mjakb', w, eye).reshape(nb // per, GATE_TILE, GATE_TILE)


def _layer(x, mod, mod_row, p, tab, states, latent, final):
    b, t, d = x.shape
    tc = min(t, TIME_CHUNK)
    xc, hbw, sbw, kb, ktf, vb, hin, fin_h_bw, fin_s_bw = _backward_sweep(
        x, mod, mod_row, p['norm1_w'], p['w_bw'], tab, p['lru_conv_w'], p['lru_conv_b'], p['wg'], p['lp'],
        states[1], states[3], tc)
    x, fin_h_fw, fin_s_fw = _forward_sweep(
        x, mod, mod_row, hin, p['w_fw'], xc, hbw, sbw, kb, ktf, vb, tab, p['wg'], p['lp'],
        p['ret_gn_w'], p['w_out'], states[0], states[2], tc)
    x = _ffn(x, mod, mod_row, p['norm2_w'], p['ffn_wg'], p['ffn_wu'], p['ffn_wd'], p['ffn_cw'], p['ffn_cb'],
             p['final_norm_w'], latent, final)
    return x, (fin_h_fw, fin_h_bw, fin_s_fw, fin_s_bw)


def kernel(x_prompt, x_sample, state_lru_fw, state_lru_bw, state_ret_fw, state_ret_bw, c, c_ctx,
           norm1_w, w_mod, b_mod, w_in, lru_conv_w, lru_conv_b,
           lru_wa_fw, lru_ba_fw, lru_wx_fw, lru_bx_fw, lru_lambda_fw,
           lru_wa_bw, lru_ba_bw, lru_wx_bw, lru_bx_bw, lru_lambda_bw,
           ret_decay_fw, ret_decay_bw, ret_gn_w, w_out, norm2_w,
           ffn_w_gate, ffn_w_up, ffn_conv_w, ffn_conv_b, ffn_w_down, final_norm_w):
    depth = w_in.shape[0]
    bp = x_prompt.shape[0]
    bs, d = c.shape
    lw = lru_conv_w.shape[2]
    dh = RET_CHUNK
    rw = RET_HEADS * dh

    ctx_row = bs
    rows = -(-(bs + 1) // SUBLANES) * SUBLANES
    cvec = jnp.zeros((rows, d), F32).at[:bs].set(c).at[ctx_row].set(c_ctx)
    row_ctx = lambda i: ctx_row
    row_lat = lambda i: i

    x_p, x_s = x_prompt, x_sample
    fins = []
    for l in range(depth):
        mod = _modulation(cvec, w_mod[l], b_mod[l][None]).reshape(rows, N_MOD, d)
        tab = _decay_tables(jnp.stack([ret_decay_fw[l], ret_decay_bw[l]]).astype(F32))
        p = dict(
            norm1_w=norm1_w[l][None],
            w_bw=jnp.concatenate([w_in[l][:, 0:lw], w_in[l][:, 2 * lw + rw:2 * lw + 3 * rw]], axis=1).astype(BF16),
            w_fw=jnp.concatenate([w_in[l][:, lw:2 * lw + rw], w_in[l][:, 2 * lw + 3 * rw:]], axis=1).astype(BF16),
            lru_conv_w=lru_conv_w[l], lru_conv_b=lru_conv_b[l][None],
            wg=jnp.stack([jnp.stack([_block_diag_tiles(lru_wa_fw[l]), _block_diag_tiles(lru_wx_fw[l])]),
                          jnp.stack([_block_diag_tiles(lru_wa_bw[l]), _block_diag_tiles(lru_wx_bw[l])])]
                         ).astype(BF16),
            lp=jnp.stack([jnp.stack([lru_ba_fw[l], lru_bx_fw[l], lru_lambda_fw[l]]),
                          jnp.stack([lru_ba_bw[l], lru_bx_bw[l], lru_lambda_bw[l]])]).astype(F32),
            ret_gn_w=ret_gn_w[l][None], w_out=w_out[l].astype(BF16), norm2_w=norm2_w[l][None],
            ffn_wg=ffn_w_gate[l].astype(BF16), ffn_wu=ffn_w_up[l].astype(BF16), ffn_wd=ffn_w_down[l].astype(BF16),
            ffn_cw=ffn_conv_w[l].reshape(9, -1), ffn_cb=ffn_conv_b[l][None],
            final_norm_w=final_norm_w[None],
        )
        final = l == depth - 1
        zeros_h = jnp.zeros((bp, 1, lw), F32)
        zeros_s = jnp.zeros((bp, RET_HEADS, dh, dh), F32)
        x_p, fin = _layer(x_p, mod, row_ctx, p, tab, (zeros_h, zeros_h, zeros_s, zeros_s), False, final)
        fins.append(fin)
        init_lat = (state_lru_fw[:, l][:, None].astype(F32), state_lru_bw[:, l][:, None].astype(F32),
                    state_ret_fw[:, l].astype(F32), state_ret_bw[:, l].astype(F32))
        x_s, _ = _layer(x_s, mod, row_lat, p, tab, init_lat, True, final)

    dt = x_prompt.dtype
    new_lru_fw = jnp.concatenate([f[0] for f in fins], axis=1).astype(dt)
    new_lru_bw = jnp.concatenate([f[1] for f in fins], axis=1).astype(dt)
    new_ret_fw = jnp.stack([f[2] for f in fins], axis=1).astype(dt)
    new_ret_bw = jnp.stack([f[3] for f in fins], axis=1).astype(dt)
    return (x_p, x_s, new_lru_fw, new_lru_bw, new_ret_fw, new_ret_bw)
```

```python
import functools
import math

import jax
import jax.numpy as jnp
from jax import lax
from jax.experimental import pallas as pl
from jax.experimental.pallas import tpu as pltpu

F32 = jnp.float32
BF16 = jnp.bfloat16

N_MOD = 6
EPS = 1e-6
LRU_C = 8.0
LRU_CONV_WIDTH = 4
LRU_CONV_PAD_LEFT = 2
RET_HEADS = 4
RET_CHUNK = 128
GRID_W = 64
GATE_TILE = 256
FFN_TILE = 256
TILE_GROUP = 4
SUBLANES = 8
LANES = 128
HALO_ROWS = SUBLANES
LEAD_ROWS = 2 * SUBLANES
MOD_COL_BLOCKS = 4
SEG_ROWS = 4
TIME_CHUNK = 512
FFN_ROWS = 512
VMEM_LIMIT = 56 * 1024 * 1024
def _dot(a, b):
    return jnp.dot(a.astype(BF16), b.astype(BF16), preferred_element_type=F32)


def _silu(x):
    h = 0.5 * x
    return h * jnp.tanh(h) + h


def _gelu_tanh(x):
    c = math.sqrt(2.0 / math.pi)
    return (0.5 * x) * (1.0 + jnp.tanh(x * (c + (c * 0.044715) * (x * x))))


def _softplus(x):
    return jnp.maximum(x, 0.0) + jnp.log1p(jnp.exp(-jnp.abs(x)))


def _rms(x):
    return x * lax.rsqrt(jnp.mean(x * x, axis=-1, keepdims=True) + EPS)


def _params(*sem):
    return pltpu.CompilerParams(dimension_semantics=sem, vmem_limit_bytes=VMEM_LIMIT)


def _const_spec(shape):
    nd = len(shape)
    return pl.BlockSpec(shape, lambda *_: (0,) * nd)


def _resident(a):
    return pl.BlockSpec(a.shape, lambda *_: (0,) * a.ndim, pipeline_mode=pl.Buffered(1))


def _mod_kernel(c_ref, w_ref, b_ref, o_ref):
    o_ref[...] = _dot(_silu(c_ref[...]), w_ref[...]) + b_ref[...]


def _modulation(cvec, w, b):
    rows, d = cvec.shape
    n = w.shape[1]
    tn = n // MOD_COL_BLOCKS
    return pl.pallas_call(
        _mod_kernel,
        grid=(n // tn,),
        in_specs=[pl.BlockSpec((rows, d), lambda j: (0, 0)),
                  pl.BlockSpec((d, tn), lambda j: (0, j)),
                  pl.BlockSpec((1, tn), lambda j: (0, j))],
        out_specs=pl.BlockSpec((rows, tn), lambda j: (0, j)),
        out_shape=jax.ShapeDtypeStruct((rows, n), F32),
        compiler_params=_params("arbitrary"),
        name="modulation",
    )(cvec, w, b)


TAB_INTRA, TAB_HEAD_FW, TAB_HEAD_BW, TAB_TAIL_FW, TAB_TAIL_BW, TAB_STEP_FW, TAB_STEP_BW = range(7)
TILE_STATE, TILE_K, TILE_KTAIL = range(3)
N_TILES = 3


def _tab_kernel(dec_ref, tab_ref):
    c = RET_CHUNK
    row = lax.broadcasted_iota(jnp.int32, (c, c), 0).astype(F32)
    col = lax.broadcasted_iota(jnp.int32, (c, c), 1).astype(F32)
    rel = row - col
    for h in range(RET_HEADS):
        lf = -_softplus(-jnp.full((c, c), dec_ref[0, h], F32))
        lb = -_softplus(-jnp.full((c, c), dec_ref[1, h], F32))
        tab_ref[TAB_INTRA, h] = jnp.where(rel > 0, jnp.exp(lf * rel),
                                          jnp.where(rel < 0, jnp.exp(lb * (-rel)), 2.0))
        tab_ref[TAB_HEAD_FW, h] = jnp.exp(lf * (row + 1.0))
        tab_ref[TAB_HEAD_BW, h] = jnp.exp(lb * (c - row))
        tab_ref[TAB_TAIL_FW, h] = jnp.exp(lf * (c - 1.0 - row))
        tab_ref[TAB_TAIL_BW, h] = jnp.exp(lb * row)
        tab_ref[TAB_STEP_FW, h] = jnp.exp(lf * c)
        tab_ref[TAB_STEP_BW, h] = jnp.exp(lb * c)


def _decay_tables(dec):
    return pl.pallas_call(
        _tab_kernel,
        in_specs=[pl.BlockSpec(memory_space=pltpu.SMEM)],
        out_shape=jax.ShapeDtypeStruct((7, RET_HEADS, RET_CHUNK, RET_CHUNK), F32),
        name="decay_tables",
    )(dec)


def _norm_mod(x, m, nw):
    return _rms(x) * (nw * (1.0 + m[1:2])) + m[0:1]


def _lru_conv(prev, cur, nxt, cw_ref, cb_ref):
    tc = cur.shape[0]
    lead = prev.shape[0]
    ext = jnp.concatenate([prev, cur, nxt], axis=0)
    rows = ext.shape[0]
    cw = cw_ref[...]
    acc = cb_ref[...] + cur * cw[LRU_CONV_PAD_LEFT:LRU_CONV_PAD_LEFT + 1]
    for j in range(LRU_CONV_WIDTH):
        off = j - LRU_CONV_PAD_LEFT
        if off:
            acc = acc + pltpu.roll(ext, (-off) % rows, 0)[lead:lead + tc] * cw[j:j + 1]
    return acc


def _lru_gates(xc, xc_bf, wg_ref, lp_ref, direction):
    lp = lp_ref[direction]
    width = xc.shape[1]

    def gate(g):
        parts = [jnp.dot(xc_bf[:, GATE_TILE * m:GATE_TILE * (m + 1)], wg_ref[direction, g, m],
                         preferred_element_type=F32) for m in range(width // GATE_TILE)]
        return jnp.concatenate(parts, axis=1)

    tanh_r = jnp.tanh(0.5 * gate(0) + 0.5 * lp[0:1])
    tanh_i = jnp.tanh(0.5 * gate(1) + 0.5 * lp[1:2])
    half = (-0.5 * LRU_C) * _softplus(-lp[2:3])
    log_a = half * tanh_r + half
    a = jnp.exp(log_a)
    gain2 = -jnp.tanh(log_a) * (a * a + 1.0)
    gain = jnp.where(gain2 == 0.0, 0.0, gain2 * lax.rsqrt(gain2))
    u = gain * ((0.5 * tanh_i + 0.5) * xc)
    return a, u


def _store_slabs(ref, val):
    for k in range(ref.shape[0]):
        ref[k] = val[:, k * LANES:(k + 1) * LANES]


def _lru_scan(a_ref, u_ref, h_ref, h_pre, carry_ref, reverse, tc):
    nslab = a_ref.shape[0]
    blk = SUBLANES * SEG_ROWS
    nblk = tc // blk
    row = lax.broadcasted_iota(jnp.int32, (SUBLANES, LANES), 0)
    order = tuple(range(SEG_ROWS - 1, -1, -1)) if reverse else tuple(range(SEG_ROWS))
    edge = 0 if reverse else SUBLANES - 1

    def earlier(x, d, fill):
        if reverse:
            return jnp.where(row < SUBLANES - d, pltpu.roll(x, SUBLANES - d, 0), fill)
        return jnp.where(row >= d, pltpu.roll(x, d, 0), fill)

    carry = [carry_ref[k] for k in range(nslab)]
    for bi in range(nblk):
        base = ((nblk - 1 - bi) if reverse else bi) * blk
        for k in range(nslab):
            idx = [pl.ds(base + j, SUBLANES, stride=SEG_ROWS) for j in range(SEG_ROWS)]
            av = [a_ref[k, i, :] for i in idx]
            uv = [u_ref[k, i, :] for i in idx]
            h, p = uv[order[0]], av[order[0]]
            for j in order[1:]:
                h = av[j] * h + uv[j]
                p = av[j] * p
            d = 1
            while d < SUBLANES:
                h = p * earlier(h, d, 0.0) + h
                p = p * earlier(p, d, 1.0)
                d *= 2
            c_in = carry[k]
            state = earlier(h, 1, 0.0) + earlier(p, 1, 1.0) * c_in
            carry[k] = jnp.broadcast_to(h[edge:edge + 1] + p[edge:edge + 1] * c_in[edge:edge + 1], (SUBLANES, LANES))
            for j in order:
                state = av[j] * state + uv[j]
                h_ref[(*h_pre, k, idx[j], slice(None))] = state
    for k in range(nslab):
        carry_ref[k] = carry[k]


def _bw_kernel(xprev_ref, x_ref, mod_ref, nw_ref, w_ref, tab_ref, cw_ref, cb_ref, wg_ref, lp_ref, h0_ref, s0_ref,
               xc_ref, hbw_ref, tiles_ref, tok_ref, finh_ref, fins_ref,
               a_ref, u_ref, hcar_ref, s_ref, nx_ref, *, tc, nc):
    c = pl.program_id(1)
    chunk = nc - 1 - c
    dh = RET_CHUNK
    nslab = a_ref.shape[0]
    lw = nslab * LANES

    @pl.when(c == 0)
    def _():
        for k in range(nslab):
            hcar_ref[k] = jnp.broadcast_to(h0_ref[0, :, k * LANES:(k + 1) * LANES], (SUBLANES, LANES))
        s_ref[...] = s0_ref[0]
        nx_ref[...] = jnp.zeros_like(nx_ref)

    m = mod_ref[0]
    nw = nw_ref[...]
    h_cur = _norm_mod(x_ref[0], m, nw).astype(BF16)
    d = h_cur.shape[1]
    tok_ref[0, :, 0:d] = h_cur
    h_ext = jnp.concatenate([_norm_mod(xprev_ref[0], m, nw).astype(BF16), h_cur], axis=0)
    lx_ext = jnp.dot(h_ext, w_ref[:, 0:lw], preferred_element_type=F32)
    kv = jnp.dot(h_cur, w_ref[:, lw:], preferred_element_type=F32)
    lead = xprev_ref.shape[1]
    lx = lx_ext[lead:]
    lx_prev = jnp.where(chunk == 0, 0.0, lx_ext[0:lead])
    xc = _lru_conv(lx_prev, lx, nx_ref[...], cw_ref, cb_ref)
    nx_ref[...] = lx[0:nx_ref.shape[0]]
    xc_ref[0] = xc

    a, u = _lru_gates(xc, xc.astype(BF16), wg_ref, lp_ref, 1)
    _store_slabs(a_ref, a)
    _store_slabs(u_ref, u)
    _lru_scan(a_ref, u_ref, hbw_ref, (0,), hcar_ref, True, tc)

    k = kv[:, 0:RET_HEADS * dh] * dh ** -0.5
    v_bf = kv[:, RET_HEADS * dh:].astype(BF16)
    tok_ref[0, :, d:] = v_bf
    pairs = [(n, hd) for n in range(tc // RET_CHUNK) for hd in range(RET_HEADS)]
    rows = lambda n: slice(n * RET_CHUNK, (n + 1) * RET_CHUNK)
    cols = lambda hd: slice(hd * dh, (hd + 1) * dh)
    for n, hd in pairs:
        k_nh = k[rows(n), cols(hd)]
        tiles_ref[0, n, TILE_K, hd] = k_nh.T.astype(BF16)
        tiles_ref[0, n, TILE_KTAIL, hd] = (k_nh * tab_ref[TAB_TAIL_FW, hd]).T.astype(BF16)
    kvs = {(n, hd): lax.dot_general((k[rows(n), cols(hd)] * tab_ref[TAB_TAIL_BW, hd]).astype(BF16),
                                    v_bf[rows(n), cols(hd)], (((0,), (0,)), ((), ())), preferred_element_type=F32)
           for n, hd in pairs}
    for hd in range(RET_HEADS):
        s = s_ref[hd]
        for n in reversed(range(tc // RET_CHUNK)):
            tiles_ref[0, n, TILE_STATE, hd] = s.astype(BF16)
            s = tab_ref[TAB_STEP_BW, hd] * s + kvs[n, hd]
        s_ref[hd] = s

    @pl.when(c == 0)
    def _():
        for k in range(nslab):
            finh_ref[0, :, k * LANES:(k + 1) * LANES] = hbw_ref[0, k, tc - 1:tc, :]

    @pl.when(c == nc - 1)
    def _():
        fins_ref[0] = s_ref[...]


def _backward_sweep(x, mod, mod_row, nw, w_bw, tab, cw, cb, wg, lp, h0, s0, tc):
    b, t, d = x.shape
    lw = cw.shape[1]
    nc = t // tc
    nlead = tc // LEAD_ROWS
    nsub = tc // RET_CHUNK
    nslab = lw // LANES
    dh = RET_CHUNK
    ch = lambda j: nc - 1 - j
    tok = lambda dt: jax.ShapeDtypeStruct((b, t, lw), dt)
    tok_spec = pl.BlockSpec((1, tc, lw), lambda i, j: (i, ch(j), 0))
    tiles = jax.ShapeDtypeStruct((b, t // RET_CHUNK, N_TILES, RET_HEADS, dh, dh), BF16)
    tiles_spec = pl.BlockSpec((1, nsub, N_TILES, RET_HEADS, dh, dh), lambda i, j: (i, ch(j), 0, 0, 0, 0))
    return pl.pallas_call(
        functools.partial(_bw_kernel, tc=tc, nc=nc),
        grid=(b, nc),
        in_specs=[
            pl.BlockSpec((1, LEAD_ROWS, d), lambda i, j: (i, jnp.maximum(ch(j) * nlead - 1, 0), 0)),
            pl.BlockSpec((1, tc, d), lambda i, j: (i, ch(j), 0)),
            pl.BlockSpec((1, N_MOD, d), lambda i, j: (mod_row(i), 0, 0)),
            _const_spec(nw.shape), _resident(w_bw), _resident(tab), _const_spec(cw.shape), _const_spec(cb.shape),
            _resident(wg), _const_spec(lp.shape),
            pl.BlockSpec((1, 1, lw), lambda i, j: (i, 0, 0)),
            pl.BlockSpec((1, RET_HEADS, dh, dh), lambda i, j: (i, 0, 0, 0)),
        ],
        out_specs=[
            tok_spec,
            pl.BlockSpec((1, nslab, tc, LANES), lambda i, j: (i, 0, ch(j), 0)),
            tiles_spec,
            pl.BlockSpec((1, tc, d + lw), lambda i, j: (i, ch(j), 0)),
            pl.BlockSpec((1, 1, lw), lambda i, j: (i, 0, 0)),
            pl.BlockSpec((1, RET_HEADS, dh, dh), lambda i, j: (i, 0, 0, 0)),
        ],
        out_shape=[
            tok(F32),
            jax.ShapeDtypeStruct((b, nslab, t, LANES), F32),
            tiles,
            jax.ShapeDtypeStruct((b, t, d + lw), BF16),
            jax.ShapeDtypeStruct((b, 1, lw), F32),
            jax.ShapeDtypeStruct((b, RET_HEADS, dh, dh), F32),
        ],
        scratch_shapes=[pltpu.VMEM((nslab, tc, LANES), F32),
                        pltpu.VMEM((nslab, tc, LANES), F32),
                        pltpu.VMEM((nslab, SUBLANES, LANES), F32),
                        pltpu.VMEM((RET_HEADS, dh, dh), F32),
                        pltpu.VMEM((HALO_ROWS, lw), F32)],
        compiler_params=_params("arbitrary", "arbitrary"),
        name="backward_sweep",
    )(x, x, mod, nw, w_bw, tab, cw, cb, wg, lp, h0, s0)


def _fw_kernel(x_ref, mod_ref, tok_ref, w_ref, xc_ref, hbw_ref, tiles_ref,
               tab_ref, wg_ref, lp_ref, gnw_ref, wout_ref, h0_ref, s0_ref,
               o_ref, finh_ref, fins_ref, a_ref, u_ref, h_ref, hcar_ref, s_ref, y_ref, *, tc, nc):
    c = pl.program_id(1)
    dh = RET_CHUNK
    lw = xc_ref.shape[2]
    nslab = a_ref.shape[0]

    @pl.when(c == 0)
    def _():
        for k in range(nslab):
            hcar_ref[k] = jnp.broadcast_to(h0_ref[0, :, k * LANES:(k + 1) * LANES], (SUBLANES, LANES))
        s_ref[...] = s0_ref[0]

    m = mod_ref[0]
    d = x_ref.shape[2]
    h_in = tok_ref[0, :, 0:d]
    vcols = lambda hd: slice(d + hd * dh, d + (hd + 1) * dh)

    xc = xc_ref[0]
    a, u = _lru_gates(xc, xc.astype(BF16), wg_ref, lp_ref, 0)
    _store_slabs(a_ref, a)
    _store_slabs(u_ref, u)
    _lru_scan(a_ref, u_ref, h_ref, (), hcar_ref, False, tc)

    lru_g = jnp.dot(h_in, w_ref[:, 0:lw], preferred_element_type=F32)
    qg = jnp.dot(h_in, w_ref[:, lw:], preferred_element_type=F32)
    q_all = qg[:, 0:RET_HEADS * dh]
    ret_g = qg[:, RET_HEADS * dh:]
    for k in range(nslab):
        cols = slice(k * LANES, (k + 1) * LANES)
        y_ref[:, cols] = ((h_ref[k] + hbw_ref[0, k]) * _gelu_tanh(lru_g[:, cols])).astype(BF16)

    gnw = gnw_ref[...]
    pairs = [(n, hd) for n in range(tc // RET_CHUNK) for hd in range(RET_HEADS)]
    rows = lambda n: slice(n * RET_CHUNK, (n + 1) * RET_CHUNK)
    cols = lambda hd: slice(hd * dh, (hd + 1) * dh)
    kv = {(n, hd): jnp.dot(tiles_ref[0, n, TILE_KTAIL, hd], tok_ref[0, rows(n), vcols(hd)], preferred_element_type=F32)
          for n, hd in pairs}
    state = {}
    for hd in range(RET_HEADS):
        s = s_ref[hd]
        for n in range(tc // RET_CHUNK):
            state[n, hd] = s
            s = tab_ref[TAB_STEP_FW, hd] * s + kv[n, hd]
        s_ref[hd] = s
    scores = {(n, hd): jnp.dot(q_all[rows(n), cols(hd)].astype(BF16), tiles_ref[0, n, TILE_K, hd],
                               preferred_element_type=F32) for n, hd in pairs}
    intra = {(n, hd): _dot(scores[n, hd] * tab_ref[TAB_INTRA, hd], tok_ref[0, rows(n), vcols(hd)])
             for n, hd in pairs}
    out = {}
    for n, hd in pairs:
        q = q_all[rows(n), cols(hd)]
        q_both = jnp.concatenate([q * tab_ref[TAB_HEAD_FW, hd], q * tab_ref[TAB_HEAD_BW, hd]], axis=1)
        s_both = jnp.concatenate([state[n, hd].astype(BF16), tiles_ref[0, n, TILE_STATE, hd]], axis=0)
        out[n, hd] = intra[n, hd] + _dot(q_both, s_both)
    for n, hd in pairs:
        o = out[n, hd]
        dev = o - jnp.mean(o, axis=-1, keepdims=True)
        on = dev * lax.rsqrt(jnp.mean(dev * dev, axis=-1, keepdims=True) + EPS) * gnw[:, cols(hd)]
        y_ref[rows(n), lw + hd * dh:lw + (hd + 1) * dh] = (on * _silu(ret_g[rows(n), cols(hd)])).astype(BF16)

    o_ref[0] = x_ref[0] + m[2:3] * _dot(y_ref[...], wout_ref[...])

    @pl.when(c == 0)
    def _():
        for k in range(nslab):
            finh_ref[0, :, k * LANES:(k + 1) * LANES] = h_ref[k, 0:1, :]

    @pl.when(c == nc - 1)
    def _():
        fins_ref[0] = s_ref[...]


def _forward_sweep(x, mod, mod_row, tok, w_fw, xc, hbw, tiles, tab, wg, lp, gnw, wout_bf, h0, s0, tc):
    b, t, d = x.shape
    lw = xc.shape[2]
    nc = t // tc
    nsub = tc // RET_CHUNK
    nslab = lw // LANES
    dh = RET_CHUNK
    tok_spec = pl.BlockSpec((1, tc, lw), lambda i, j: (i, j, 0))
    tiles_spec = pl.BlockSpec((1, nsub, N_TILES, RET_HEADS, dh, dh), lambda i, j: (i, j, 0, 0, 0, 0))
    return pl.pallas_call(
        functools.partial(_fw_kernel, tc=tc, nc=nc),
        grid=(b, nc),
        in_specs=[
            pl.BlockSpec((1, tc, d), lambda i, j: (i, j, 0)),
            pl.BlockSpec((1, N_MOD, d), lambda i, j: (mod_row(i), 0, 0)),
            pl.BlockSpec((1, tc, d + lw), lambda i, j: (i, j, 0)), _resident(w_fw),
            tok_spec,
            pl.BlockSpec((1, nslab, tc, LANES), lambda i, j: (i, 0, j, 0)),
            tiles_spec,
            _resident(tab),
            _resident(wg), _const_spec(lp.shape), _const_spec(gnw.shape), _resident(wout_bf),
            pl.BlockSpec((1, 1, lw), lambda i, j: (i, 0, 0)),
            pl.BlockSpec((1, RET_HEADS, dh, dh), lambda i, j: (i, 0, 0, 0)),
        ],
        out_specs=[
            pl.BlockSpec((1, tc, d), lambda i, j: (i, j, 0)),
            pl.BlockSpec((1, 1, lw), lambda i, j: (i, 0, 0)),
            pl.BlockSpec((1, RET_HEADS, dh, dh), lambda i, j: (i, 0, 0, 0)),
        ],
        out_shape=[
            jax.ShapeDtypeStruct((b, t, d), F32),
            jax.ShapeDtypeStruct((b, 1, lw), F32),
            jax.ShapeDtypeStruct((b, RET_HEADS, dh, dh), F32),
        ],
        scratch_shapes=[pltpu.VMEM((nslab, tc, LANES), F32),
                        pltpu.VMEM((nslab, tc, LANES), F32),
                        pltpu.VMEM((nslab, tc, LANES), F32),
                        pltpu.VMEM((nslab, SUBLANES, LANES), F32),
                        pltpu.VMEM((RET_HEADS, dh, dh), F32),
                        pltpu.VMEM((tc, wout_bf.shape[0]), BF16)],
        compiler_params=_params("arbitrary", "arbitrary"),
        name="forward_sweep",
    )(x, mod, tok, w_fw, xc, hbw, tiles, tab, wg, lp, gnw, wout_bf, h0, s0)


def _ffn_kernel(*refs, tf, halo, period, taps, final):
    if halo:
        xp_ref, x_ref, xn_ref = refs[:3]
        refs = refs[3:]
    else:
        x_ref = refs[0]
        refs = refs[1:]
    mod_ref, n2w_ref, wg_ref, wu_ref, wd_ref, cw_ref, cb_ref, fnw_ref, o_ref = refs
    m = mod_ref[0]
    n2wm = n2w_ref[...] * (1.0 + m[4:5])

    def norm_mod(x):
        return _rms(x) * n2wm + m[3:4]

    x = x_ref[0]
    h2_cur = norm_mod(x).astype(BF16)
    h2 = h2_cur
    if halo:
        j = pl.program_id(1)
        nj = pl.num_programs(1)
        h2 = jnp.concatenate([jnp.where(j == 0, 0.0, norm_mod(xp_ref[0])).astype(BF16), h2_cur,
                              jnp.where(j == nj - 1, 0.0, norm_mod(xn_ref[0])).astype(BF16)], axis=0)
    rows = tf + 2 * halo
    nper = rows // period
    sub = lax.broadcasted_iota(jnp.int32, (nper, SUBLANES, FFN_TILE), 1)

    def neighbour(g, step):
        r3 = pltpu.roll(g, (-step) % rows, 0).reshape(nper, period, FFN_TILE)
        if step < 0:
            fixed = [jnp.where(sub == 0, 0.0, r3[:, 0:SUBLANES]), r3[:, SUBLANES:]]
        else:
            fixed = [r3[:, 0:period - SUBLANES], jnp.where(sub == SUBLANES - 1, 0.0, r3[:, period - SUBLANES:])]
        return jnp.concatenate(fixed, axis=1).reshape(rows, FFN_TILE)

    acts = []
    ntiles = wg_ref.shape[1] // FFN_TILE
    for n0 in range(0, ntiles, TILE_GROUP):
        group = range(n0, min(n0 + TILE_GROUP, ntiles))
        cs = {n: slice(n * FFN_TILE, (n + 1) * FFN_TILE) for n in group}
        g = {n: jnp.dot(h2, wg_ref[:, cs[n]], preferred_element_type=F32) for n in group}
        up = {n: jnp.dot(h2_cur, wu_ref[:, cs[n]], preferred_element_type=F32) for n in group}
        g_left = {n: neighbour(g[n], -1) for n in group}
        g_right = {n: neighbour(g[n], 1) for n in group}
        for n in group:
            cw = cw_ref[:, cs[n]]
            conv = cb_ref[:, cs[n]]
            for dr, wrow in taps:
                lo = halo + dr * period
                conv = (conv + g_left[n][lo:lo + tf] * cw[3 * wrow:3 * wrow + 1]
                        + g[n][lo:lo + tf] * cw[3 * wrow + 1:3 * wrow + 2]
                        + g_right[n][lo:lo + tf] * cw[3 * wrow + 2:3 * wrow + 3])
            acts.append((_gelu_tanh(conv) * up[n]).astype(BF16))

    out = x + m[5:6] * jnp.dot(jnp.concatenate(acts, axis=1), wd_ref[...], preferred_element_type=F32)
    o_ref[0] = _rms(out) * fnw_ref[...] if final else out


def _ffn(x, mod, mod_row, n2w, wg, wu, wd, cw, cb, fnw, latent, final):
    b, t, d = x.shape
    if latent:
        tf, halo, period = FFN_ROWS, GRID_W, GRID_W
        taps = ((-1, 0), (0, 1), (1, 2))
        nb = tf // halo
        x_specs = [pl.BlockSpec((1, halo, d), lambda i, j: (i, jnp.maximum(j * nb - 1, 0), 0)),
                   pl.BlockSpec((1, tf, d), lambda i, j: (i, j, 0)),
                   pl.BlockSpec((1, halo, d), lambda i, j: (i, jnp.minimum((j + 1) * nb, t // halo - 1), 0))]
        x_args = (x, x, x)
    else:
        tf, halo, period = t, 0, t
        taps = ((0, 1),)
        x_specs = [pl.BlockSpec((1, tf, d), lambda i, j: (i, j, 0))]
        x_args = (x,)
    assert period & (period - 1) == 0 and t % tf == 0
    return pl.pallas_call(
        functools.partial(_ffn_kernel, tf=tf, halo=halo, period=period, taps=taps, final=final),
        grid=(b, t // tf),
        in_specs=x_specs + [
            pl.BlockSpec((1, N_MOD, d), lambda i, j: (mod_row(i), 0, 0)),
            _const_spec(n2w.shape), _resident(wg), _resident(wu), _resident(wd),
            _const_spec(cw.shape), _const_spec(cb.shape), _const_spec(fnw.shape)],
        out_specs=pl.BlockSpec((1, tf, d), lambda i, j: (i, j, 0)),
        out_shape=jax.ShapeDtypeStruct((b, t, d), F32),
        compiler_params=_params("arbitrary", "arbitrary"),
        name="ffn",
    )(*x_args, mod, n2w, wg, wu, wd, cw, cb, fnw)


def _block_diag_tiles(w):
    nb, c, _ = w.shape
    per = GATE_TILE // c
    eye = jnp.eye(per, dtype=w.dtype)
    w = w.reshape(nb // per, per, c, c)
    return jnp.einsum('mjab,jk->mjakb', w, eye).reshape(nb // per, GATE_TILE, GATE_TILE)


def _layer(x, mod, mod_row, p, tab, states, latent, final):
    b, t, d = x.shape
    tc = min(t, TIME_CHUNK)
    xc, hbw, tiles, tok, fin_h_bw, fin_s_bw = _backward_sweep(
        x, mod, mod_row, p['norm1_w'], p['w_bw'], tab, p['lru_conv_w'], p['lru_conv_b'], p['wg'], p['lp'],
        states[1], states[3], tc)
    x, fin_h_fw, fin_s_fw = _forward_sweep(
        x, mod, mod_row, tok, p['w_fw'], xc, hbw, tiles, tab, p['wg'], p['lp'],
        p['ret_gn_w'], p['w_out'], states[0], states[2], tc)
    x = _ffn(x, mod, mod_row, p['norm2_w'], p['ffn_wg'], p['ffn_wu'], p['ffn_wd'], p['ffn_cw'], p['ffn_cb'],
             p['final_norm_w'], latent, final)
    return x, (fin_h_fw, fin_h_bw, fin_s_fw, fin_s_bw)


def kernel(x_prompt, x_sample, state_lru_fw, state_lru_bw, state_ret_fw, state_ret_bw, c, c_ctx,
           norm1_w, w_mod, b_mod, w_in, lru_conv_w, lru_conv_b,
           lru_wa_fw, lru_ba_fw, lru_wx_fw, lru_bx_fw, lru_lambda_fw,
           lru_wa_bw, lru_ba_bw, lru_wx_bw, lru_bx_bw, lru_lambda_bw,
           ret_decay_fw, ret_decay_bw, ret_gn_w, w_out, norm2_w,
           ffn_w_gate, ffn_w_up, ffn_conv_w, ffn_conv_b, ffn_w_down, final_norm_w):
    depth = w_in.shape[0]
    bp = x_prompt.shape[0]
    bs, d = c.shape
    lw = lru_conv_w.shape[2]
    dh = RET_CHUNK
    rw = RET_HEADS * dh

    ctx_row = bs
    rows = -(-(bs + 1) // SUBLANES) * SUBLANES
    cvec = jnp.zeros((rows, d), F32).at[:bs].set(c).at[ctx_row].set(c_ctx)
    row_ctx = lambda i: ctx_row
    row_lat = lambda i: i

    x_p, x_s = x_prompt, x_sample
    fins = []
    for l in range(depth):
        mod = _modulation(cvec, w_mod[l], b_mod[l][None]).reshape(rows, N_MOD, d)
        tab = _decay_tables(jnp.stack([ret_decay_fw[l], ret_decay_bw[l]]).astype(F32))
        p = dict(
            norm1_w=norm1_w[l][None],
            w_bw=jnp.concatenate([w_in[l][:, 0:lw], w_in[l][:, 2 * lw + rw:2 * lw + 3 * rw]], axis=1).astype(BF16),
            w_fw=jnp.concatenate([w_in[l][:, lw:2 * lw + rw], w_in[l][:, 2 * lw + 3 * rw:]], axis=1).astype(BF16),
            lru_conv_w=lru_conv_w[l], lru_conv_b=lru_conv_b[l][None],
            wg=jnp.stack([jnp.stack([_block_diag_tiles(lru_wa_fw[l]), _block_diag_tiles(lru_wx_fw[l])]),
                          jnp.stack([_block_diag_tiles(lru_wa_bw[l]), _block_diag_tiles(lru_wx_bw[l])])]
                         ).astype(BF16),
            lp=jnp.stack([jnp.stack([lru_ba_fw[l], lru_bx_fw[l], lru_lambda_fw[l]]),
                          jnp.stack([lru_ba_bw[l], lru_bx_bw[l], lru_lambda_bw[l]])]).astype(F32),
            ret_gn_w=ret_gn_w[l][None], w_out=w_out[l].astype(BF16), norm2_w=norm2_w[l][None],
            ffn_wg=ffn_w_gate[l].astype(BF16), ffn_wu=ffn_w_up[l].astype(BF16), ffn_wd=ffn_w_down[l].astype(BF16),
            ffn_cw=ffn_conv_w[l].reshape(9, -1), ffn_cb=ffn_conv_b[l][None],
            final_norm_w=final_norm_w[None],
        )
        final = l == depth - 1
        zeros_h = jnp.zeros((bp, 1, lw), F32)
        zeros_s = jnp.zeros((bp, RET_HEADS, dh, dh), F32)
        x_p, fin = _layer(x_p, mod, row_ctx, p, tab, (zeros_h, zeros_h, zeros_s, zeros_s), False, final)
        fins.append(fin)
        init_lat = (state_lru_fw[:, l][:, None].astype(F32), state_lru_bw[:, l][:, None].astype(F32),
                    state_ret_fw[:, l].astype(F32), state_ret_bw[:, l].astype(F32))
        x_s, _ = _layer(x_s, mod, row_lat, p, tab, init_lat, True, final)

    dt = x_prompt.dtype
    new_lru_fw = jnp.concatenate([f[0] for f in fins], axis=1).astype(dt)
    new_lru_bw = jnp.concatenate([f[1] for f in fins], axis=1).astype(dt)
    new_ret_fw = jnp.stack([f[2] for f in fins], axis=1).astype(dt)
    new_ret_bw = jnp.stack([f[3] for f in fins], axis=1).astype(dt)
    return (x_p, x_s, new_lru_fw, new_lru_bw, new_ret_fw, new_ret_bw)
```

```python
import functools
import math

import jax
import jax.numpy as jnp
from jax import lax
from jax.experimental import pallas as pl
from jax.experimental.pallas import tpu as pltpu

F32 = jnp.float32
BF16 = jnp.bfloat16

N_MOD = 6
EPS = 1e-6
LRU_C = 8.0
LRU_CONV_WIDTH = 4
LRU_CONV_PAD_LEFT = 2
RET_HEADS = 4
RET_CHUNK = 128
GRID_W = 64
GATE_TILE = 256
FFN_TILE = 256
TILE_GROUP = 4
SUBLANES = 8
LANES = 128
HALO_ROWS = SUBLANES
LEAD_ROWS = 2 * SUBLANES
MOD_COL_BLOCKS = 4
SEG_ROWS = 4
TIME_CHUNK = 512
FFN_ROWS = 512
VMEM_LIMIT = 56 * 1024 * 1024
def _dot(a, b):
    return jnp.dot(a.astype(BF16), b.astype(BF16), preferred_element_type=F32)


def _silu(x):
    h = 0.5 * x
    return h * jnp.tanh(h) + h


def _gelu_tanh(x):
    c = math.sqrt(2.0 / math.pi)
    return (0.5 * x) * (1.0 + jnp.tanh(x * (c + (c * 0.044715) * (x * x))))


def _softplus(x):
    return jnp.maximum(x, 0.0) + jnp.log1p(jnp.exp(-jnp.abs(x)))


def _rms(x):
    return x * lax.rsqrt(jnp.mean(x * x, axis=-1, keepdims=True) + EPS)


def _params(*sem):
    return pltpu.CompilerParams(dimension_semantics=sem, vmem_limit_bytes=VMEM_LIMIT)


def _const_spec(shape):
    nd = len(shape)
    return pl.BlockSpec(shape, lambda *_: (0,) * nd)


def _state_row(state):
    return (lambda i: i) if state.shape[0] > 1 else (lambda i: 0)


def _resident(a):
    return pl.BlockSpec(a.shape, lambda *_: (0,) * a.ndim, pipeline_mode=pl.Buffered(1))


def _mod_kernel(c_ref, w_ref, b_ref, o_ref):
    o_ref[...] = _dot(_silu(c_ref[...]), w_ref[...]) + b_ref[...]


def _modulation(cvec, w, b):
    rows, d = cvec.shape
    n = w.shape[1]
    tn = n // MOD_COL_BLOCKS
    return pl.pallas_call(
        _mod_kernel,
        grid=(n // tn,),
        in_specs=[pl.BlockSpec((rows, d), lambda j: (0, 0)),
                  pl.BlockSpec((d, tn), lambda j: (0, j)),
                  pl.BlockSpec((1, tn), lambda j: (0, j))],
        out_specs=pl.BlockSpec((rows, tn), lambda j: (0, j)),
        out_shape=jax.ShapeDtypeStruct((rows, n), F32),
        compiler_params=_params("arbitrary"),
        name="modulation",
    )(cvec, w, b)


TAB_INTRA, TAB_HEAD_FW, TAB_HEAD_BW, TAB_TAIL_FW, TAB_TAIL_BW, TAB_STEP_FW, TAB_STEP_BW = range(7)


def _tab_kernel(dec_ref, tab_ref):
    c = RET_CHUNK
    row = lax.broadcasted_iota(jnp.int32, (c, c), 0).astype(F32)
    col = lax.broadcasted_iota(jnp.int32, (c, c), 1).astype(F32)
    rel = row - col
    for h in range(RET_HEADS):
        lf = -_softplus(-jnp.full((c, c), dec_ref[0, h], F32))
        lb = -_softplus(-jnp.full((c, c), dec_ref[1, h], F32))
        tab_ref[TAB_INTRA, h] = jnp.where(rel > 0, jnp.exp(lf * rel),
                                          jnp.where(rel < 0, jnp.exp(lb * (-rel)), 2.0))
        tab_ref[TAB_HEAD_FW, h] = jnp.exp(lf * (row + 1.0))
        tab_ref[TAB_HEAD_BW, h] = jnp.exp(lb * (c - row))
        tab_ref[TAB_TAIL_FW, h] = jnp.exp(lf * (c - 1.0 - row))
        tab_ref[TAB_TAIL_BW, h] = jnp.exp(lb * row)
        tab_ref[TAB_STEP_FW, h] = jnp.exp(lf * c)
        tab_ref[TAB_STEP_BW, h] = jnp.exp(lb * c)


def _decay_tables(dec):
    return pl.pallas_call(
        _tab_kernel,
        in_specs=[pl.BlockSpec(memory_space=pltpu.SMEM)],
        out_shape=jax.ShapeDtypeStruct((7, RET_HEADS, RET_CHUNK, RET_CHUNK), F32),
        name="decay_tables",
    )(dec)


def _norm_mod(x, m, nw):
    return _rms(x) * (nw * (1.0 + m[1:2])) + m[0:1]


def _lru_conv(prev, cur, nxt, cw_ref, cb_ref):
    tc = cur.shape[0]
    lead = prev.shape[0]
    ext = jnp.concatenate([prev, cur, nxt], axis=0)
    rows = ext.shape[0]
    cw = cw_ref[...]
    acc = cb_ref[...] + cur * cw[LRU_CONV_PAD_LEFT:LRU_CONV_PAD_LEFT + 1]
    for j in range(LRU_CONV_WIDTH):
        off = j - LRU_CONV_PAD_LEFT
        if off:
            acc = acc + pltpu.roll(ext, (-off) % rows, 0)[lead:lead + tc] * cw[j:j + 1]
    return acc


def _lru_gates(xc, xc_bf, wg_ref, lp_ref, direction):
    lp = lp_ref[direction]
    width = xc.shape[1]

    def gate(g):
        parts = [jnp.dot(xc_bf[:, GATE_TILE * m:GATE_TILE * (m + 1)], wg_ref[direction, g, m],
                         preferred_element_type=F32) for m in range(width // GATE_TILE)]
        return jnp.concatenate(parts, axis=1)

    tanh_r = jnp.tanh(0.5 * gate(0) + 0.5 * lp[0:1])
    tanh_i = jnp.tanh(0.5 * gate(1) + 0.5 * lp[1:2])
    half = (-0.5 * LRU_C) * _softplus(-lp[2:3])
    log_a = half * tanh_r + half
    a = jnp.exp(log_a)
    gain2 = -jnp.tanh(log_a) * (a * a + 1.0)
    gain = jnp.where(gain2 == 0.0, 0.0, gain2 * lax.rsqrt(gain2))
    u = gain * ((0.5 * tanh_i + 0.5) * xc)
    return a, u


def _store_slabs(ref, val):
    for k in range(ref.shape[0]):
        ref[k] = val[:, k * LANES:(k + 1) * LANES]


def _lru_scan(a_ref, u_ref, h_ref, h_pre, carry_ref, reverse, tc):
    nslab = a_ref.shape[0]
    blk = SUBLANES * SEG_ROWS
    nblk = tc // blk
    row = lax.broadcasted_iota(jnp.int32, (SUBLANES, LANES), 0)
    order = tuple(range(SEG_ROWS - 1, -1, -1)) if reverse else tuple(range(SEG_ROWS))
    edge = 0 if reverse else SUBLANES - 1

    def earlier(x, d, fill):
        if reverse:
            return jnp.where(row < SUBLANES - d, pltpu.roll(x, SUBLANES - d, 0), fill)
        return jnp.where(row >= d, pltpu.roll(x, d, 0), fill)

    carry = [carry_ref[k] for k in range(nslab)]
    for bi in range(nblk):
        base = ((nblk - 1 - bi) if reverse else bi) * blk
        for k in range(nslab):
            idx = [pl.ds(base + j, SUBLANES, stride=SEG_ROWS) for j in range(SEG_ROWS)]
            av = [a_ref[k, i, :] for i in idx]
            uv = [u_ref[k, i, :] for i in idx]
            h, p = uv[order[0]], av[order[0]]
            for j in order[1:]:
                h = av[j] * h + uv[j]
                p = av[j] * p
            d = 1
            while d < SUBLANES:
                h = p * earlier(h, d, 0.0) + h
                p = p * earlier(p, d, 1.0)
                d *= 2
            c_in = carry[k]
            state = earlier(h, 1, 0.0) + earlier(p, 1, 1.0) * c_in
            carry[k] = jnp.broadcast_to(h[edge:edge + 1] + p[edge:edge + 1] * c_in[edge:edge + 1], (SUBLANES, LANES))
            for j in order:
                state = av[j] * state + uv[j]
                h_ref[(*h_pre, k, idx[j], slice(None))] = state
    for k in range(nslab):
        carry_ref[k] = carry[k]


def _bw_kernel(xprev_ref, x_ref, mod_ref, nw_ref, w_ref, tab_ref, cw_ref, cb_ref, wg_ref, lp_ref, h0_ref, s0_ref,
               xc_ref, hbw_ref, sbw_ref, kb_ref, ktf_ref, vb_ref, hin_ref, finh_ref, fins_ref,
               a_ref, u_ref, hcar_ref, s_ref, nx_ref, *, tc, nc):
    c = pl.program_id(1)
    chunk = nc - 1 - c
    dh = RET_CHUNK
    nslab = a_ref.shape[0]
    lw = nslab * LANES

    @pl.when(c == 0)
    def _():
        for k in range(nslab):
            hcar_ref[k] = jnp.broadcast_to(h0_ref[0, :, k * LANES:(k + 1) * LANES], (SUBLANES, LANES))
        s_ref[...] = s0_ref[0]
        nx_ref[...] = jnp.zeros_like(nx_ref)

    m = mod_ref[0]
    nw = nw_ref[...]
    h_cur = _norm_mod(x_ref[0], m, nw).astype(BF16)
    hin_ref[0] = h_cur
    h_ext = jnp.concatenate([_norm_mod(xprev_ref[0], m, nw).astype(BF16), h_cur], axis=0)
    lx_ext = jnp.dot(h_ext, w_ref[:, 0:lw], preferred_element_type=F32)
    kv = jnp.dot(h_cur, w_ref[:, lw:], preferred_element_type=F32)
    lead = xprev_ref.shape[1]
    lx = lx_ext[lead:]
    lx_prev = jnp.where(chunk == 0, 0.0, lx_ext[0:lead])
    xc = _lru_conv(lx_prev, lx, nx_ref[...], cw_ref, cb_ref)
    nx_ref[...] = lx[0:nx_ref.shape[0]]
    xc_ref[0] = xc

    a, u = _lru_gates(xc, xc.astype(BF16), wg_ref, lp_ref, 1)
    _store_slabs(a_ref, a)
    _store_slabs(u_ref, u)
    _lru_scan(a_ref, u_ref, hbw_ref, (0,), hcar_ref, True, tc)

    k = kv[:, 0:RET_HEADS * dh] * dh ** -0.5
    v_bf = kv[:, RET_HEADS * dh:].astype(BF16)
    vb_ref[0] = v_bf
    pairs = [(n, hd) for n in range(tc // RET_CHUNK) for hd in range(RET_HEADS)]
    rows = lambda n: slice(n * RET_CHUNK, (n + 1) * RET_CHUNK)
    cols = lambda hd: slice(hd * dh, (hd + 1) * dh)
    for n, hd in pairs:
        k_nh = k[rows(n), cols(hd)]
        kb_ref[0, n, hd] = k_nh.T.astype(BF16)
        ktf_ref[0, n, hd] = (k_nh * tab_ref[TAB_TAIL_FW, hd]).T.astype(BF16)
    kvs = {(n, hd): lax.dot_general((k[rows(n), cols(hd)] * tab_ref[TAB_TAIL_BW, hd]).astype(BF16),
                                    v_bf[rows(n), cols(hd)], (((0,), (0,)), ((), ())), preferred_element_type=F32)
           for n, hd in pairs}
    for hd in range(RET_HEADS):
        s = s_ref[hd]
        for n in reversed(range(tc // RET_CHUNK)):
            sbw_ref[0, n, hd] = s.astype(BF16)
            s = tab_ref[TAB_STEP_BW, hd] * s + kvs[n, hd]
        s_ref[hd] = s

    @pl.when(c == 0)
    def _():
        for k in range(nslab):
            finh_ref[0, :, k * LANES:(k + 1) * LANES] = hbw_ref[0, k, tc - 1:tc, :]

    @pl.when(c == nc - 1)
    def _():
        fins_ref[0] = s_ref[...]


def _backward_sweep(x, mod, mod_row, nw, w_bw, tab, cw, cb, wg, lp, h0, s0, tc):
    b, t, d = x.shape
    lw = cw.shape[1]
    nc = t // tc
    nlead = tc // LEAD_ROWS
    nsub = tc // RET_CHUNK
    nslab = lw // LANES
    dh = RET_CHUNK
    ch = lambda j: nc - 1 - j
    srow = _state_row(h0)
    tok = lambda dt: jax.ShapeDtypeStruct((b, t, lw), dt)
    tok_spec = pl.BlockSpec((1, tc, lw), lambda i, j: (i, ch(j), 0))
    sub = jax.ShapeDtypeStruct((b, t // RET_CHUNK, RET_HEADS, dh, dh), BF16)
    sub_spec = pl.BlockSpec((1, nsub, RET_HEADS, dh, dh), lambda i, j: (i, ch(j), 0, 0, 0))
    return pl.pallas_call(
        functools.partial(_bw_kernel, tc=tc, nc=nc),
        grid=(b, nc),
        in_specs=[
            pl.BlockSpec((1, LEAD_ROWS, d), lambda i, j: (i, jnp.maximum(ch(j) * nlead - 1, 0), 0)),
            pl.BlockSpec((1, tc, d), lambda i, j: (i, ch(j), 0)),
            pl.BlockSpec((1, N_MOD, d), lambda i, j: (mod_row(i), 0, 0)),
            _const_spec(nw.shape), _resident(w_bw), _resident(tab), _const_spec(cw.shape), _const_spec(cb.shape),
            _resident(wg), _const_spec(lp.shape),
            pl.BlockSpec((1, 1, lw), lambda i, j: (srow(i), 0, 0)),
            pl.BlockSpec((1, RET_HEADS, dh, dh), lambda i, j: (srow(i), 0, 0, 0)),
        ],
        out_specs=[
            tok_spec,
            pl.BlockSpec((1, nslab, tc, LANES), lambda i, j: (i, 0, ch(j), 0)),
            sub_spec, sub_spec, sub_spec, tok_spec,
            pl.BlockSpec((1, tc, d), lambda i, j: (i, ch(j), 0)),
            pl.BlockSpec((1, 1, lw), lambda i, j: (i, 0, 0)),
            pl.BlockSpec((1, RET_HEADS, dh, dh), lambda i, j: (i, 0, 0, 0)),
        ],
        out_shape=[
            tok(F32),
            jax.ShapeDtypeStruct((b, nslab, t, LANES), F32),
            sub, sub, sub, tok(BF16),
            jax.ShapeDtypeStruct((b, t, d), BF16),
            jax.ShapeDtypeStruct((b, 1, lw), F32),
            jax.ShapeDtypeStruct((b, RET_HEADS, dh, dh), F32),
        ],
        scratch_shapes=[pltpu.VMEM((nslab, tc, LANES), F32),
                        pltpu.VMEM((nslab, tc, LANES), F32),
                        pltpu.VMEM((nslab, SUBLANES, LANES), F32),
                        pltpu.VMEM((RET_HEADS, dh, dh), F32),
                        pltpu.VMEM((HALO_ROWS, lw), F32)],
        compiler_params=_params("arbitrary", "arbitrary"),
        name="backward_sweep",
    )(x, x, mod, nw, w_bw, tab, cw, cb, wg, lp, h0, s0)


def _fw_kernel(x_ref, mod_ref, hin_ref, w_ref, xc_ref, hbw_ref, sbw_ref, kb_ref, ktf_ref, vb_ref,
               tab_ref, wg_ref, lp_ref, gnw_ref, wout_ref, h0_ref, s0_ref,
               o_ref, finh_ref, fins_ref, a_ref, u_ref, h_ref, hcar_ref, s_ref, y_ref, *, tc, nc):
    c = pl.program_id(1)
    dh = RET_CHUNK
    lw = xc_ref.shape[2]
    nslab = a_ref.shape[0]

    @pl.when(c == 0)
    def _():
        for k in range(nslab):
            hcar_ref[k] = jnp.broadcast_to(h0_ref[0, :, k * LANES:(k + 1) * LANES], (SUBLANES, LANES))
        s_ref[...] = s0_ref[0]

    m = mod_ref[0]
    h_in = hin_ref[0]

    xc = xc_ref[0]
    a, u = _lru_gates(xc, xc.astype(BF16), wg_ref, lp_ref, 0)
    _store_slabs(a_ref, a)
    _store_slabs(u_ref, u)
    _lru_scan(a_ref, u_ref, h_ref, (), hcar_ref, False, tc)

    lru_g = jnp.dot(h_in, w_ref[:, 0:lw], preferred_element_type=F32)
    qg = jnp.dot(h_in, w_ref[:, lw:], preferred_element_type=F32)
    q_all = qg[:, 0:RET_HEADS * dh]
    ret_g = qg[:, RET_HEADS * dh:]
    for k in range(nslab):
        cols = slice(k * LANES, (k + 1) * LANES)
        y_ref[:, cols] = ((h_ref[k] + hbw_ref[0, k]) * _gelu_tanh(lru_g[:, cols])).astype(BF16)

    gnw = gnw_ref[...]
    pairs = [(n, hd) for n in range(tc // RET_CHUNK) for hd in range(RET_HEADS)]
    rows = lambda n: slice(n * RET_CHUNK, (n + 1) * RET_CHUNK)
    cols = lambda hd: slice(hd * dh, (hd + 1) * dh)
    kv = {(n, hd): jnp.dot(ktf_ref[0, n, hd], vb_ref[0, rows(n), cols(hd)], preferred_element_type=F32)
          for n, hd in pairs}
    state = {}
    for hd in range(RET_HEADS):
        s = s_ref[hd]
        for n in range(tc // RET_CHUNK):
            state[n, hd] = s
            s = tab_ref[TAB_STEP_FW, hd] * s + kv[n, hd]
        s_ref[hd] = s
    scores = {(n, hd): jnp.dot(q_all[rows(n), cols(hd)].astype(BF16), kb_ref[0, n, hd],
                               preferred_element_type=F32) for n, hd in pairs}
    intra = {(n, hd): _dot(scores[n, hd] * tab_ref[TAB_INTRA, hd], vb_ref[0, rows(n), cols(hd)])
             for n, hd in pairs}
    out = {}
    for n, hd in pairs:
        q = q_all[rows(n), cols(hd)]
        q_both = jnp.concatenate([q * tab_ref[TAB_HEAD_FW, hd], q * tab_ref[TAB_HEAD_BW, hd]], axis=1)
        s_both = jnp.concatenate([state[n, hd].astype(BF16), sbw_ref[0, n, hd]], axis=0)
        out[n, hd] = intra[n, hd] + _dot(q_both, s_both)
    for n, hd in pairs:
        o = out[n, hd]
        dev = o - jnp.mean(o, axis=-1, keepdims=True)
        on = dev * lax.rsqrt(jnp.mean(dev * dev, axis=-1, keepdims=True) + EPS) * gnw[:, cols(hd)]
        y_ref[rows(n), lw + hd * dh:lw + (hd + 1) * dh] = (on * _silu(ret_g[rows(n), cols(hd)])).astype(BF16)

    o_ref[0] = x_ref[0] + m[2:3] * _dot(y_ref[...], wout_ref[...])

    @pl.when(c == 0)
    def _():
        for k in range(nslab):
            finh_ref[0, :, k * LANES:(k + 1) * LANES] = h_ref[k, 0:1, :]

    @pl.when(c == nc - 1)
    def _():
        fins_ref[0] = s_ref[...]


def _forward_sweep(x, mod, mod_row, hin, w_fw, xc, hbw, sbw, kb, ktf, vb, tab, wg, lp, gnw, wout_bf, h0, s0, tc):
    b, t, d = x.shape
    lw = xc.shape[2]
    nc = t // tc
    nsub = tc // RET_CHUNK
    nslab = lw // LANES
    dh = RET_CHUNK
    srow = _state_row(h0)
    tok_spec = pl.BlockSpec((1, tc, lw), lambda i, j: (i, j, 0))
    sub_spec = pl.BlockSpec((1, nsub, RET_HEADS, dh, dh), lambda i, j: (i, j, 0, 0, 0))
    return pl.pallas_call(
        functools.partial(_fw_kernel, tc=tc, nc=nc),
        grid=(b, nc),
        in_specs=[
            pl.BlockSpec((1, tc, d), lambda i, j: (i, j, 0)),
            pl.BlockSpec((1, N_MOD, d), lambda i, j: (mod_row(i), 0, 0)),
            pl.BlockSpec((1, tc, d), lambda i, j: (i, j, 0)), _resident(w_fw),
            tok_spec,
            pl.BlockSpec((1, nslab, tc, LANES), lambda i, j: (i, 0, j, 0)),
            sub_spec, sub_spec, sub_spec, tok_spec,
            _resident(tab),
            _resident(wg), _const_spec(lp.shape), _const_spec(gnw.shape), _resident(wout_bf),
            pl.BlockSpec((1, 1, lw), lambda i, j: (srow(i), 0, 0)),
            pl.BlockSpec((1, RET_HEADS, dh, dh), lambda i, j: (srow(i), 0, 0, 0)),
        ],
        out_specs=[
            pl.BlockSpec((1, tc, d), lambda i, j: (i, j, 0)),
            pl.BlockSpec((1, 1, lw), lambda i, j: (i, 0, 0)),
            pl.BlockSpec((1, RET_HEADS, dh, dh), lambda i, j: (i, 0, 0, 0)),
        ],
        out_shape=[
            jax.ShapeDtypeStruct((b, t, d), F32),
            jax.ShapeDtypeStruct((b, 1, lw), F32),
            jax.ShapeDtypeStruct((b, RET_HEADS, dh, dh), F32),
        ],
        scratch_shapes=[pltpu.VMEM((nslab, tc, LANES), F32),
                        pltpu.VMEM((nslab, tc, LANES), F32),
                        pltpu.VMEM((nslab, tc, LANES), F32),
                        pltpu.VMEM((nslab, SUBLANES, LANES), F32),
                        pltpu.VMEM((RET_HEADS, dh, dh), F32),
                        pltpu.VMEM((tc, wout_bf.shape[0]), BF16)],
        compiler_params=_params("arbitrary", "arbitrary"),
        name="forward_sweep",
    )(x, mod, hin, w_fw, xc, hbw, sbw, kb, ktf, vb, tab, wg, lp, gnw, wout_bf, h0, s0)


def _ffn_kernel(*refs, tf, halo, period, taps, final):
    if halo:
        xp_ref, x_ref, xn_ref = refs[:3]
        refs = refs[3:]
    else:
        x_ref = refs[0]
        refs = refs[1:]
    mod_ref, n2w_ref, wg_ref, wu_ref, wd_ref, cw_ref, cb_ref, fnw_ref, o_ref = refs
    m = mod_ref[0]
    n2wm = n2w_ref[...] * (1.0 + m[4:5])

    def norm_mod(x):
        return _rms(x) * n2wm + m[3:4]

    x = x_ref[0]
    h2_cur = norm_mod(x).astype(BF16)
    h2 = h2_cur
    if halo:
        j = pl.program_id(1)
        nj = pl.num_programs(1)
        h2 = jnp.concatenate([jnp.where(j == 0, 0.0, norm_mod(xp_ref[0])).astype(BF16), h2_cur,
                              jnp.where(j == nj - 1, 0.0, norm_mod(xn_ref[0])).astype(BF16)], axis=0)
    rows = tf + 2 * halo
    nper = rows // period
    sub = lax.broadcasted_iota(jnp.int32, (nper, SUBLANES, FFN_TILE), 1)

    def neighbour(g, step):
        r3 = pltpu.roll(g, (-step) % rows, 0).reshape(nper, period, FFN_TILE)
        if step < 0:
            fixed = [jnp.where(sub == 0, 0.0, r3[:, 0:SUBLANES]), r3[:, SUBLANES:]]
        else:
            fixed = [r3[:, 0:period - SUBLANES], jnp.where(sub == SUBLANES - 1, 0.0, r3[:, period - SUBLANES:])]
        return jnp.concatenate(fixed, axis=1).reshape(rows, FFN_TILE)

    acts = []
    ntiles = wg_ref.shape[1] // FFN_TILE
    for n0 in range(0, ntiles, TILE_GROUP):
        group = range(n0, min(n0 + TILE_GROUP, ntiles))
        cs = {n: slice(n * FFN_TILE, (n + 1) * FFN_TILE) for n in group}
        g = {n: jnp.dot(h2, wg_ref[:, cs[n]], preferred_element_type=F32) for n in group}
        up = {n: jnp.dot(h2_cur, wu_ref[:, cs[n]], preferred_element_type=F32) for n in group}
        g_left = {n: neighbour(g[n], -1) for n in group}
        g_right = {n: neighbour(g[n], 1) for n in group}
        for n in group:
            cw = cw_ref[:, cs[n]]
            conv = cb_ref[:, cs[n]]
            for dr, wrow in taps:
                lo = halo + dr * period
                conv = (conv + g_left[n][lo:lo + tf] * cw[3 * wrow:3 * wrow + 1]
                        + g[n][lo:lo + tf] * cw[3 * wrow + 1:3 * wrow + 2]
                        + g_right[n][lo:lo + tf] * cw[3 * wrow + 2:3 * wrow + 3])
            acts.append((_gelu_tanh(conv) * up[n]).astype(BF16))

    out = x + m[5:6] * jnp.dot(jnp.concatenate(acts, axis=1), wd_ref[...], preferred_element_type=F32)
    o_ref[0] = _rms(out) * fnw_ref[...] if final else out


def _ffn(x, mod, mod_row, n2w, wg, wu, wd, cw, cb, fnw, latent, final):
    b, t, d = x.shape
    if latent:
        tf, halo, period = FFN_ROWS, GRID_W, GRID_W
        taps = ((-1, 0), (0, 1), (1, 2))
        nb = tf // halo
        x_specs = [pl.BlockSpec((1, halo, d), lambda i, j: (i, jnp.maximum(j * nb - 1, 0), 0)),
                   pl.BlockSpec((1, tf, d), lambda i, j: (i, j, 0)),
                   pl.BlockSpec((1, halo, d), lambda i, j: (i, jnp.minimum((j + 1) * nb, t // halo - 1), 0))]
        x_args = (x, x, x)
    else:
        tf, halo, period = t, 0, t
        taps = ((0, 1),)
        x_specs = [pl.BlockSpec((1, tf, d), lambda i, j: (i, j, 0))]
        x_args = (x,)
    assert period & (period - 1) == 0 and t % tf == 0
    return pl.pallas_call(
        functools.partial(_ffn_kernel, tf=tf, halo=halo, period=period, taps=taps, final=final),
        grid=(b, t // tf),
        in_specs=x_specs + [
            pl.BlockSpec((1, N_MOD, d), lambda i, j: (mod_row(i), 0, 0)),
            _const_spec(n2w.shape), _resident(wg), _resident(wu), _resident(wd),
            _const_spec(cw.shape), _const_spec(cb.shape), _const_spec(fnw.shape)],
        out_specs=pl.BlockSpec((1, tf, d), lambda i, j: (i, j, 0)),
        out_shape=jax.ShapeDtypeStruct((b, t, d), F32),
        compiler_params=_params("arbitrary", "arbitrary"),
        name="ffn",
    )(*x_args, mod, n2w, wg, wu, wd, cw, cb, fnw)


def _block_diag_tiles(w):
    nb, c, _ = w.shape
    per = GATE_TILE // c
    eye = jnp.eye(per, dtype=w.dtype)
    w = w.reshape(nb // per, per, c, c)
    return jnp.einsum('mjab,jk->mjakb', w, eye).reshape(nb // per, GATE_TILE, GATE_TILE)


def _layer(x, mod, mod_row, p, tab, states, latent, final):
    b, t, d = x.shape
    tc = min(t, TIME_CHUNK)
    xc, hbw, sbw, kb, ktf, vb, hin, fin_h_bw, fin_s_bw = _backward_sweep(
        x, mod, mod_row, p['norm1_w'], p['w_bw'], tab, p['lru_conv_w'], p['lru_conv_b'], p['wg'], p['lp'],
        states[1], states[3], tc)
    x, fin_h_fw, fin_s_fw = _forward_sweep(
        x, mod, mod_row, hin, p['w_fw'], xc, hbw, sbw, kb, ktf, vb, tab, p['wg'], p['lp'],
        p['ret_gn_w'], p['w_out'], states[0], states[2], tc)
    x = _ffn(x, mod, mod_row, p['norm2_w'], p['ffn_wg'], p['ffn_wu'], p['ffn_wd'], p['ffn_cw'], p['ffn_cb'],
             p['final_norm_w'], latent, final)
    return x, (fin_h_fw, fin_h_bw, fin_s_fw, fin_s_bw)


def kernel(x_prompt, x_sample, state_lru_fw, state_lru_bw, state_ret_fw, state_ret_bw, c, c_ctx,
           norm1_w, w_mod, b_mod, w_in, lru_conv_w, lru_conv_b,
           lru_wa_fw, lru_ba_fw, lru_wx_fw, lru_bx_fw, lru_lambda_fw,
           lru_wa_bw, lru_ba_bw, lru_wx_bw, lru_bx_bw, lru_lambda_bw,
           ret_decay_fw, ret_decay_bw, ret_gn_w, w_out, norm2_w,
           ffn_w_gate, ffn_w_up, ffn_conv_w, ffn_conv_b, ffn_w_down, final_norm_w):
    depth = w_in.shape[0]
    bp = x_prompt.shape[0]
    bs, d = c.shape
    lw = lru_conv_w.shape[2]
    dh = RET_CHUNK
    rw = RET_HEADS * dh

    ctx_row = bs
    rows = -(-(bs + 1) // SUBLANES) * SUBLANES
    cvec = jnp.zeros((rows, d), F32).at[:bs].set(c).at[ctx_row].set(c_ctx)
    row_ctx = lambda i: ctx_row
    row_lat = lambda i: i

    x_p, x_s = x_prompt, x_sample
    fins = []
    for l in range(depth):
        mod = _modulation(cvec, w_mod[l], b_mod[l][None]).reshape(rows, N_MOD, d)
        tab = _decay_tables(jnp.stack([ret_decay_fw[l], ret_decay_bw[l]]).astype(F32))
        p = dict(
            norm1_w=norm1_w[l][None],
            w_bw=jnp.concatenate([w_in[l][:, 0:lw], w_in[l][:, 2 * lw + rw:2 * lw + 3 * rw]], axis=1).astype(BF16),
            w_fw=jnp.concatenate([w_in[l][:, lw:2 * lw + rw], w_in[l][:, 2 * lw + 3 * rw:]], axis=1).astype(BF16),
            lru_conv_w=lru_conv_w[l], lru_conv_b=lru_conv_b[l][None],
            wg=jnp.stack([jnp.stack([_block_diag_tiles(lru_wa_fw[l]), _block_diag_tiles(lru_wx_fw[l])]),
                          jnp.stack([_block_diag_tiles(lru_wa_bw[l]), _block_diag_tiles(lru_wx_bw[l])])]
                         ).astype(BF16),
            lp=jnp.stack([jnp.stack([lru_ba_fw[l], lru_bx_fw[l], lru_lambda_fw[l]]),
                          jnp.stack([lru_ba_bw[l], lru_bx_bw[l], lru_lambda_bw[l]])]).astype(F32),
            ret_gn_w=ret_gn_w[l][None], w_out=w_out[l].astype(BF16), norm2_w=norm2_w[l][None],
            ffn_wg=ffn_w_gate[l].astype(BF16), ffn_wu=ffn_w_up[l].astype(BF16), ffn_wd=ffn_w_down[l].astype(BF16),
            ffn_cw=ffn_conv_w[l].reshape(9, -1), ffn_cb=ffn_conv_b[l][None],
            final_norm_w=final_norm_w[None],
        )
        final = l == depth - 1
        zeros_h = jnp.zeros((1, 1, lw), F32)
        zeros_s = jnp.zeros((1, RET_HEADS, dh, dh), F32)
        x_p, fin = _layer(x_p, mod, row_ctx, p, tab, (zeros_h, zeros_h, zeros_s, zeros_s), False, final)
        fins.append(fin)
        init_lat = (state_lru_fw[:, l][:, None].astype(F32), state_lru_bw[:, l][:, None].astype(F32),
                    state_ret_fw[:, l].astype(F32), state_ret_bw[:, l].astype(F32))
        x_s, _ = _layer(x_s, mod, row_lat, p, tab, init_lat, True, final)

    dt = x_prompt.dtype
    new_lru_fw = jnp.concatenate([f[0] for f in fins], axis=1).astype(dt)
    new_lru_bw = jnp.concatenate([f[1] for f in fins], axis=1).astype(dt)
    new_ret_fw = jnp.stack([f[2] for f in fins], axis=1).astype(dt)
    new_ret_bw = jnp.stack([f[3] for f in fins], axis=1).astype(dt)
    return (x_p, x_s, new_lru_fw, new_lru_bw, new_ret_fw, new_ret_bw)
```
